```python
import math
import jax, jax.numpy as jnp
from jax import lax
import numpy as np

D_MODEL = 2048
BATCH = 16
SEQ = 2048
DEPTH = 1

CHUNK = 64
PLE_DIM = 256
EPS = 1e-6
GLA_HEADS = 4
GLA_DK = D_MODEL // (2 * GLA_HEADS)
GLA_DV = D_MODEL // GLA_HEADS
GLA_LOWRANK = 16
GLA_TAU = 16.0
DN_HEADS = 16
DN_DK = D_MODEL // DN_HEADS
DN_DV = D_MODEL // DN_HEADS
DN_CONV = 4
D_FF = 4 * D_MODEL

GLA_QK = GLA_HEADS * GLA_DK
GLA_V = GLA_HEADS * GLA_DV
DN_QK = DN_HEADS * DN_DK
DN_V = DN_HEADS * DN_DV
DN_QKV = 2 * DN_QK + DN_V
IN_SPLITS = (GLA_QK, GLA_QK, GLA_V, GLA_V, GLA_LOWRANK, DN_QKV, DN_V, DN_HEADS, DN_HEADS, D_MODEL, D_MODEL)
D_IN = 2 * GLA_QK + 2 * GLA_V + GLA_LOWRANK + DN_QKV + DN_V + 2 * DN_HEADS + 2 * D_MODEL

kernel_name = "hybrid_gla_gated_deltanet_block"


def rms_norm(x, g):
    xf = x.astype(jnp.float32)
    y = xf * lax.rsqrt(jnp.mean(xf * xf, axis=-1, keepdims=True) + EPS)
    return (y * g.astype(jnp.float32)).astype(x.dtype)


def head_rms_norm(o, g):
    return o * lax.rsqrt(jnp.mean(o * o, axis=-1, keepdims=True) + EPS) * g.astype(jnp.float32)


def l2_normalize(t):
    return t * lax.rsqrt(jnp.sum(t * t, axis=-1, keepdims=True) + EPS)


def split_cols(z, sizes):
    out, off = [], 0
    for s in sizes:
        out.append(z[..., off:off + s])
        off += s
    return out


def to_chunks(t, n_heads):
    b, s, _ = t.shape
    return t.astype(jnp.float32).reshape(b, s // CHUNK, CHUNK, n_heads, -1).transpose(0, 3, 1, 2, 4)


def heads_to_chunks(t):
    b, s, h = t.shape
    return t.astype(jnp.float32).reshape(b, s // CHUNK, CHUNK, h).transpose(0, 3, 1, 2)


def from_chunks(t):
    b, h, nc, c, d = t.shape
    return t.transpose(0, 2, 3, 1, 4).reshape(b, nc * c, h * d)


def causal_depthwise_conv(x, w):
    k, c = w.shape
    return lax.conv_general_dilated(x, w[:, None, :].astype(x.dtype), window_strides=(1,),
                                    padding=[(k - 1, 0)], dimension_numbers=('NWC', 'WIO', 'NWC'),
                                    feature_group_count=c)


def chunk_major(t):
    return jnp.moveaxis(t, 2, 0)


def gla_mixer(q, k, v, log_f):
    b, h, nc, c, dk = q.shape
    dv = v.shape[-1]
    bcum = jnp.cumsum(log_f, axis=3)
    q_in = q * jnp.exp(bcum)
    k_in = k * jnp.exp(-bcum)
    causal = jnp.tril(jnp.ones((c, c), dtype=bool))
    a = jnp.where(causal, jnp.einsum('bhncd,bhnsd->bhncs', q_in, k_in), 0.0)
    o_intra = jnp.einsum('bhncs,bhnsv->bhncv', a, v)
    b_last = bcum[:, :, :, -1, :]
    k_dec = k * jnp.exp(b_last[:, :, :, None, :] - bcum)

    def step(state, xs):
        q_c, k_c, v_c, f_last = xs
        o = jnp.einsum('bhcd,bhdv->bhcv', q_c, state)
        state = state * f_last[..., None] + jnp.einsum('bhcd,bhcv->bhdv', k_c, v_c)
        return state, o

    s0 = jnp.zeros((b, h, dk, dv), jnp.float32)
    _, o_inter = lax.scan(step, s0, (chunk_major(q_in), chunk_major(k_dec), chunk_major(v),
                                     chunk_major(jnp.exp(b_last))))
    return o_intra + jnp.moveaxis(o_inter, 0, 2)


def gated_delta_mixer(q, k, v, g, beta):
    b, h, nc, c, dk = q.shape
    dv = v.shape[-1]
    gcum = jnp.cumsum(g, axis=-1)
    incl = jnp.tril(jnp.ones((c, c), dtype=bool))
    strict = jnp.tril(jnp.ones((c, c), dtype=bool), -1)
    decay = jnp.exp(jnp.where(incl, gcum[..., :, None] - gcum[..., None, :], -jnp.inf))
    k_beta = k * beta[..., None]
    a = jnp.where(strict, jnp.einsum('bhncd,bhnsd->bhncs', k_beta, k) * decay, 0.0)
    eye = jnp.eye(c, dtype=jnp.float32)
    t_mat = lax.linalg.triangular_solve(eye + a, jnp.broadcast_to(eye, a.shape), left_side=True,
                                        lower=True, unit_diagonal=True)
    u = jnp.einsum('bhncs,bhnsv->bhncv', t_mat, v * beta[..., None])
    w = jnp.einsum('bhncs,bhnsd->bhncd', t_mat, k_beta * jnp.exp(gcum)[..., None])
    attn = jnp.where(incl, jnp.einsum('bhncd,bhnsd->bhncs', q, k) * decay, 0.0)
    q_dec = q * jnp.exp(gcum)[..., None]
    g_last = gcum[..., -1]
    k_dec = k * jnp.exp(g_last[..., None] - gcum)[..., None]

    def step(state, xs):
        q_c, k_c, u_c, w_c, attn_c, f_last = xs
        v_new = u_c - jnp.einsum('bhcd,bhdv->bhcv', w_c, state)
        o = jnp.einsum('bhcd,bhdv->bhcv', q_c, state) + jnp.einsum('bhcs,bhsv->bhcv', attn_c, v_new)
        state = state * f_last[..., None, None] + jnp.einsum('bhcd,bhcv->bhdv', k_c, v_new)
        return state, o

    s0 = jnp.zeros((b, h, dk, dv), jnp.float32)
    _, o = lax.scan(step, s0, (chunk_major(q_dec), chunk_major(k_dec), chunk_major(u), chunk_major(w),
                               chunk_major(attn), chunk_major(jnp.exp(g_last))))
    return jnp.moveaxis(o, 0, 2)


def hybrid_layer(x, p_i, g_mix, w_in, gla_w2, gla_b, gla_norm, dn_conv, dn_a_log, dn_dt_bias, dn_norm,
                 w_out, g_mlp, w_up, w_down, g_ple, w_ple_gate, w_ple_proj):
    f32 = jnp.float32
    h = rms_norm(x, g_mix)
    z = h @ w_in
    (gla_q, gla_k, gla_v, gla_g, gla_lr, dn_qkv, dn_z, dn_a, dn_b, gate_a, gate_b) = split_cols(z, IN_SPLITS)

    log_f = jax.nn.log_sigmoid((gla_lr @ gla_w2 + gla_b).astype(f32)) / GLA_TAU
    o_gla = gla_mixer(to_chunks(gla_q, GLA_HEADS) * (GLA_DK ** -0.5), to_chunks(gla_k, GLA_HEADS),
                      to_chunks(gla_v, GLA_HEADS), to_chunks(log_f, GLA_HEADS))
    o_gla = from_chunks(head_rms_norm(o_gla, gla_norm)) * jax.nn.silu(gla_g.astype(f32))

    qkv = jax.nn.silu(causal_depthwise_conv(dn_qkv, dn_conv))
    dq, dk, dv = split_cols(qkv, (DN_QK, DN_QK, DN_V))
    dq = l2_normalize(to_chunks(dq, DN_HEADS)) * (DN_DK ** -0.5)
    dk = l2_normalize(to_chunks(dk, DN_HEADS))
    a_neg = -jnp.exp(dn_a_log.astype(f32))[:, None, None]
    g = a_neg * jax.nn.softplus(heads_to_chunks(dn_a) + dn_dt_bias.astype(f32)[:, None, None])
    beta = jax.nn.sigmoid(heads_to_chunks(dn_b))
    o_dn = gated_delta_mixer(dq, dk, to_chunks(dv, DN_HEADS), g, beta)
    o_dn = from_chunks(head_rms_norm(o_dn, dn_norm)) * jax.nn.silu(dn_z.astype(f32))

    mixed = (jax.nn.sigmoid(gate_a.astype(f32)) * o_gla + jax.nn.sigmoid(gate_b.astype(f32)) * o_dn).astype(x.dtype)
    x = x + mixed @ w_out

    h2 = rms_norm(x, g_mlp)
    x = x + jnp.square(jax.nn.relu(h2 @ w_up)) @ w_down

    h3 = rms_norm(x, g_ple)
    x = x + jax.nn.sigmoid(h3 @ w_ple_gate) * (p_i @ w_ple_proj)
    return x


def setup_inputs(seed: int = 0) -> dict:
    key = jax.random.key(seed)
    ks = jax.random.split(key, 24)
    f32 = jnp.float32

    def nrm(k, shape, scale):
        return jax.random.normal(k, shape, f32) * scale

    def gain(k, shape):
        return 1.0 + 0.02 * jax.random.normal(k, shape, f32)

    dt = jnp.exp(jax.random.uniform(ks[10], (DEPTH, DN_HEADS), f32, math.log(1e-3), math.log(1e-1)))
    return {
        "x": nrm(ks[0], (BATCH, SEQ, D_MODEL), 1.0),
        "p": nrm(ks[1], (DEPTH, BATCH, SEQ, PLE_DIM), 1.0),
        "g_mix": gain(ks[2], (DEPTH, D_MODEL)),
        "w_in": nrm(ks[3], (DEPTH, D_MODEL, D_IN), D_MODEL ** -0.5),
        "gla_w2": nrm(ks[4], (DEPTH, GLA_LOWRANK, GLA_QK), GLA_LOWRANK ** -0.5),
        "gla_b": nrm(ks[5], (DEPTH, GLA_QK), 0.1),
        "gla_norm": gain(ks[6], (DEPTH, GLA_DV)),
        "dn_conv": nrm(ks[7], (DEPTH, DN_CONV, DN_QKV), DN_CONV ** -0.5),
        "dn_a_log": jnp.log(jax.random.uniform(ks[8], (DEPTH, DN_HEADS), f32, 1.0, 16.0)),
        "dn_dt_bias": dt + jnp.log(-jnp.expm1(-dt)),
        "dn_norm": gain(ks[9], (DEPTH, DN_DV)),
        "w_out": nrm(ks[11], (DEPTH, D_MODEL, D_MODEL), D_MODEL ** -0.5),
        "g_mlp": gain(ks[12], (DEPTH, D_MODEL)),
        "w_up": nrm(ks[13], (DEPTH, D_MODEL, D_FF), D_MODEL ** -0.5),
        "w_down": nrm(ks[14], (DEPTH, D_FF, D_MODEL), D_FF ** -0.5),
        "g_ple": gain(ks[15], (DEPTH, D_MODEL)),
        "w_ple_gate": nrm(ks[16], (DEPTH, D_MODEL, D_MODEL), D_MODEL ** -0.5),
        "w_ple_proj": nrm(ks[17], (DEPTH, PLE_DIM, D_MODEL), PLE_DIM ** -0.5),
        "g_final": gain(ks[18], (D_MODEL,)),
    }


def reference(x, p, g_mix, w_in, gla_w2, gla_b, gla_norm, dn_conv, dn_a_log, dn_dt_bias, dn_norm,
              w_out, g_mlp, w_up, w_down, g_ple, w_ple_gate, w_ple_proj, g_final):
    for i in range(DEPTH):
        x = hybrid_layer(x, p[i], g_mix[i], w_in[i], gla_w2[i], gla_b[i], gla_norm[i], dn_conv[i],
                         dn_a_log[i], dn_dt_bias[i], dn_norm[i], w_out[i], g_mlp[i], w_up[i], w_down[i],
                         g_ple[i], w_ple_gate[i], w_ple_proj[i])
    return rms_norm(x, g_final)
```

```python
import functools

import jax
import jax.numpy as jnp
from jax import lax
from jax.experimental import pallas as pl
from jax.experimental.pallas import tpu as pltpu

F32 = jnp.float32
BF16 = jnp.bfloat16

EPS = 1e-6
CHUNK = 64
GLA_HEADS = 4
GLA_LOWRANK = 16
GLA_TAU = 16.0
DN_HEADS = 16
DN_CONV = 4
LANES = 128
SUPER = 256
VMEM_LIMIT = 56 * 1024 * 1024

SM_LR = 0
SM_A = GLA_LOWRANK
SM_B = SM_A + DN_HEADS
SM_GL = SM_B + DN_HEADS


def _cparams(sem):
    return pltpu.CompilerParams(dimension_semantics=sem, vmem_limit_bytes=VMEM_LIMIT)


def _dot(a, b):
    return jnp.dot(a, b, preferred_element_type=F32)


def _dot_nt(a, b):
    return lax.dot_general(a, b, (((1,), (1,)), ((), ())), preferred_element_type=F32)


def _dot_tn(a, b):
    return lax.dot_general(a, b, (((0,), (0,)), ((), ())), preferred_element_type=F32)


def _sigmoid(x):
    return 1.0 / (1.0 + jnp.exp(-x))


def _silu(x):
    return x * _sigmoid(x)


def _log_sigmoid(x):
    return jnp.minimum(x, 0.0) - jnp.log(1.0 + jnp.exp(-jnp.abs(x)))


def _softplus(x):
    return jnp.maximum(x, 0.0) + jnp.log(1.0 + jnp.exp(-jnp.abs(x)))


def _rms(x, g):
    return x * lax.rsqrt(jnp.mean(x * x, axis=-1, keepdims=True) + EPS) * g


def _chunk_cumsum(x, row):
    n = x.shape[0]
    shift = 1
    while shift < CHUNK:
        x = x + jnp.where(row >= shift, pltpu.roll(x, shift, 0), 0.0)
        shift *= 2
    del n
    return x


def _chunk_rev_cumsum(x, row):
    n = x.shape[0]
    shift = 1
    while shift < CHUNK:
        x = x + jnp.where(row < CHUNK - shift, pltpu.roll(x, n - shift, 0), 0.0)
        shift *= 2
    return x


def _in_proj_body(n_plain, n_silu, x_ref, g_ref, wb_ref, ws_ref, z_ref, zs_ref, h_ref):
    j = pl.program_id(1)

    @pl.when(j == 0)
    def _():
        hb = _rms(x_ref[...], g_ref[...]).astype(BF16)
        h_ref[...] = hb
        zs_ref[...] = _dot(hb, ws_ref[...])

    @pl.when(j < n_plain)
    def _():
        z_ref[...] = _dot(h_ref[...], wb_ref[...]).astype(BF16)

    @pl.when((j >= n_plain) & (j < n_plain + n_silu))
    def _():
        z_ref[...] = _silu(_dot(h_ref[...], wb_ref[...])).astype(BF16)

    @pl.when(j >= n_plain + n_silu)
    def _():
        z_ref[...] = _sigmoid(_dot(h_ref[...], wb_ref[...])).astype(BF16)


def _in_proj(x2d, g_mix, w_big, w_small, n_plain_cols, n_silu_cols, tm=1024, tn=1024):
    t, d = x2d.shape
    n = w_big.shape[1]
    body = functools.partial(_in_proj_body, n_plain_cols // tn, n_silu_cols // tn)
    return pl.pallas_call(
        body,
        grid=(t // tm, n // tn),
        in_specs=[
            pl.BlockSpec((tm, d), lambda i, j: (i, 0)),
            pl.BlockSpec((1, d), lambda i, j: (0, 0)),
            pl.BlockSpec((d, tn), lambda i, j: (0, j)),
            pl.BlockSpec((d, LANES), lambda i, j: (0, 0)),
        ],
        out_specs=[
            pl.BlockSpec((tm, tn), lambda i, j: (i, j)),
            pl.BlockSpec((tm, LANES), lambda i, j: (i, 0)),
        ],
        out_shape=[
            jax.ShapeDtypeStruct((t, n), BF16),
            jax.ShapeDtypeStruct((t, LANES), F32),
        ],
        scratch_shapes=[pltpu.VMEM((tm, d), BF16)],
        compiler_params=_cparams(("parallel", "arbitrary")),
        name="in_proj",
    )(x2d, g_mix, w_big, w_small)


def _gates_body(zs_ref, aneg_ref, dtb_ref, g_ref, gt_ref):
    zs = zs_ref[...]
    s = zs.shape[0]
    lane = lax.broadcasted_iota(jnp.int32, zs.shape, 1)
    row = lax.broadcasted_iota(jnp.int32, zs.shape, 0) % CHUNK
    is_a = (lane >= SM_A) & (lane < SM_B)
    is_b = (lane >= SM_B) & (lane < SM_GL)
    g = jnp.where(is_a, aneg_ref[...] * _softplus(zs + dtb_ref[...]), 0.0)
    gcum = _chunk_cumsum(g, row)
    gtot = gcum + _chunk_rev_cumsum(g, row) - g
    beta = _sigmoid(zs)
    out = jnp.where(is_a, gcum, jnp.where(is_b, beta, 0.0))
    out = out + pltpu.roll(jnp.where(is_a, gtot, 0.0), SM_GL - SM_A, 1)
    del s
    g_ref[...] = out
    gt_ref[...] = out.T


def _gates(zs3, a_neg, dt_bias):
    b, s, _ = zs3.shape
    return pl.pallas_call(
        _gates_body,
        grid=(b,),
        in_specs=[
            pl.BlockSpec((None, s, LANES), lambda i: (i, 0, 0)),
            pl.BlockSpec((1, LANES), lambda i: (0, 0)),
            pl.BlockSpec((1, LANES), lambda i: (0, 0)),
        ],
        out_specs=[
            pl.BlockSpec((None, s, LANES), lambda i: (i, 0, 0)),
            pl.BlockSpec((None, LANES, s), lambda i: (i, 0, 0)),
        ],
        out_shape=[
            jax.ShapeDtypeStruct((b, s, LANES), F32),
            jax.ShapeDtypeStruct((b, LANES, s), F32),
        ],
        compiler_params=_cparams(("parallel",)),
        name="dn_gates",
    )(zs3, a_neg, dt_bias)


def _gla_body(scale, q_ref, k_ref, v_ref, lr_ref, w2_ref, b_ref, nrm_ref, o_ref, st_ref):
    s = q_ref.shape[0]
    dk = q_ref.shape[1]
    st_ref[...] = jnp.zeros_like(st_ref)
    row = lax.broadcasted_iota(jnp.int32, (CHUNK, dk), 0)
    ri = lax.broadcasted_iota(jnp.int32, (CHUNK, CHUNK), 0)
    ci = lax.broadcasted_iota(jnp.int32, (CHUNK, CHUNK), 1)
    causal = ci <= ri

    def chunk(c, carry):
        r = pl.ds(pl.multiple_of(c * CHUNK, CHUNK), CHUNK)
        q = q_ref[r, :].astype(F32) * scale
        k = k_ref[r, :].astype(F32)
        v = v_ref[r, :]
        pre = _dot(lr_ref[r, :].astype(BF16), w2_ref[...]) + b_ref[...]
        bc = _chunk_cumsum(_log_sigmoid(pre) * (1.0 / GLA_TAU), row)
        b_last = bc[CHUNK - 1:CHUNK, :]
        q_in = (q * jnp.exp(bc)).astype(BF16)
        k_in = (k * jnp.exp(-bc)).astype(BF16)
        k_dec = (k * jnp.exp(b_last - bc)).astype(BF16)
        a = jnp.where(causal, _dot_nt(q_in, k_in), 0.0).astype(BF16)
        st = st_ref[...]
        o = _dot(a, v) + _dot_nt(q_in, st.astype(BF16))
        st_ref[...] = st * jnp.exp(b_last) + _dot_tn(v, k_dec)
        o_ref[r, :] = _rms(o, nrm_ref[...]).astype(o_ref.dtype)
        return carry

    lax.fori_loop(0, s // CHUNK, chunk, 0)


def _gla(z3, zs3, w2p, gla_b, gla_norm, d_model):
    b, s, _ = z3.shape
    dk = d_model // (2 * GLA_HEADS)
    dv = d_model // GLA_HEADS
    qk_cols = GLA_HEADS * dk
    body = functools.partial(_gla_body, dk ** -0.5)
    return pl.pallas_call(
        body,
        grid=(b, GLA_HEADS),
        in_specs=[
            pl.BlockSpec((None, s, dk), lambda i, h: (i, 0, h)),
            pl.BlockSpec((None, s, dk), lambda i, h: (i, 0, qk_cols // dk + h)),
            pl.BlockSpec((None, s, dv), lambda i, h: (i, 0, 2 * qk_cols // dv + h)),
            pl.BlockSpec((None, s, LANES), lambda i, h: (i, 0, 0)),
            pl.BlockSpec((LANES, dk), lambda i, h: (0, h)),
            pl.BlockSpec((1, dk), lambda i, h: (0, h)),
            pl.BlockSpec((1, dv), lambda i, h: (0, 0)),
        ],
        out_specs=pl.BlockSpec((None, s, dv), lambda i, h: (i, 0, h)),
        out_shape=jax.ShapeDtypeStruct((b, s, GLA_HEADS * dv), BF16),
        scratch_shapes=[pltpu.VMEM((dv, dk), F32)],
        compiler_params=_cparams(("parallel", "parallel")),
        name="gla_mixer",
    )(z3, z3, z3, zs3, w2p, gla_b, gla_norm)


def _conv_silu(x_ref, w_ref, lo, s_idx, first):
    base = pl.multiple_of(s_idx * SUPER, SUPER)
    cols = pl.ds(lo, LANES)
    xa = x_ref[pl.ds(base, SUPER), cols].astype(F32)
    pbase = pl.multiple_of(jnp.maximum(base - 8, 0), 8)
    xp = x_ref[pl.ds(pbase, 8), cols].astype(F32)
    xp = jnp.where(first, 0.0, xp)
    w = w_ref[:, cols]
    row8 = lax.broadcasted_iota(jnp.int32, (8, LANES), 0)
    acc = xa * w[DN_CONV - 1:DN_CONV, :]
    for j in range(1, DN_CONV):
        rolled = pltpu.roll(xa, j, 0)
        head = jnp.where(row8 < j, pltpu.roll(xp, j, 0), rolled[:8, :])
        shifted = jnp.concatenate([head, rolled[8:, :]], axis=0)
        acc = acc + shifted * w[DN_CONV - 1 - j:DN_CONV - j, :]
    return _silu(acc)


def _l2n(t):
    return t * lax.rsqrt(jnp.sum(t * t, axis=-1, keepdims=True) + EPS)


def _lane_pick(x, idx):
    lane = lax.broadcasted_iota(jnp.int32, x.shape, 1)
    return jnp.sum(jnp.where(lane == idx, x, 0.0), axis=-1, keepdims=True)


def _dn_body(hb, qscale, q_ref, k_ref, v_ref, cq_ref, ck_ref, cv_ref, g_ref, gt_ref, nrm_ref,
             o_ref, u_s, w_s, qd_s, kd_s, at_s, fl_s, st_s):
    s = q_ref.shape[0]
    n_super = s // SUPER
    per = SUPER // CHUNK
    head0 = pl.program_id(1) * hb

    ri = lax.broadcasted_iota(jnp.int32, (SUPER, SUPER), 0)
    ci = lax.broadcasted_iota(jnp.int32, (SUPER, SUPER), 1)
    same = (ri // CHUNK) == (ci // CHUNK)
    incl = same & (ci <= ri)
    strict = same & (ci < ri)

    for i in range(hb):
        lo = i * LANES
        head = head0 + i

        def prep(sidx, carry, lo=lo, head=head, i=i):
            base = pl.multiple_of(sidx * SUPER, SUPER)
            rows = pl.ds(base, SUPER)
            first = sidx == 0
            q = _l2n(_conv_silu(q_ref, cq_ref, lo, sidx, first)) * qscale
            k = _l2n(_conv_silu(k_ref, ck_ref, lo, sidx, first))
            v = _conv_silu(v_ref, cv_ref, lo, sidx, first)
            gs = g_ref[rows, :]
            gc = _lane_pick(gs, SM_A + head)
            bt = _lane_pick(gs, SM_B + head)
            gl = _lane_pick(gs, SM_GL + head)
            grow = gt_ref[pl.ds(SM_A + head, 1), rows]
            dec = jnp.exp(jnp.where(incl, gc - grow, -jnp.inf))
            kb = k * bt
            kbf = k.astype(BF16)
            a = jnp.where(strict, _dot_nt(kb.astype(BF16), kbf) * dec, 0.0)
            attn = _dot_nt(q.astype(BF16), kbf) * dec
            p = -a
            r = p
            n_sq = CHUNK.bit_length() - 2
            for _ in range(n_sq):
                pb = p.astype(BF16)
                p = _dot(pb, pb)
                r = r + p + _dot(r.astype(BF16), p.astype(BF16))
            eg = jnp.exp(gc)
            vb = v * bt
            kbe = kb * eg
            rhs = jnp.concatenate([vb, kbe], axis=1)
            uw = rhs + _dot(r.astype(BF16), rhs.astype(BF16))
            cols = pl.ds(lo, LANES)
            u_s[rows, cols] = uw[:, :LANES].astype(BF16)
            w_s[rows, cols] = uw[:, LANES:].astype(BF16)
            qd_s[rows, cols] = (q * eg).astype(BF16)
            kd_s[rows, cols] = (k * jnp.exp(gl - gc)).astype(BF16)
            fl_s[rows, cols] = jnp.broadcast_to(jnp.exp(gl), (SUPER, LANES))
            for rr in range(per):
                blk = attn[rr * CHUNK:(rr + 1) * CHUNK, rr * CHUNK:(rr + 1) * CHUNK]
                at_s[i, pl.ds(base + rr * CHUNK, CHUNK), :] = blk.astype(BF16)
            return carry

        lax.fori_loop(0, n_super, prep, 0)

    st_s[...] = jnp.zeros_like(st_s)

    def scan(c, carry):
        r = pl.ds(pl.multiple_of(c * CHUNK, CHUNK), CHUNK)
        for i in range(hb):
            cols = pl.ds(i * LANES, LANES)
            st = st_s[i]
            stb = st.astype(BF16)
            vnew = u_s[r, cols].astype(F32) - _dot(w_s[r, cols], stb)
            vnb = vnew.astype(BF16)
            o = _dot(qd_s[r, cols], stb) + _dot(at_s[i, r, :], vnb)
            st_s[i] = st * fl_s[pl.ds(c * CHUNK, 1), cols] + _dot_tn(kd_s[r, cols], vnb)
            o_ref[r, cols] = _rms(o, nrm_ref[...]).astype(o_ref.dtype)
        return carry

    lax.fori_loop(0, s // CHUNK, scan, 0)


def _deltanet(z3, dn_conv, g3, gt3, dn_norm, d_model, qkv_off, hb=2):
    b, s, _ = z3.shape
    dh = d_model // DN_HEADS
    assert dh == LANES
    bw = hb * dh
    nblk = d_model // bw
    body = functools.partial(_dn_body, hb, dh ** -0.5)
    z_spec = lambda part: pl.BlockSpec(
        (None, s, bw), lambda i, h: (i, 0, (qkv_off + part * d_model) // bw + h))
    c_spec = lambda part: pl.BlockSpec((DN_CONV, bw), lambda i, h: (0, part * nblk + h))
    return pl.pallas_call(
        body,
        grid=(b, nblk),
        in_specs=[
            z_spec(0), z_spec(1), z_spec(2),
            c_spec(0), c_spec(1), c_spec(2),
            pl.BlockSpec((None, s, LANES), lambda i, h: (i, 0, 0)),
            pl.BlockSpec((None, LANES, s), lambda i, h: (i, 0, 0)),
            pl.BlockSpec((1, dh), lambda i, h: (0, 0)),
        ],
        out_specs=pl.BlockSpec((None, s, bw), lambda i, h: (i, 0, h)),
        out_shape=jax.ShapeDtypeStruct((b, s, d_model), BF16),
        scratch_shapes=[
            pltpu.VMEM((s, bw), BF16),
            pltpu.VMEM((s, bw), BF16),
            pltpu.VMEM((s, bw), BF16),
            pltpu.VMEM((s, bw), BF16),
            pltpu.VMEM((hb, s, CHUNK), BF16),
            pltpu.VMEM((s, bw), F32),
            pltpu.VMEM((hb, dh, dh), F32),
        ],
        compiler_params=_cparams(("parallel", "parallel")),
        name="deltanet_mixer",
    )(z3, z3, z3, dn_conv, dn_conv, dn_conv, g3, gt3, dn_norm)


def _merge_body(og_ref, od_ref, sg_ref, sz_ref, ga_ref, gb_ref, x_ref, w_ref, g_ref, x1_ref, h2_ref):
    f = lambda r: r[...].astype(F32)
    mixed = f(ga_ref) * (f(og_ref) * f(sg_ref)) + f(gb_ref) * (f(od_ref) * f(sz_ref))
    x1 = x_ref[...] + _dot(mixed.astype(BF16), w_ref[...])
    x1_ref[...] = x1
    h2_ref[...] = _rms(x1, g_ref[...]).astype(BF16)


def _merge(o_gla, o_dn, z2d, act_off, x2d, w_out, g_mlp, tm=256):
    t, d = x2d.shape
    row = lambda c: pl.BlockSpec((tm, d), lambda i, c=c: (i, c))
    a0 = act_off // d
    return pl.pallas_call(
        _merge_body,
        grid=(t // tm,),
        in_specs=[
            row(0), row(0), row(a0), row(a0 + 1), row(a0 + 2), row(a0 + 3), row(0),
            pl.BlockSpec((d, d), lambda i: (0, 0)),
            pl.BlockSpec((1, d), lambda i: (0, 0)),
        ],
        out_specs=[row(0), row(0)],
        out_shape=[jax.ShapeDtypeStruct((t, d), F32), jax.ShapeDtypeStruct((t, d), BF16)],
        compiler_params=_cparams(("parallel",)),
        name="merge_out_proj",
    )(o_gla, o_dn, z2d, z2d, z2d, z2d, x2d, w_out, g_mlp)


def _mlp_body(h_ref, wu_ref, wd_ref, x_ref, o_ref):
    j = pl.program_id(1)
    mid = jnp.square(jnp.maximum(_dot(h_ref[...], wu_ref[...]), 0.0)).astype(BF16)
    part = _dot(mid, wd_ref[...])

    @pl.when(j == 0)
    def _():
        o_ref[...] = x_ref[...] + part

    @pl.when(j > 0)
    def _():
        o_ref[...] += part


def _mlp(h2, w_up, w_down, x1, tm=512, tf=1024):
    t, d = x1.shape
    ff = w_up.shape[1]
    return pl.pallas_call(
        _mlp_body,
        grid=(t // tm, ff // tf),
        in_specs=[
            pl.BlockSpec((tm, d), lambda i, j: (i, 0)),
            pl.BlockSpec((d, tf), lambda i, j: (0, j)),
            pl.BlockSpec((tf, d), lambda i, j: (j, 0)),
            pl.BlockSpec((tm, d), lambda i, j: (i, 0)),
        ],
        out_specs=pl.BlockSpec((tm, d), lambda i, j: (i, 0)),
        out_shape=jax.ShapeDtypeStruct((t, d), F32),
        compiler_params=_cparams(("parallel", "arbitrary")),
        name="relu2_mlp",
    )(h2, w_up, w_down, x1)


def _ple_body(x_ref, p_ref, wg_ref, wp_ref, gp_ref, gf_ref, o_ref):
    x2 = x_ref[...]
    h3 = _rms(x2, gp_ref[...]).astype(BF16)
    gate = _sigmoid(_dot(h3, wg_ref[...]))
    proj = _dot(p_ref[...].astype(BF16), wp_ref[...])
    o_ref[...] = _rms(x2 + gate * proj, gf_ref[...])


def _ple(x2, p2d, w_gate, w_proj, g_ple, g_final, tm=256):
    t, d = x2.shape
    pd = p2d.shape[1]
    return pl.pallas_call(
        _ple_body,
        grid=(t // tm,),
        in_specs=[
            pl.BlockSpec((tm, d), lambda i: (i, 0)),
            pl.BlockSpec((tm, pd), lambda i: (i, 0)),
            pl.BlockSpec((d, d), lambda i: (0, 0)),
            pl.BlockSpec((pd, d), lambda i: (0, 0)),
            pl.BlockSpec((1, d), lambda i: (0, 0)),
            pl.BlockSpec((1, d), lambda i: (0, 0)),
        ],
        out_specs=pl.BlockSpec((tm, d), lambda i: (i, 0)),
        out_shape=jax.ShapeDtypeStruct((t, d), F32),
        compiler_params=_cparams(("parallel",)),
        name="ple_final_norm",
    )(x2, p2d, w_gate, w_proj, g_ple, g_final)


def _layer(x, p_i, g_mix, w_in, gla_w2, gla_b, gla_norm, dn_conv, dn_a_log, dn_dt_bias, dn_norm,
           w_out, g_mlp, w_up, w_down, g_ple, w_ple_gate, w_ple_proj, g_out):
    b, s, d = x.shape
    t = b * s
    gla_qk = d // 2
    dn_qkv = 3 * d
    o_q, o_k = 0, gla_qk
    o_v = 2 * gla_qk
    o_g = o_v + d
    o_lr = o_g + d
    o_dn = o_lr + GLA_LOWRANK
    o_z = o_dn + dn_qkv
    o_a = o_z + d
    o_b = o_a + DN_HEADS
    o_ga = o_b + DN_HEADS
    o_gb = o_ga + d
    cs = lambda lo, n: w_in[:, lo:lo + n]
    w_big = jnp.concatenate(
        [cs(o_q, gla_qk), cs(o_k, gla_qk), cs(o_v, d), cs(o_dn, dn_qkv),
         cs(o_g, d), cs(o_z, d), cs(o_ga, d), cs(o_gb, d)], axis=1).astype(BF16)
    w_small = jnp.concatenate(
        [cs(o_lr, GLA_LOWRANK), cs(o_a, DN_HEADS), cs(o_b, DN_HEADS),
         jnp.zeros((d, LANES - SM_GL), w_in.dtype)], axis=1).astype(BF16)
    n_plain = 2 * gla_qk + d + dn_qkv
    qkv_off = 2 * gla_qk + d

    x2d = x.reshape(t, d)
    z2d, zs2d = _in_proj(x2d, g_mix.reshape(1, d), w_big, w_small, n_plain, 2 * d)
    z3 = z2d.reshape(b, s, -1)
    zs3 = zs2d.reshape(b, s, LANES)

    pad = lambda v: jnp.zeros((1, LANES), F32).at[0, SM_A:SM_B].set(v.astype(F32))
    g3, gt3 = _gates(zs3, pad(-jnp.exp(dn_a_log.astype(F32))), pad(dn_dt_bias))

    w2p = jnp.zeros((LANES, gla_qk), F32).at[:GLA_LOWRANK].set(gla_w2).astype(BF16)
    o_gla = _gla(z3, zs3, w2p, gla_b.reshape(1, -1).astype(F32), gla_norm.reshape(1, -1).astype(F32), d)
    o_dn = _deltanet(z3, dn_conv.astype(F32), g3, gt3, dn_norm.reshape(1, -1).astype(F32), d, qkv_off)

    x1, h2 = _merge(o_gla.reshape(t, d), o_dn.reshape(t, d), z2d, n_plain, x2d,
                    w_out.astype(BF16), g_mlp.reshape(1, d))
    x2 = _mlp(h2, w_up.astype(BF16), w_down.astype(BF16), x1)
    out = _ple(x2, p_i.reshape(t, -1), w_ple_gate.astype(BF16), w_ple_proj.astype(BF16),
               g_ple.reshape(1, d), g_out.reshape(1, d))
    return out.reshape(b, s, d)


def kernel(x, p, g_mix, w_in, gla_w2, gla_b, gla_norm, dn_conv, dn_a_log, dn_dt_bias, dn_norm,
           w_out, g_mlp, w_up, w_down, g_ple, w_ple_gate, w_ple_proj, g_final):
    depth = w_in.shape[0]
    assert depth == 1, "the final rms_norm is fused into the last layer's kernel"
    return _layer(x, p[0], g_mix[0], w_in[0], gla_w2[0], gla_b[0], gla_norm[0], dn_conv[0],
                  dn_a_log[0], dn_dt_bias[0], dn_norm[0], w_out[0], g_mlp[0], w_up[0], w_down[0],
                  g_ple[0], w_ple_gate[0], w_ple_proj[0], g_final)
```

```python
import functools

import jax
import jax.numpy as jnp
from jax import lax
from jax.experimental import pallas as pl
from jax.experimental.pallas import tpu as pltpu

F32 = jnp.float32
BF16 = jnp.bfloat16

EPS = 1e-6
CHUNK = 64
GLA_HEADS = 4
GLA_LOWRANK = 16
GLA_TAU = 16.0
DN_HEADS = 16
DN_CONV = 4
LANES = 128
TILE = 256
HALF = 128
DN_TILE = 1024
VMEM_LIMIT = 56 * 1024 * 1024

SM_LR = 0
SM_A = GLA_LOWRANK
SM_B = SM_A + DN_HEADS
SM_GL = SM_B + DN_HEADS


def _cparams(sem):
    return pltpu.CompilerParams(dimension_semantics=sem, vmem_limit_bytes=VMEM_LIMIT)


def _dot(a, b):
    return jnp.dot(a, b, preferred_element_type=F32)


def _dot_nt(a, b):
    return lax.dot_general(a, b, (((1,), (1,)), ((), ())), preferred_element_type=F32)


def _dot_tn(a, b):
    return lax.dot_general(a, b, (((0,), (0,)), ((), ())), preferred_element_type=F32)


def _sigmoid(x):
    return 1.0 / (1.0 + jnp.exp(-x))


def _silu(x):
    return x * _sigmoid(x)


def _log_sigmoid(x):
    return jnp.minimum(x, 0.0) - jnp.log(1.0 + jnp.exp(-jnp.abs(x)))


def _softplus(x):
    return jnp.maximum(x, 0.0) + jnp.log(1.0 + jnp.exp(-jnp.abs(x)))


def _rms(x, g):
    return x * lax.rsqrt(jnp.mean(x * x, axis=-1, keepdims=True) + EPS) * g


def _chunk_cumsum(x, row):
    n = x.shape[0]
    shift = 1
    while shift < CHUNK:
        x = x + jnp.where(row >= shift, pltpu.roll(x, shift, 0), 0.0)
        shift *= 2
    del n
    return x


def _chunk_rev_cumsum(x, row):
    n = x.shape[0]
    shift = 1
    while shift < CHUNK:
        x = x + jnp.where(row < CHUNK - shift, pltpu.roll(x, n - shift, 0), 0.0)
        shift *= 2
    return x


def _in_proj_body(n_plain, n_silu, x_ref, g_ref, wb_ref, ws_ref, z_ref, zs_ref, h_ref):
    j = pl.program_id(1)

    @pl.when(j == 0)
    def _():
        hb = _rms(x_ref[...], g_ref[...]).astype(BF16)
        h_ref[...] = hb
        zs_ref[...] = _dot(hb, ws_ref[...])

    @pl.when(j < n_plain)
    def _():
        z_ref[...] = _dot(h_ref[...], wb_ref[...]).astype(BF16)

    @pl.when((j >= n_plain) & (j < n_plain + n_silu))
    def _():
        z_ref[...] = _silu(_dot(h_ref[...], wb_ref[...])).astype(BF16)

    @pl.when(j >= n_plain + n_silu)
    def _():
        z_ref[...] = _sigmoid(_dot(h_ref[...], wb_ref[...])).astype(BF16)


def _in_proj(x2d, g_mix, w_big, w_small, n_plain_cols, n_silu_cols, tm=1024, tn=1024):
    t, d = x2d.shape
    n = w_big.shape[1]
    body = functools.partial(_in_proj_body, n_plain_cols // tn, n_silu_cols // tn)
    return pl.pallas_call(
        body,
        grid=(t // tm, n // tn),
        in_specs=[
            pl.BlockSpec((tm, d), lambda i, j: (i, 0)),
            pl.BlockSpec((1, d), lambda i, j: (0, 0)),
            pl.BlockSpec((d, tn), lambda i, j: (0, j)),
            pl.BlockSpec((d, LANES), lambda i, j: (0, 0)),
        ],
        out_specs=[
            pl.BlockSpec((tm, tn), lambda i, j: (i, j)),
            pl.BlockSpec((tm, LANES), lambda i, j: (i, 0)),
        ],
        out_shape=[
            jax.ShapeDtypeStruct((t, n), BF16),
            jax.ShapeDtypeStruct((t, LANES), F32),
        ],
        scratch_shapes=[pltpu.VMEM((tm, d), BF16)],
        compiler_params=_cparams(("parallel", "arbitrary")),
        name="in_proj",
    )(x2d, g_mix, w_big, w_small)


def _gates_body(zs_ref, aneg_ref, dtb_ref, g_ref, gt_ref):
    zs = zs_ref[...]
    s = zs.shape[0]
    lane = lax.broadcasted_iota(jnp.int32, zs.shape, 1)
    row = lax.broadcasted_iota(jnp.int32, zs.shape, 0) % CHUNK
    is_a = (lane >= SM_A) & (lane < SM_B)
    is_b = (lane >= SM_B) & (lane < SM_GL)
    g = jnp.where(is_a, aneg_ref[...] * _softplus(zs + dtb_ref[...]), 0.0)
    gcum = _chunk_cumsum(g, row)
    gtot = gcum + _chunk_rev_cumsum(g, row) - g
    beta = _sigmoid(zs)
    out = jnp.where(is_a, gcum, jnp.where(is_b, beta, 0.0))
    out = out + pltpu.roll(jnp.where(is_a, gtot, 0.0), SM_GL - SM_A, 1)
    del s
    g_ref[...] = out
    gt_ref[...] = out.T


def _gates(zs3, a_neg, dt_bias):
    b, s, _ = zs3.shape
    return pl.pallas_call(
        _gates_body,
        grid=(b,),
        in_specs=[
            pl.BlockSpec((None, s, LANES), lambda i: (i, 0, 0)),
            pl.BlockSpec((1, LANES), lambda i: (0, 0)),
            pl.BlockSpec((1, LANES), lambda i: (0, 0)),
        ],
        out_specs=[
            pl.BlockSpec((None, s, LANES), lambda i: (i, 0, 0)),
            pl.BlockSpec((None, LANES, s), lambda i: (i, 0, 0)),
        ],
        out_shape=[
            jax.ShapeDtypeStruct((b, s, LANES), F32),
            jax.ShapeDtypeStruct((b, LANES, s), F32),
        ],
        compiler_params=_cparams(("parallel",)),
        name="dn_gates",
    )(zs3, a_neg, dt_bias)


def _gla_body(scale, q_ref, k_ref, v_ref, lr_ref, w2_ref, b_ref, nrm_ref, o_ref, st_ref):
    s, dk = q_ref.shape
    per = TILE // CHUNK
    st_ref[...] = jnp.zeros_like(st_ref)
    row = lax.broadcasted_iota(jnp.int32, (TILE, dk), 0) % CHUNK
    ri = lax.broadcasted_iota(jnp.int32, (TILE, TILE), 0)
    ci = lax.broadcasted_iota(jnp.int32, (TILE, TILE), 1)
    causal = ((ri // CHUNK) == (ci // CHUNK)) & (ci <= ri)

    def tile(t, carry):
        base = pl.multiple_of(t * TILE, TILE)
        r = pl.ds(base, TILE)
        q = q_ref[r, :].astype(F32) * scale
        k = k_ref[r, :].astype(F32)
        v = v_ref[r, :]
        pre = _dot(lr_ref[r, :].astype(BF16), w2_ref[...]) + b_ref[...]
        bc = _chunk_cumsum(_log_sigmoid(pre) * (1.0 / GLA_TAU), row)
        q_in = (q * jnp.exp(bc)).astype(BF16)
        k_in = (k * jnp.exp(-bc)).astype(BF16)
        a = jnp.where(causal, _dot_nt(q_in, k_in), 0.0).astype(BF16)
        o_intra = _dot(a, v)
        for c in range(per):
            lo, hi = c * CHUNK, (c + 1) * CHUNK
            b_last = bc[hi - 1:hi, :]
            k_dec = (k[lo:hi, :] * jnp.exp(b_last - bc[lo:hi, :])).astype(BF16)
            st = st_ref[...]
            o = o_intra[lo:hi, :] + _dot_nt(q_in[lo:hi, :], st.astype(BF16))
            st_ref[...] = st * jnp.exp(b_last) + _dot_tn(v[lo:hi, :], k_dec)
            o_ref[pl.ds(base + lo, CHUNK), :] = _rms(o, nrm_ref[...]).astype(o_ref.dtype)
        return carry

    lax.fori_loop(0, s // TILE, tile, 0)


def _gla(z3, zs3, w2p, gla_b, gla_norm, d_model):
    b, s, _ = z3.shape
    dk = d_model // (2 * GLA_HEADS)
    dv = d_model // GLA_HEADS
    assert s % TILE == 0
    qk_cols = GLA_HEADS * dk
    body = functools.partial(_gla_body, dk ** -0.5)
    return pl.pallas_call(
        body,
        grid=(b, GLA_HEADS),
        in_specs=[
            pl.BlockSpec((None, s, dk), lambda i, h: (i, 0, h)),
            pl.BlockSpec((None, s, dk), lambda i, h: (i, 0, qk_cols // dk + h)),
            pl.BlockSpec((None, s, dv), lambda i, h: (i, 0, 2 * qk_cols // dv + h)),
            pl.BlockSpec((None, s, LANES), lambda i, h: (i, 0, 0)),
            pl.BlockSpec((LANES, dk), lambda i, h: (0, h)),
            pl.BlockSpec((1, dk), lambda i, h: (0, h)),
            pl.BlockSpec((1, dv), lambda i, h: (0, 0)),
        ],
        out_specs=pl.BlockSpec((None, s, dv), lambda i, h: (i, 0, h)),
        out_shape=jax.ShapeDtypeStruct((b, s, GLA_HEADS * dv), BF16),
        scratch_shapes=[pltpu.VMEM((dv, dk), F32)],
        compiler_params=_cparams(("parallel", "parallel")),
        name="gla_mixer",
    )(z3, z3, z3, zs3, w2p, gla_b, gla_norm)


def _conv_silu(x_ref, w_ref, lo, base, n_rows, first):
    cols = pl.ds(lo, LANES)
    xa = x_ref[pl.ds(base, n_rows), cols].astype(F32)
    pbase = pl.multiple_of(jnp.maximum(base - 8, 0), 8)
    xp = x_ref[pl.ds(pbase, 8), cols].astype(F32)
    xp = jnp.where(first, 0.0, xp)
    w = w_ref[:, cols]
    row8 = lax.broadcasted_iota(jnp.int32, (8, LANES), 0)
    acc = xa * w[DN_CONV - 1:DN_CONV, :]
    for j in range(1, DN_CONV):
        rolled = pltpu.roll(xa, j, 0)
        head = jnp.where(row8 < j, pltpu.roll(xp, j, 0), rolled[:8, :])
        shifted = jnp.concatenate([head, rolled[8:, :]], axis=0)
        acc = acc + shifted * w[DN_CONV - 1 - j:DN_CONV - j, :]
    return _silu(acc)


def _l2n(t):
    return t * lax.rsqrt(jnp.sum(t * t, axis=-1, keepdims=True) + EPS)


def _lane_pick(x, idx):
    lane = lax.broadcasted_iota(jnp.int32, x.shape, 1)
    return jnp.sum(jnp.where(lane == idx, x, 0.0), axis=-1, keepdims=True)


def _bdot(a, b):
    return lax.dot_general(a, b, (((2,), (1,)), ((0,), (0,))), preferred_element_type=F32)


def _bdot_nt(a, b):
    return lax.dot_general(a, b, (((2,), (2,)), ((0,), (0,))), preferred_element_type=F32)


def _dn_body(hb, qscale, q_ref, k_ref, v_ref, cq_ref, ck_ref, cv_ref, g_ref, gt_ref, nrm_ref,
             o_ref, kp_s, np_s, qp_s, op_s, fl_s, st_s):
    s = q_ref.shape[0]
    n_half = DN_TILE // HALF
    per_tile = DN_TILE // CHUNK
    head0 = pl.program_id(1) * hb

    ri = lax.broadcasted_iota(jnp.int32, (HALF, HALF), 0)
    ci = lax.broadcasted_iota(jnp.int32, (HALF, HALF), 1)
    same = (ri // CHUNK) == (ci // CHUNK)
    incl = same & (ci <= ri)
    strict = same & (ci < ri)
    n_sq = CHUNK.bit_length() - 2
    b3 = lambda x: x.reshape(n_half, HALF, x.shape[-1])

    for i in range(hb):
        def prep(t, carry, i=i):
            lo = i * LANES
            head = head0 + i
            cols = pl.ds(lo, LANES)
            base = pl.multiple_of(t * DN_TILE, DN_TILE)
            rows = pl.ds(base, DN_TILE)
            first = t == 0
            q = _l2n(_conv_silu(q_ref, cq_ref, lo, base, DN_TILE, first)) * qscale
            k = _l2n(_conv_silu(k_ref, ck_ref, lo, base, DN_TILE, first))
            v = _conv_silu(v_ref, cv_ref, lo, base, DN_TILE, first)
            gs = g_ref[rows, :]
            gc = _lane_pick(gs, SM_A + head)
            bt = _lane_pick(gs, SM_B + head)
            gl = _lane_pick(gs, SM_GL + head)
            eg = jnp.exp(gc)
            kb = k * bt
            vb = v * bt
            kbe = kb * eg
            qd = q * eg
            kd = (k * jnp.exp(gl - gc)).astype(BF16)
            flb = jnp.broadcast_to(jnp.exp(gl), (DN_TILE, LANES))
            grow_t = gt_ref[pl.ds(SM_A + head, 1), rows]
            grow = jnp.stack([grow_t[:, j * HALF:(j + 1) * HALF] for j in range(n_half)], axis=0)
            dec = jnp.exp(jnp.where(incl, b3(gc) - grow, -jnp.inf))
            kbf = b3(k).astype(BF16)
            a = jnp.where(strict, _bdot_nt(b3(kb).astype(BF16), kbf) * dec, 0.0)
            attn = (_bdot_nt(b3(q).astype(BF16), kbf) * dec).astype(BF16)
            p = -a
            r = p
            for _ in range(n_sq):
                pb = p.astype(BF16)
                p = _bdot(pb, pb)
                r = r + p + _bdot(r.astype(BF16), p.astype(BF16))
            rhs = jnp.concatenate([b3(vb), b3(kbe)], axis=2)
            uw = (rhs + _bdot(r.astype(BF16), rhs.astype(BF16))).astype(BF16)
            aw = _bdot(attn, uw)
            op_s[rows, cols] = aw[:, :, :LANES].reshape(DN_TILE, LANES).astype(op_s.dtype)
            qp_s[rows, cols] = (qd - aw[:, :, LANES:].reshape(DN_TILE, LANES)).astype(qp_s.dtype)
            uw2 = uw.reshape(DN_TILE, 2 * LANES)
            for c in range(per_tile):
                c0 = c * CHUNK
                kn = _dot_tn(kd[c0:c0 + CHUNK, :], uw2[c0:c0 + CHUNK, :])
                cidx = t * per_tile + c
                np_s[i, cidx] = kn[:, :LANES].astype(np_s.dtype)
                kp_s[i, cidx] = kn[:, LANES:].astype(kp_s.dtype)
                fl_s[i, pl.ds(cidx, 1), :] = flb[c0:c0 + 1, :]
            return carry

        lax.fori_loop(0, s // DN_TILE, prep, 0)

    st_s[...] = jnp.zeros_like(st_s)

    def scan(c, carry):
        r = pl.ds(pl.multiple_of(c * CHUNK, CHUNK), CHUNK)
        for i in range(hb):
            cols = pl.ds(i * LANES, LANES)
            st = st_s[i]
            stb = st.astype(BF16)
            o = _dot(qp_s[r, cols], stb) + op_s[r, cols].astype(F32)
            st_s[i] = st * fl_s[i, pl.ds(c, 1), :] - _dot(kp_s[i, c], stb) + np_s[i, c]
            o_ref[r, cols] = _rms(o, nrm_ref[...]).astype(o_ref.dtype)
        return carry

    lax.fori_loop(0, s // CHUNK, scan, 0)


def _deltanet(z3, dn_conv, g3, gt3, dn_norm, d_model, qkv_off, hb=4):
    b, s, _ = z3.shape
    dh = d_model // DN_HEADS
    assert dh == LANES and s % DN_TILE == 0
    bw = hb * dh
    nblk = d_model // bw
    n_chunks = s // CHUNK
    body = functools.partial(_dn_body, hb, dh ** -0.5)
    z_spec = lambda part: pl.BlockSpec(
        (None, s, bw), lambda i, h: (i, 0, (qkv_off + part * d_model) // bw + h))
    c_spec = lambda part: pl.BlockSpec((DN_CONV, bw), lambda i, h: (0, part * nblk + h))
    return pl.pallas_call(
        body,
        grid=(b, nblk),
        in_specs=[
            z_spec(0), z_spec(1), z_spec(2),
            c_spec(0), c_spec(1), c_spec(2),
            pl.BlockSpec((None, s, LANES), lambda i, h: (i, 0, 0)),
            pl.BlockSpec((None, LANES, s), lambda i, h: (i, 0, 0)),
            pl.BlockSpec((1, dh), lambda i, h: (0, 0)),
        ],
        out_specs=pl.BlockSpec((None, s, bw), lambda i, h: (i, 0, h)),
        out_shape=jax.ShapeDtypeStruct((b, s, d_model), BF16),
        scratch_shapes=[
            pltpu.VMEM((hb, n_chunks, dh, dh), BF16),
            pltpu.VMEM((hb, n_chunks, dh, dh), F32),
            pltpu.VMEM((s, bw), BF16),
            pltpu.VMEM((s, bw), BF16),
            pltpu.VMEM((hb, n_chunks, LANES), F32),
            pltpu.VMEM((hb, dh, dh), F32),
        ],
        compiler_params=_cparams(("parallel", "parallel")),
        name="deltanet_mixer",
    )(z3, z3, z3, dn_conv, dn_conv, dn_conv, g3, gt3, dn_norm)


def _merge_body(og_ref, od_ref, sg_ref, sz_ref, ga_ref, gb_ref, x_ref, w_ref, g_ref, x1_ref, h2_ref):
    f = lambda r: r[...].astype(F32)
    mixed = f(ga_ref) * (f(og_ref) * f(sg_ref)) + f(gb_ref) * (f(od_ref) * f(sz_ref))
    x1 = x_ref[...] + _dot(mixed.astype(BF16), w_ref[...])
    x1_ref[...] = x1
    h2_ref[...] = _rms(x1, g_ref[...]).astype(BF16)


def _merge(o_gla, o_dn, z2d, act_off, x2d, w_out, g_mlp, tm=256):
    t, d = x2d.shape
    row = lambda c: pl.BlockSpec((tm, d), lambda i, c=c: (i, c))
    a0 = act_off // d
    return pl.pallas_call(
        _merge_body,
        grid=(t // tm,),
        in_specs=[
            row(0), row(0), row(a0), row(a0 + 1), row(a0 + 2), row(a0 + 3), row(0),
            pl.BlockSpec((d, d), lambda i: (0, 0)),
            pl.BlockSpec((1, d), lambda i: (0, 0)),
        ],
        out_specs=[row(0), row(0)],
        out_shape=[jax.ShapeDtypeStruct((t, d), F32), jax.ShapeDtypeStruct((t, d), BF16)],
        compiler_params=_cparams(("parallel",)),
        name="merge_out_proj",
    )(o_gla, o_dn, z2d, z2d, z2d, z2d, x2d, w_out, g_mlp)


def _mlp_body(h_ref, wu_ref, wd_ref, x_ref, o_ref):
    j = pl.program_id(1)
    mid = jnp.square(jnp.maximum(_dot(h_ref[...], wu_ref[...]), 0.0)).astype(BF16)
    part = _dot(mid, wd_ref[...])

    @pl.when(j == 0)
    def _():
        o_ref[...] = x_ref[...] + part

    @pl.when(j > 0)
    def _():
        o_ref[...] += part


def _mlp(h2, w_up, w_down, x1, tm=512, tf=1024):
    t, d = x1.shape
    ff = w_up.shape[1]
    return pl.pallas_call(
        _mlp_body,
        grid=(t // tm, ff // tf),
        in_specs=[
            pl.BlockSpec((tm, d), lambda i, j: (i, 0)),
            pl.BlockSpec((d, tf), lambda i, j: (0, j)),
            pl.BlockSpec((tf, d), lambda i, j: (j, 0)),
            pl.BlockSpec((tm, d), lambda i, j: (i, 0)),
        ],
        out_specs=pl.BlockSpec((tm, d), lambda i, j: (i, 0)),
        out_shape=jax.ShapeDtypeStruct((t, d), F32),
        compiler_params=_cparams(("parallel", "arbitrary")),
        name="relu2_mlp",
    )(h2, w_up, w_down, x1)


def _ple_body(x_ref, p_ref, wg_ref, wp_ref, gp_ref, gf_ref, o_ref):
    x2 = x_ref[...]
    h3 = _rms(x2, gp_ref[...]).astype(BF16)
    gate = _sigmoid(_dot(h3, wg_ref[...]))
    proj = _dot(p_ref[...].astype(BF16), wp_ref[...])
    o_ref[...] = _rms(x2 + gate * proj, gf_ref[...])


def _ple(x2, p2d, w_gate, w_proj, g_ple, g_final, tm=256):
    t, d = x2.shape
    pd = p2d.shape[1]
    return pl.pallas_call(
        _ple_body,
        grid=(t // tm,),
        in_specs=[
            pl.BlockSpec((tm, d), lambda i: (i, 0)),
            pl.BlockSpec((tm, pd), lambda i: (i, 0)),
            pl.BlockSpec((d, d), lambda i: (0, 0)),
            pl.BlockSpec((pd, d), lambda i: (0, 0)),
            pl.BlockSpec((1, d), lambda i: (0, 0)),
            pl.BlockSpec((1, d), lambda i: (0, 0)),
        ],
        out_specs=pl.BlockSpec((tm, d), lambda i: (i, 0)),
        out_shape=jax.ShapeDtypeStruct((t, d), F32),
        compiler_params=_cparams(("parallel",)),
        name="ple_final_norm",
    )(x2, p2d, w_gate, w_proj, g_ple, g_final)


def _layer(x, p_i, g_mix, w_in, gla_w2, gla_b, gla_norm, dn_conv, dn_a_log, dn_dt_bias, dn_norm,
           w_out, g_mlp, w_up, w_down, g_ple, w_ple_gate, w_ple_proj, g_out):
    b, s, d = x.shape
    t = b * s
    gla_qk = d // 2
    dn_qkv = 3 * d
    o_q, o_k = 0, gla_qk
    o_v = 2 * gla_qk
    o_g = o_v + d
    o_lr = o_g + d
    o_dn = o_lr + GLA_LOWRANK
    o_z = o_dn + dn_qkv
    o_a = o_z + d
    o_b = o_a + DN_HEADS
    o_ga = o_b + DN_HEADS
    o_gb = o_ga + d
    cs = lambda lo, n: w_in[:, lo:lo + n]
    w_big = jnp.concatenate(
        [cs(o_q, gla_qk), cs(o_k, gla_qk), cs(o_v, d), cs(o_dn, dn_qkv),
         cs(o_g, d), cs(o_z, d), cs(o_ga, d), cs(o_gb, d)], axis=1).astype(BF16)
    w_small = jnp.concatenate(
        [cs(o_lr, GLA_LOWRANK), cs(o_a, DN_HEADS), cs(o_b, DN_HEADS),
         jnp.zeros((d, LANES - SM_GL), w_in.dtype)], axis=1).astype(BF16)
    n_plain = 2 * gla_qk + d + dn_qkv
    qkv_off = 2 * gla_qk + d

    x2d = x.reshape(t, d)
    z2d, zs2d = _in_proj(x2d, g_mix.reshape(1, d), w_big, w_small, n_plain, 2 * d)
    z3 = z2d.reshape(b, s, -1)
    zs3 = zs2d.reshape(b, s, LANES)

    pad = lambda v: jnp.zeros((1, LANES), F32).at[0, SM_A:SM_B].set(v.astype(F32))
    g3, gt3 = _gates(zs3, pad(-jnp.exp(dn_a_log.astype(F32))), pad(dn_dt_bias))

    w2p = jnp.zeros((LANES, gla_qk), F32).at[:GLA_LOWRANK].set(gla_w2).astype(BF16)
    o_gla = _gla(z3, zs3, w2p, gla_b.reshape(1, -1).astype(F32), gla_norm.reshape(1, -1).astype(F32), d)
    o_dn = _deltanet(z3, dn_conv.astype(F32), g3, gt3, dn_norm.reshape(1, -1).astype(F32), d, qkv_off)

    x1, h2 = _merge(o_gla.reshape(t, d), o_dn.reshape(t, d), z2d, n_plain, x2d,
                    w_out.astype(BF16), g_mlp.reshape(1, d))
    x2 = _mlp(h2, w_up.astype(BF16), w_down.astype(BF16), x1)
    out = _ple(x2, p_i.reshape(t, -1), w_ple_gate.astype(BF16), w_ple_proj.astype(BF16),
               g_ple.reshape(1, d), g_out.reshape(1, d))
    return out.reshape(b, s, d)


def kernel(x, p, g_mix, w_in, gla_w2, gla_b, gla_norm, dn_conv, dn_a_log, dn_dt_bias, dn_norm,
           w_out, g_mlp, w_up, w_down, g_ple, w_ple_gate, w_ple_proj, g_final):
    depth = w_in.shape[0]
    assert depth == 1, "the final rms_norm is fused into the last layer's kernel"
    return _layer(x, p[0], g_mix[0], w_in[0], gla_w2[0], gla_b[0], gla_norm[0], dn_conv[0],
                  dn_a_log[0], dn_dt_bias[0], dn_norm[0], w_out[0], g_mlp[0], w_up[0], w_down[0],
                  g_ple[0], w_ple_gate[0], w_ple_proj[0], g_final)
```

```python
import functools

import jax
import jax.numpy as jnp
from jax import lax
from jax.experimental import pallas as pl
from jax.experimental.pallas import tpu as pltpu

F32 = jnp.float32
BF16 = jnp.bfloat16

EPS = 1e-6
CHUNK = 64
GLA_HEADS = 4
GLA_LOWRANK = 16
GLA_TAU = 16.0
DN_HEADS = 16
DN_CONV = 4
LANES = 128
TILE = 256
HALF = 128
DN_TILE = 1024
SCAN_UNROLL = 4
VMEM_LIMIT = 56 * 1024 * 1024

SM_LR = 0
SM_A = GLA_LOWRANK
SM_B = SM_A + DN_HEADS
SM_GL = SM_B + DN_HEADS


def _cparams(sem):
    return pltpu.CompilerParams(dimension_semantics=sem, vmem_limit_bytes=VMEM_LIMIT)


def _dot(a, b):
    return jnp.dot(a, b, preferred_element_type=F32)


def _dot_nt(a, b):
    return lax.dot_general(a, b, (((1,), (1,)), ((), ())), preferred_element_type=F32)


def _dot_tn(a, b):
    return lax.dot_general(a, b, (((0,), (0,)), ((), ())), preferred_element_type=F32)


def _sigmoid(x):
    return 1.0 / (1.0 + jnp.exp(-x))


def _silu(x):
    return x * _sigmoid(x)


def _log_sigmoid(x):
    return jnp.minimum(x, 0.0) - jnp.log(1.0 + jnp.exp(-jnp.abs(x)))


def _softplus(x):
    return jnp.maximum(x, 0.0) + jnp.log(1.0 + jnp.exp(-jnp.abs(x)))


def _rms(x, g):
    return x * lax.rsqrt(jnp.mean(x * x, axis=-1, keepdims=True) + EPS) * g


def _chunk_cumsum(x, row):
    n = x.shape[0]
    shift = 1
    while shift < CHUNK:
        x = x + jnp.where(row >= shift, pltpu.roll(x, shift, 0), 0.0)
        shift *= 2
    del n
    return x


def _chunk_cumsum_mxu(tri, x):
    hi = x.astype(BF16)
    rest = x - hi.astype(F32)
    mid = rest.astype(BF16)
    lo = (rest - mid.astype(F32)).astype(BF16)
    return _dot(tri, hi) + _dot(tri, mid) + _dot(tri, lo)


def _chunk_rev_cumsum(x, row):
    n = x.shape[0]
    shift = 1
    while shift < CHUNK:
        x = x + jnp.where(row < CHUNK - shift, pltpu.roll(x, n - shift, 0), 0.0)
        shift *= 2
    return x


def _in_proj_body(n_plain, n_silu, x_ref, g_ref, wb_ref, ws_ref, z_ref, zs_ref, h_ref):
    j = pl.program_id(1)

    @pl.when(j == 0)
    def _():
        hb = _rms(x_ref[...], g_ref[...]).astype(BF16)
        h_ref[...] = hb
        zs_ref[...] = _dot(hb, ws_ref[...])

    @pl.when(j < n_plain)
    def _():
        z_ref[...] = _dot(h_ref[...], wb_ref[...]).astype(BF16)

    @pl.when((j >= n_plain) & (j < n_plain + n_silu))
    def _():
        z_ref[...] = _silu(_dot(h_ref[...], wb_ref[...])).astype(BF16)

    @pl.when(j >= n_plain + n_silu)
    def _():
        z_ref[...] = _sigmoid(_dot(h_ref[...], wb_ref[...])).astype(BF16)


def _in_proj(x2d, g_mix, w_big, w_small, n_plain_cols, n_silu_cols, tm=1024, tn=1024):
    t, d = x2d.shape
    n = w_big.shape[1]
    body = functools.partial(_in_proj_body, n_plain_cols // tn, n_silu_cols // tn)
    return pl.pallas_call(
        body,
        grid=(t // tm, n // tn),
        in_specs=[
            pl.BlockSpec((tm, d), lambda i, j: (i, 0)),
            pl.BlockSpec((1, d), lambda i, j: (0, 0)),
            pl.BlockSpec((d, tn), lambda i, j: (0, j)),
            pl.BlockSpec((d, LANES), lambda i, j: (0, 0)),
        ],
        out_specs=[
            pl.BlockSpec((tm, tn), lambda i, j: (i, j)),
            pl.BlockSpec((tm, LANES), lambda i, j: (i, 0)),
        ],
        out_shape=[
            jax.ShapeDtypeStruct((t, n), BF16),
            jax.ShapeDtypeStruct((t, LANES), F32),
        ],
        scratch_shapes=[pltpu.VMEM((tm, d), BF16)],
        compiler_params=_cparams(("parallel", "arbitrary")),
        name="in_proj",
    )(x2d, g_mix, w_big, w_small)


def _gates_body(zs_ref, aneg_ref, dtb_ref, g_ref, gt_ref):
    zs = zs_ref[...]
    s = zs.shape[0]
    lane = lax.broadcasted_iota(jnp.int32, zs.shape, 1)
    row = lax.broadcasted_iota(jnp.int32, zs.shape, 0) % CHUNK
    is_a = (lane >= SM_A) & (lane < SM_B)
    is_b = (lane >= SM_B) & (lane < SM_GL)
    g = jnp.where(is_a, aneg_ref[...] * _softplus(zs + dtb_ref[...]), 0.0)
    gcum = _chunk_cumsum(g, row)
    gtot = gcum + _chunk_rev_cumsum(g, row) - g
    beta = _sigmoid(zs)
    out = jnp.where(is_a, gcum, jnp.where(is_b, beta, 0.0))
    out = out + pltpu.roll(jnp.where(is_a, gtot, 0.0), SM_GL - SM_A, 1)
    del s
    g_ref[...] = out
    gt_ref[...] = out.T


def _gates(zs3, a_neg, dt_bias):
    b, s, _ = zs3.shape
    return pl.pallas_call(
        _gates_body,
        grid=(b,),
        in_specs=[
            pl.BlockSpec((None, s, LANES), lambda i: (i, 0, 0)),
            pl.BlockSpec((1, LANES), lambda i: (0, 0)),
            pl.BlockSpec((1, LANES), lambda i: (0, 0)),
        ],
        out_specs=[
            pl.BlockSpec((None, s, LANES), lambda i: (i, 0, 0)),
            pl.BlockSpec((None, LANES, s), lambda i: (i, 0, 0)),
        ],
        out_shape=[
            jax.ShapeDtypeStruct((b, s, LANES), F32),
            jax.ShapeDtypeStruct((b, LANES, s), F32),
        ],
        compiler_params=_cparams(("parallel",)),
        name="dn_gates",
    )(zs3, a_neg, dt_bias)


def _gla_body(hb, scale, q_ref, k_ref, v_ref, lr_ref, w2_ref, b_ref, nrm_ref, o_ref, st_ref):
    s = q_ref.shape[0]
    dk = q_ref.shape[1] // hb
    dv = v_ref.shape[1] // hb
    per = TILE // CHUNK
    heads = range(hb)
    kcol = [pl.ds(h * dk, dk) for h in heads]
    vcol = [pl.ds(h * dv, dv) for h in heads]
    st_ref[...] = jnp.zeros_like(st_ref)
    row = lax.broadcasted_iota(jnp.int32, (TILE, dk), 0) % CHUNK
    ri = lax.broadcasted_iota(jnp.int32, (TILE, TILE), 0)
    ci = lax.broadcasted_iota(jnp.int32, (TILE, TILE), 1)
    causal = ((ri // CHUNK) == (ci // CHUNK)) & (ci <= ri)

    def tile(t, carry):
        base = pl.multiple_of(t * TILE, TILE)
        r = pl.ds(base, TILE)
        lrb = lr_ref[r, :].astype(BF16)
        q = [q_ref[r, kcol[h]].astype(F32) * scale for h in heads]
        k = [k_ref[r, kcol[h]].astype(F32) for h in heads]
        v = [v_ref[r, vcol[h]] for h in heads]
        pre = [_dot(lrb, w2_ref[:, kcol[h]]) + b_ref[:, kcol[h]] for h in heads]
        bc = [_chunk_cumsum(_log_sigmoid(x) * (1.0 / GLA_TAU), row) for x in pre]
        q_in = [(q[h] * jnp.exp(bc[h])).astype(BF16) for h in heads]
        k_in = [(k[h] * jnp.exp(-bc[h])).astype(BF16) for h in heads]
        a = [jnp.where(causal, _dot_nt(q_in[h], k_in[h]), 0.0).astype(BF16) for h in heads]
        o_intra = [_dot(a[h], v[h]) for h in heads]
        for c in range(per):
            lo, hi = c * CHUNK, (c + 1) * CHUNK
            b_last = [bc[h][hi - 1:hi, :] for h in heads]
            k_dec = [(k[h][lo:hi, :] * jnp.exp(b_last[h] - bc[h][lo:hi, :])).astype(BF16) for h in heads]
            st = [st_ref[h] for h in heads]
            o = [o_intra[h][lo:hi, :] + _dot_nt(q_in[h][lo:hi, :], st[h].astype(BF16)) for h in heads]
            for h in heads:
                st_ref[h] = st[h] * jnp.exp(b_last[h]) + _dot_tn(v[h][lo:hi, :], k_dec[h])
            for h in heads:
                o_ref[pl.ds(base + lo, CHUNK), vcol[h]] = _rms(o[h], nrm_ref[...]).astype(o_ref.dtype)
        return carry

    lax.fori_loop(0, s // TILE, tile, 0)


def _gla(z3, zs3, w2p, gla_b, gla_norm, d_model, hb=2):
    b, s, _ = z3.shape
    dk = d_model // (2 * GLA_HEADS)
    dv = d_model // GLA_HEADS
    assert s % TILE == 0 and GLA_HEADS % hb == 0
    qk_cols = GLA_HEADS * dk
    kw, vw = hb * dk, hb * dv
    body = functools.partial(_gla_body, hb, dk ** -0.5)
    return pl.pallas_call(
        body,
        grid=(b, GLA_HEADS // hb),
        in_specs=[
            pl.BlockSpec((None, s, kw), lambda i, h: (i, 0, h)),
            pl.BlockSpec((None, s, kw), lambda i, h: (i, 0, qk_cols // kw + h)),
            pl.BlockSpec((None, s, vw), lambda i, h: (i, 0, 2 * qk_cols // vw + h)),
            pl.BlockSpec((None, s, LANES), lambda i, h: (i, 0, 0)),
            pl.BlockSpec((LANES, kw), lambda i, h: (0, h)),
            pl.BlockSpec((1, kw), lambda i, h: (0, h)),
            pl.BlockSpec((1, dv), lambda i, h: (0, 0)),
        ],
        out_specs=pl.BlockSpec((None, s, vw), lambda i, h: (i, 0, h)),
        out_shape=jax.ShapeDtypeStruct((b, s, GLA_HEADS * dv), BF16),
        scratch_shapes=[pltpu.VMEM((hb, dv, dk), F32)],
        compiler_params=_cparams(("parallel", "parallel")),
        name="gla_mixer",
    )(z3, z3, z3, zs3, w2p, gla_b, gla_norm)


def _conv_silu(x_ref, w_ref, cols, base, n_rows, first, buf):
    pbase = pl.multiple_of(jnp.maximum(base - 8, 0), 8)
    buf[pl.ds(0, 8), :] = jnp.where(first, 0.0, x_ref[pl.ds(pbase, 8), cols].astype(F32))
    buf[pl.ds(8, n_rows), :] = x_ref[pl.ds(base, n_rows), cols].astype(F32)
    w = w_ref[:, cols]
    acc = None
    for d in range(DN_CONV):
        term = buf[pl.ds(8 - d, n_rows), :] * w[DN_CONV - 1 - d:DN_CONV - d, :]
        acc = term if acc is None else acc + term
    return _silu(acc)


def _l2n(t):
    return t * lax.rsqrt(jnp.sum(t * t, axis=-1, keepdims=True) + EPS)


def _lane_pick(x, idx):
    lane = lax.broadcasted_iota(jnp.int32, x.shape, 1)
    return jnp.sum(jnp.where(lane == idx, x, 0.0), axis=-1, keepdims=True)


def _bdot(a, b):
    return lax.dot_general(a, b, (((2,), (1,)), ((0,), (0,))), preferred_element_type=F32)


def _bdot_nt(a, b):
    return lax.dot_general(a, b, (((2,), (2,)), ((0,), (0,))), preferred_element_type=F32)


def _dn_body(hb, qscale, q_ref, k_ref, v_ref, cq_ref, ck_ref, cv_ref, g_ref, gt_ref, nrm_ref,
             o_ref, kp_s, np_s, qp_s, op_s, fl_s, st_s, cbuf):
    s = q_ref.shape[0]
    n_half = DN_TILE // HALF
    per_tile = DN_TILE // CHUNK
    head0 = pl.program_id(1) * hb

    ri = lax.broadcasted_iota(jnp.int32, (HALF, HALF), 0)
    ci = lax.broadcasted_iota(jnp.int32, (HALF, HALF), 1)
    same = (ri // CHUNK) == (ci // CHUNK)
    incl = same & (ci <= ri)
    strict = same & (ci < ri)
    n_sq = CHUNK.bit_length() - 2
    b3 = lambda x: x.reshape(n_half, HALF, x.shape[-1])

    n_tiles = s // DN_TILE
    n_units = hb * n_tiles

    def unit_coords(u):
        u = jnp.asarray(u, jnp.int32)
        i = lax.div(u, n_tiles)
        t = lax.rem(u, n_tiles)
        return i, t, pl.ds(pl.multiple_of(i * LANES, LANES), LANES)

    def prep(u):
        i, t, cols = unit_coords(u)
        head = head0 + i
        base = pl.multiple_of(t * DN_TILE, DN_TILE)
        rows = pl.ds(base, DN_TILE)
        first = t == 0
        q = _l2n(_conv_silu(q_ref, cq_ref, cols, base, DN_TILE, first, cbuf.at[0])) * qscale
        k = _l2n(_conv_silu(k_ref, ck_ref, cols, base, DN_TILE, first, cbuf.at[1]))
        v = _conv_silu(v_ref, cv_ref, cols, base, DN_TILE, first, cbuf.at[2])
        gs = g_ref[rows, :]
        gc = _lane_pick(gs, SM_A + head)
        bt = _lane_pick(gs, SM_B + head)
        gl = _lane_pick(gs, SM_GL + head)
        eg = jnp.exp(gc)
        kb = k * bt
        vb = v * bt
        kbe = kb * eg
        qd = q * eg
        kd = (k * jnp.exp(gl - gc)).astype(BF16)
        flb = jnp.broadcast_to(jnp.exp(gl), (DN_TILE, LANES))
        grow_t = gt_ref[pl.ds(SM_A + head, 1), rows]
        grow = jnp.stack([grow_t[:, j * HALF:(j + 1) * HALF] for j in range(n_half)], axis=0)
        dec = jnp.exp(jnp.where(incl, b3(gc) - grow, -jnp.inf))
        kbf = b3(k).astype(BF16)
        a = jnp.where(strict, _bdot_nt(b3(kb).astype(BF16), kbf) * dec, 0.0)
        attn = (_bdot_nt(b3(q).astype(BF16), kbf) * dec).astype(BF16)
        p = -a
        r = p
        for _ in range(n_sq):
            pb = p.astype(BF16)
            p = _bdot(pb, pb)
            r = r + p + _bdot(r.astype(BF16), p.astype(BF16))
        rhs = jnp.concatenate([b3(vb), b3(kbe)], axis=2)
        uw = (rhs + _bdot(r.astype(BF16), rhs.astype(BF16))).astype(BF16)
        aw = _bdot(attn, uw)
        op_s[rows, cols] = aw[:, :, :LANES].reshape(DN_TILE, LANES).astype(op_s.dtype)
        qp_s[rows, cols] = (qd - aw[:, :, LANES:].reshape(DN_TILE, LANES)).astype(qp_s.dtype)
        uw2 = uw.reshape(DN_TILE, 2 * LANES)
        for c in range(per_tile):
            c0 = c * CHUNK
            kn = _dot_tn(kd[c0:c0 + CHUNK, :], uw2[c0:c0 + CHUNK, :])
            cidx = t * per_tile + c
            np_s[i, cidx] = kn[:, :LANES].astype(np_s.dtype)
            kp_s[i, cidx] = kn[:, LANES:].astype(kp_s.dtype)
            fl_s[i, pl.ds(cidx, 1), :] = flb[c0:c0 + 1, :]

    def prep_step(u, carry):
        prep(u)
        return carry

    lax.fori_loop(0, n_units, prep_step, 0)

    st_s[...] = jnp.zeros_like(st_s)

    def scan(c, carry):
        r = pl.ds(pl.multiple_of(c * CHUNK, CHUNK), CHUNK)
        for i in range(hb):
            cols = pl.ds(i * LANES, LANES)
            st = st_s[i]
            stb = st.astype(BF16)
            o = _dot(qp_s[r, cols], stb) + op_s[r, cols].astype(F32)
            st_s[i] = st * fl_s[i, pl.ds(c, 1), :] - _dot(kp_s[i, c], stb) + np_s[i, c]
            o_ref[r, cols] = _rms(o, nrm_ref[...]).astype(o_ref.dtype)
        return carry

    lax.fori_loop(0, s // CHUNK, scan, 0, unroll=SCAN_UNROLL)


def _deltanet(z3, dn_conv, g3, gt3, dn_norm, d_model, qkv_off, hb=4):
    b, s, _ = z3.shape
    dh = d_model // DN_HEADS
    assert dh == LANES and s % DN_TILE == 0
    bw = hb * dh
    nblk = d_model // bw
    n_chunks = s // CHUNK
    body = functools.partial(_dn_body, hb, dh ** -0.5)
    z_spec = lambda part: pl.BlockSpec(
        (None, s, bw), lambda i, h: (i, 0, (qkv_off + part * d_model) // bw + h))
    c_spec = lambda part: pl.BlockSpec((DN_CONV, bw), lambda i, h: (0, part * nblk + h))
    return pl.pallas_call(
        body,
        grid=(b, nblk),
        in_specs=[
            z_spec(0), z_spec(1), z_spec(2),
            c_spec(0), c_spec(1), c_spec(2),
            pl.BlockSpec((None, s, LANES), lambda i, h: (i, 0, 0)),
            pl.BlockSpec((None, LANES, s), lambda i, h: (i, 0, 0)),
            pl.BlockSpec((1, dh), lambda i, h: (0, 0)),
        ],
        out_specs=pl.BlockSpec((None, s, bw), lambda i, h: (i, 0, h)),
        out_shape=jax.ShapeDtypeStruct((b, s, d_model), BF16),
        scratch_shapes=[
            pltpu.VMEM((hb, n_chunks, dh, dh), BF16),
            pltpu.VMEM((hb, n_chunks, dh, dh), F32),
            pltpu.VMEM((s, bw), BF16),
            pltpu.VMEM((s, bw), BF16),
            pltpu.VMEM((hb, n_chunks, LANES), F32),
            pltpu.VMEM((hb, dh, dh), F32),
            pltpu.VMEM((3, DN_TILE + 8, LANES), F32),
        ],
        compiler_params=_cparams(("parallel", "parallel")),
        name="deltanet_mixer",
    )(z3, z3, z3, dn_conv, dn_conv, dn_conv, g3, gt3, dn_norm)


def _merge_body(og_ref, od_ref, sg_ref, sz_ref, ga_ref, gb_ref, x_ref, w_ref, g_ref, x1_ref, h2_ref):
    f = lambda r: r[...].astype(F32)
    mixed = f(ga_ref) * (f(og_ref) * f(sg_ref)) + f(gb_ref) * (f(od_ref) * f(sz_ref))
    x1 = x_ref[...] + _dot(mixed.astype(BF16), w_ref[...])
    x1_ref[...] = x1
    h2_ref[...] = _rms(x1, g_ref[...]).astype(BF16)


def _merge(o_gla, o_dn, z2d, act_off, x2d, w_out, g_mlp, tm=256):
    t, d = x2d.shape
    row = lambda c: pl.BlockSpec((tm, d), lambda i, c=c: (i, c))
    a0 = act_off // d
    return pl.pallas_call(
        _merge_body,
        grid=(t // tm,),
        in_specs=[
            row(0), row(0), row(a0), row(a0 + 1), row(a0 + 2), row(a0 + 3), row(0),
            pl.BlockSpec((d, d), lambda i: (0, 0)),
            pl.BlockSpec((1, d), lambda i: (0, 0)),
        ],
        out_specs=[row(0), row(0)],
        out_shape=[jax.ShapeDtypeStruct((t, d), F32), jax.ShapeDtypeStruct((t, d), BF16)],
        compiler_params=_cparams(("parallel",)),
        name="merge_out_proj",
    )(o_gla, o_dn, z2d, z2d, z2d, z2d, x2d, w_out, g_mlp)


def _mlp_body(h_ref, wu_ref, wd_ref, x_ref, o_ref):
    j = pl.program_id(1)
    mid = jnp.square(jnp.maximum(_dot(h_ref[...], wu_ref[...]), 0.0)).astype(BF16)
    part = _dot(mid, wd_ref[...])

    @pl.when(j == 0)
    def _():
        o_ref[...] = x_ref[...] + part

    @pl.when(j > 0)
    def _():
        o_ref[...] += part


def _mlp(h2, w_up, w_down, x1, tm=512, tf=1024):
    t, d = x1.shape
    ff = w_up.shape[1]
    return pl.pallas_call(
        _mlp_body,
        grid=(t // tm, ff // tf),
        in_specs=[
            pl.BlockSpec((tm, d), lambda i, j: (i, 0)),
            pl.BlockSpec((d, tf), lambda i, j: (0, j)),
            pl.BlockSpec((tf, d), lambda i, j: (j, 0)),
            pl.BlockSpec((tm, d), lambda i, j: (i, 0)),
        ],
        out_specs=pl.BlockSpec((tm, d), lambda i, j: (i, 0)),
        out_shape=jax.ShapeDtypeStruct((t, d), F32),
        compiler_params=_cparams(("parallel", "arbitrary")),
        name="relu2_mlp",
    )(h2, w_up, w_down, x1)


def _ple_body(x_ref, p_ref, wg_ref, wp_ref, gp_ref, gf_ref, o_ref):
    x2 = x_ref[...]
    h3 = _rms(x2, gp_ref[...]).astype(BF16)
    gate = _sigmoid(_dot(h3, wg_ref[...]))
    proj = _dot(p_ref[...].astype(BF16), wp_ref[...])
    o_ref[...] = _rms(x2 + gate * proj, gf_ref[...])


def _ple(x2, p2d, w_gate, w_proj, g_ple, g_final, tm=256):
    t, d = x2.shape
    pd = p2d.shape[1]
    return pl.pallas_call(
        _ple_body,
        grid=(t // tm,),
        in_specs=[
            pl.BlockSpec((tm, d), lambda i: (i, 0)),
            pl.BlockSpec((tm, pd), lambda i: (i, 0)),
            pl.BlockSpec((d, d), lambda i: (0, 0)),
            pl.BlockSpec((pd, d), lambda i: (0, 0)),
            pl.BlockSpec((1, d), lambda i: (0, 0)),
            pl.BlockSpec((1, d), lambda i: (0, 0)),
        ],
        out_specs=pl.BlockSpec((tm, d), lambda i: (i, 0)),
        out_shape=jax.ShapeDtypeStruct((t, d), F32),
        compiler_params=_cparams(("parallel",)),
        name="ple_final_norm",
    )(x2, p2d, w_gate, w_proj, g_ple, g_final)


def _layer(x, p_i, g_mix, w_in, gla_w2, gla_b, gla_norm, dn_conv, dn_a_log, dn_dt_bias, dn_norm,
           w_out, g_mlp, w_up, w_down, g_ple, w_ple_gate, w_ple_proj, g_out):
    b, s, d = x.shape
    t = b * s
    gla_qk = d // 2
    dn_qkv = 3 * d
    o_q, o_k = 0, gla_qk
    o_v = 2 * gla_qk
    o_g = o_v + d
    o_lr = o_g + d
    o_dn = o_lr + GLA_LOWRANK
    o_z = o_dn + dn_qkv
    o_a = o_z + d
    o_b = o_a + DN_HEADS
    o_ga = o_b + DN_HEADS
    o_gb = o_ga + d
    cs = lambda lo, n: w_in[:, lo:lo + n]
    w_big = jnp.concatenate(
        [cs(o_q, gla_qk), cs(o_k, gla_qk), cs(o_v, d), cs(o_dn, dn_qkv),
         cs(o_g, d), cs(o_z, d), cs(o_ga, d), cs(o_gb, d)], axis=1).astype(BF16)
    w_small = jnp.concatenate(
        [cs(o_lr, GLA_LOWRANK), cs(o_a, DN_HEADS), cs(o_b, DN_HEADS),
         jnp.zeros((d, LANES - SM_GL), w_in.dtype)], axis=1).astype(BF16)
    n_plain = 2 * gla_qk + d + dn_qkv
    qkv_off = 2 * gla_qk + d

    x2d = x.reshape(t, d)
    z2d, zs2d = _in_proj(x2d, g_mix.reshape(1, d), w_big, w_small, n_plain, 2 * d)
    z3 = z2d.reshape(b, s, -1)
    zs3 = zs2d.reshape(b, s, LANES)

    pad = lambda v: jnp.zeros((1, LANES), F32).at[0, SM_A:SM_B].set(v.astype(F32))
    g3, gt3 = _gates(zs3, pad(-jnp.exp(dn_a_log.astype(F32))), pad(dn_dt_bias))

    w2p = jnp.zeros((LANES, gla_qk), F32).at[:GLA_LOWRANK].set(gla_w2).astype(BF16)
    o_gla = _gla(z3, zs3, w2p, gla_b.reshape(1, -1).astype(F32), gla_norm.reshape(1, -1).astype(F32), d)
    o_dn = _deltanet(z3, dn_conv.astype(F32), g3, gt3, dn_norm.reshape(1, -1).astype(F32), d, qkv_off)

    x1, h2 = _merge(o_gla.reshape(t, d), o_dn.reshape(t, d), z2d, n_plain, x2d,
                    w_out.astype(BF16), g_mlp.reshape(1, d))
    x2 = _mlp(h2, w_up.astype(BF16), w_down.astype(BF16), x1)
    out = _ple(x2, p_i.reshape(t, -1), w_ple_gate.astype(BF16), w_ple_proj.astype(BF16),
               g_ple.reshape(1, d), g_out.reshape(1, d))
    return out.reshape(b, s, d)


def kernel(x, p, g_mix, w_in, gla_w2, gla_b, gla_norm, dn_conv, dn_a_log, dn_dt_bias, dn_norm,
           w_out, g_mlp, w_up, w_down, g_ple, w_ple_gate, w_ple_proj, g_final):
    depth = w_in.shape[0]
    assert depth == 1, "the final rms_norm is fused into the last layer's kernel"
    return _layer(x, p[0], g_mix[0], w_in[0], gla_w2[0], gla_b[0], gla_norm[0], dn_conv[0],
                  dn_a_log[0], dn_dt_bias[0], dn_norm[0], w_out[0], g_mlp[0], w_up[0], w_down[0],
                  g_ple[0], w_ple_gate[0], w_ple_proj[0], g_final)
```

```python
import functools

import jax
import jax.numpy as jnp
from jax import lax
from jax.experimental import pallas as pl
from jax.experimental.pallas import tpu as pltpu

F32 = jnp.float32
BF16 = jnp.bfloat16

EPS = 1e-6
CHUNK = 64
GLA_HEADS = 4
GLA_LOWRANK = 16
GLA_TAU = 16.0
DN_HEADS = 16
DN_CONV = 4
LANES = 128
TILE = 256
HALF = 128
DN_TILE = 2048
SCAN_UNROLL = 4
VMEM_LIMIT = 56 * 1024 * 1024

SM_LR = 0
SM_A = GLA_LOWRANK
SM_B = SM_A + DN_HEADS
SM_GL = SM_B + DN_HEADS


def _cparams(sem):
    return pltpu.CompilerParams(dimension_semantics=sem, vmem_limit_bytes=VMEM_LIMIT)


def _dot(a, b):
    return jnp.dot(a, b, preferred_element_type=F32)


def _dot_nt(a, b):
    return lax.dot_general(a, b, (((1,), (1,)), ((), ())), preferred_element_type=F32)


def _dot_tn(a, b):
    return lax.dot_general(a, b, (((0,), (0,)), ((), ())), preferred_element_type=F32)


def _sigmoid(x):
    return 1.0 / (1.0 + jnp.exp(-x))


def _silu(x):
    return x * _sigmoid(x)


def _log_sigmoid(x):
    return jnp.minimum(x, 0.0) - jnp.log(1.0 + jnp.exp(-jnp.abs(x)))


def _softplus(x):
    return jnp.maximum(x, 0.0) + jnp.log(1.0 + jnp.exp(-jnp.abs(x)))


def _rms(x, g):
    return x * lax.rsqrt(jnp.mean(x * x, axis=-1, keepdims=True) + EPS) * g


def _chunk_cumsum(x, row):
    n = x.shape[0]
    shift = 1
    while shift < CHUNK:
        x = x + jnp.where(row >= shift, pltpu.roll(x, shift, 0), 0.0)
        shift *= 2
    del n
    return x


def _chunk_cumsum_mxu(tri, x):
    hi = x.astype(BF16)
    rest = x - hi.astype(F32)
    mid = rest.astype(BF16)
    lo = (rest - mid.astype(F32)).astype(BF16)
    return _dot(tri, hi) + _dot(tri, mid) + _dot(tri, lo)


def _chunk_rev_cumsum(x, row):
    n = x.shape[0]
    shift = 1
    while shift < CHUNK:
        x = x + jnp.where(row < CHUNK - shift, pltpu.roll(x, n - shift, 0), 0.0)
        shift *= 2
    return x


ACT_NONE, ACT_SILU, ACT_SIGMOID = 0, 1, 2
_ACT_FN = {ACT_NONE: lambda v: v, ACT_SILU: _silu, ACT_SIGMOID: _sigmoid}


def _in_proj_body(tile_acts, x_ref, g_ref, wb_ref, ws_ref, z_ref, zs_ref, h_ref):
    j = pl.program_id(1)

    @pl.when(j == 0)
    def _():
        hb = _rms(x_ref[...], g_ref[...]).astype(BF16)
        h_ref[...] = hb
        zs_ref[...] = _dot(hb, ws_ref[...])

    for act, fn in _ACT_FN.items():
        tiles = [jt for jt, a in enumerate(tile_acts) if a == act]
        if not tiles:
            continue
        hit = functools.reduce(lambda p, q: p | q, [j == jt for jt in tiles])

        @pl.when(hit)
        def _(fn=fn):
            z_ref[...] = fn(_dot(h_ref[...], wb_ref[...])).astype(BF16)


def _in_proj(x2d, g_mix, w_big, w_small, segments, tm=1024, tn=1024):
    t, d = x2d.shape
    n = w_big.shape[1]
    assert all(lo % tn == 0 for lo, _ in segments) and n % tn == 0
    starts = [lo // tn for lo, _ in segments] + [n // tn]
    tile_acts = tuple(act for (_, act), a, b in zip(segments, starts, starts[1:]) for _ in range(b - a))
    body = functools.partial(_in_proj_body, tile_acts)
    return pl.pallas_call(
        body,
        grid=(t // tm, n // tn),
        in_specs=[
            pl.BlockSpec((tm, d), lambda i, j: (i, 0)),
            pl.BlockSpec((1, d), lambda i, j: (0, 0)),
            pl.BlockSpec((d, tn), lambda i, j: (0, j)),
            pl.BlockSpec((d, LANES), lambda i, j: (0, 0)),
        ],
        out_specs=[
            pl.BlockSpec((tm, tn), lambda i, j: (i, j)),
            pl.BlockSpec((tm, LANES), lambda i, j: (i, 0)),
        ],
        out_shape=[
            jax.ShapeDtypeStruct((t, n), BF16),
            jax.ShapeDtypeStruct((t, LANES), F32),
        ],
        scratch_shapes=[pltpu.VMEM((tm, d), BF16)],
        compiler_params=_cparams(("parallel", "arbitrary")),
        name="in_proj",
    )(x2d, g_mix, w_big, w_small)


def _gates_body(zs_ref, aneg_ref, dtb_ref, g_ref, gt_ref):
    zs = zs_ref[...]
    s = zs.shape[0]
    lane = lax.broadcasted_iota(jnp.int32, zs.shape, 1)
    row = lax.broadcasted_iota(jnp.int32, zs.shape, 0) % CHUNK
    is_a = (lane >= SM_A) & (lane < SM_B)
    is_b = (lane >= SM_B) & (lane < SM_GL)
    g = jnp.where(is_a, aneg_ref[...] * _softplus(zs + dtb_ref[...]), 0.0)
    gcum = _chunk_cumsum(g, row)
    gtot = gcum + _chunk_rev_cumsum(g, row) - g
    beta = _sigmoid(zs)
    out = jnp.where(is_a, gcum, jnp.where(is_b, beta, 0.0))
    out = out + pltpu.roll(jnp.where(is_a, gtot, 0.0), SM_GL - SM_A, 1)
    del s
    g_ref[...] = out
    gt_ref[...] = out.T


def _gates(zs3, a_neg, dt_bias):
    b, s, _ = zs3.shape
    return pl.pallas_call(
        _gates_body,
        grid=(b,),
        in_specs=[
            pl.BlockSpec((None, s, LANES), lambda i: (i, 0, 0)),
            pl.BlockSpec((1, LANES), lambda i: (0, 0)),
            pl.BlockSpec((1, LANES), lambda i: (0, 0)),
        ],
        out_specs=[
            pl.BlockSpec((None, s, LANES), lambda i: (i, 0, 0)),
            pl.BlockSpec((None, LANES, s), lambda i: (i, 0, 0)),
        ],
        out_shape=[
            jax.ShapeDtypeStruct((b, s, LANES), F32),
            jax.ShapeDtypeStruct((b, LANES, s), F32),
        ],
        compiler_params=_cparams(("parallel",)),
        name="dn_gates",
    )(zs3, a_neg, dt_bias)


def _gla_body(hb, scale, q_ref, k_ref, v_ref, lr_ref, w2_ref, b_ref, nrm_ref, o_ref, st_ref):
    s = q_ref.shape[0]
    dk = q_ref.shape[1] // hb
    dv = v_ref.shape[1] // hb
    per = TILE // CHUNK
    heads = range(hb)
    kcol = [pl.ds(h * dk, dk) for h in heads]
    vcol = [pl.ds(h * dv, dv) for h in heads]
    st_ref[...] = jnp.zeros_like(st_ref)
    row = lax.broadcasted_iota(jnp.int32, (TILE, dk), 0) % CHUNK
    ri = lax.broadcasted_iota(jnp.int32, (TILE, TILE), 0)
    ci = lax.broadcasted_iota(jnp.int32, (TILE, TILE), 1)
    causal = ((ri // CHUNK) == (ci // CHUNK)) & (ci <= ri)

    def tile(t, carry):
        base = pl.multiple_of(t * TILE, TILE)
        r = pl.ds(base, TILE)
        lrb = lr_ref[r, :].astype(BF16)
        q = [q_ref[r, kcol[h]].astype(F32) * scale for h in heads]
        k = [k_ref[r, kcol[h]].astype(F32) for h in heads]
        v = [v_ref[r, vcol[h]] for h in heads]
        pre = [_dot(lrb, w2_ref[:, kcol[h]]) + b_ref[:, kcol[h]] for h in heads]
        bc = [_chunk_cumsum(_log_sigmoid(x) * (1.0 / GLA_TAU), row) for x in pre]
        q_in = [(q[h] * jnp.exp(bc[h])).astype(BF16) for h in heads]
        k_in = [(k[h] * jnp.exp(-bc[h])).astype(BF16) for h in heads]
        a = [jnp.where(causal, _dot_nt(q_in[h], k_in[h]), 0.0).astype(BF16) for h in heads]
        o_intra = [_dot(a[h], v[h]) for h in heads]
        for c in range(per):
            lo, hi = c * CHUNK, (c + 1) * CHUNK
            b_last = [bc[h][hi - 1:hi, :] for h in heads]
            k_dec = [(k[h][lo:hi, :] * jnp.exp(b_last[h] - bc[h][lo:hi, :])).astype(BF16) for h in heads]
            st = [st_ref[h] for h in heads]
            o = [o_intra[h][lo:hi, :] + _dot_nt(q_in[h][lo:hi, :], st[h].astype(BF16)) for h in heads]
            for h in heads:
                st_ref[h] = st[h] * jnp.exp(b_last[h]) + _dot_tn(v[h][lo:hi, :], k_dec[h])
            for h in heads:
                o_ref[pl.ds(base + lo, CHUNK), vcol[h]] = _rms(o[h], nrm_ref[...]).astype(o_ref.dtype)
        return carry

    lax.fori_loop(0, s // TILE, tile, 0)


def _gla(z3, zs3, w2p, gla_b, gla_norm, d_model, hb=2):
    b, s, _ = z3.shape
    dk = d_model // (2 * GLA_HEADS)
    dv = d_model // GLA_HEADS
    assert s % TILE == 0 and GLA_HEADS % hb == 0
    qk_cols = GLA_HEADS * dk
    kw, vw = hb * dk, hb * dv
    body = functools.partial(_gla_body, hb, dk ** -0.5)
    return pl.pallas_call(
        body,
        grid=(b, GLA_HEADS // hb),
        in_specs=[
            pl.BlockSpec((None, s, kw), lambda i, h: (i, 0, h)),
            pl.BlockSpec((None, s, kw), lambda i, h: (i, 0, qk_cols // kw + h)),
            pl.BlockSpec((None, s, vw), lambda i, h: (i, 0, 2 * qk_cols // vw + h)),
            pl.BlockSpec((None, s, LANES), lambda i, h: (i, 0, 0)),
            pl.BlockSpec((LANES, kw), lambda i, h: (0, h)),
            pl.BlockSpec((1, kw), lambda i, h: (0, h)),
            pl.BlockSpec((1, dv), lambda i, h: (0, 0)),
        ],
        out_specs=pl.BlockSpec((None, s, vw), lambda i, h: (i, 0, h)),
        out_shape=jax.ShapeDtypeStruct((b, s, GLA_HEADS * dv), BF16),
        scratch_shapes=[pltpu.VMEM((hb, dv, dk), F32)],
        compiler_params=_cparams(("parallel", "parallel")),
        name="gla_mixer",
    )(z3, z3, z3, zs3, w2p, gla_b, gla_norm)


def _conv_silu(x_ref, w_ref, cols, base, n_rows, first, buf):
    pbase = pl.multiple_of(jnp.maximum(base - 8, 0), 8)
    buf[pl.ds(0, 8), :] = jnp.where(first, 0.0, x_ref[pl.ds(pbase, 8), cols].astype(F32))
    buf[pl.ds(8, n_rows), :] = x_ref[pl.ds(base, n_rows), cols].astype(F32)
    w = w_ref[:, cols]
    acc = None
    for d in range(DN_CONV):
        term = buf[pl.ds(8 - d, n_rows), :] * w[DN_CONV - 1 - d:DN_CONV - d, :]
        acc = term if acc is None else acc + term
    return _silu(acc)


def _l2n(t):
    return t * lax.rsqrt(jnp.sum(t * t, axis=-1, keepdims=True) + EPS)


def _lane_pick(x, idx):
    lane = lax.broadcasted_iota(jnp.int32, x.shape, 1)
    return jnp.sum(jnp.where(lane == idx, x, 0.0), axis=-1, keepdims=True)


def _bdot(a, b):
    return lax.dot_general(a, b, (((2,), (1,)), ((0,), (0,))), preferred_element_type=F32)


def _bdot_nt(a, b):
    return lax.dot_general(a, b, (((2,), (2,)), ((0,), (0,))), preferred_element_type=F32)


def _dn_body(hb, qscale, q_ref, k_ref, v_ref, cq_ref, ck_ref, cv_ref, g_ref, gt_ref, nrm_ref,
             o_ref, kp_s, np_s, qp_s, op_s, fl_s, st_s, cbuf):
    s = q_ref.shape[0]
    n_half = DN_TILE // HALF
    per_tile = DN_TILE // CHUNK
    head0 = pl.program_id(1) * hb

    ri = lax.broadcasted_iota(jnp.int32, (HALF, HALF), 0)
    ci = lax.broadcasted_iota(jnp.int32, (HALF, HALF), 1)
    same = (ri // CHUNK) == (ci // CHUNK)
    incl = same & (ci <= ri)
    strict = same & (ci < ri)
    n_sq = CHUNK.bit_length() - 2
    b3 = lambda x: x.reshape(n_half, HALF, x.shape[-1])

    n_tiles = s // DN_TILE
    n_units = hb * n_tiles

    def unit_coords(u):
        u = jnp.asarray(u, jnp.int32)
        i = lax.div(u, n_tiles)
        t = lax.rem(u, n_tiles)
        return i, t, pl.ds(pl.multiple_of(i * LANES, LANES), LANES)

    def prep(u):
        i, t, cols = unit_coords(u)
        head = head0 + i
        base = pl.multiple_of(t * DN_TILE, DN_TILE)
        rows = pl.ds(base, DN_TILE)
        first = t == 0
        q = _l2n(_conv_silu(q_ref, cq_ref, cols, base, DN_TILE, first, cbuf.at[0])) * qscale
        k = _l2n(_conv_silu(k_ref, ck_ref, cols, base, DN_TILE, first, cbuf.at[1]))
        v = _conv_silu(v_ref, cv_ref, cols, base, DN_TILE, first, cbuf.at[2])
        gs = g_ref[rows, :]
        gc = _lane_pick(gs, SM_A + head)
        bt = _lane_pick(gs, SM_B + head)
        gl = _lane_pick(gs, SM_GL + head)
        eg = jnp.exp(gc)
        kb = k * bt
        vb = v * bt
        kbe = kb * eg
        qd = q * eg
        kd = (k * jnp.exp(gl - gc)).astype(BF16)
        flb = jnp.broadcast_to(jnp.exp(gl), (DN_TILE, LANES))
        grow_t = gt_ref[pl.ds(SM_A + head, 1), rows]
        grow = jnp.stack([grow_t[:, j * HALF:(j + 1) * HALF] for j in range(n_half)], axis=0)
        dec = jnp.exp(jnp.where(incl, b3(gc) - grow, -jnp.inf))
        kbf = b3(k).astype(BF16)
        a = jnp.where(strict, _bdot_nt(b3(kb).astype(BF16), kbf) * dec, 0.0)
        attn = (_bdot_nt(b3(q).astype(BF16), kbf) * dec).astype(BF16)
        p = -a
        r = p
        for _ in range(n_sq):
            pb = p.astype(BF16)
            p = _bdot(pb, pb)
            r = r + p + _bdot(r.astype(BF16), p.astype(BF16))
        rhs = jnp.concatenate([b3(vb), b3(kbe)], axis=2)
        uw = (rhs + _bdot(r.astype(BF16), rhs.astype(BF16))).astype(BF16)
        aw = _bdot(attn, uw)
        op_s[rows, cols] = aw[:, :, :LANES].reshape(DN_TILE, LANES).astype(op_s.dtype)
        qp_s[rows, cols] = (qd - aw[:, :, LANES:].reshape(DN_TILE, LANES)).astype(qp_s.dtype)
        uw2 = uw.reshape(DN_TILE, 2 * LANES)
        for c in range(per_tile):
            c0 = c * CHUNK
            kn = _dot_tn(kd[c0:c0 + CHUNK, :], uw2[c0:c0 + CHUNK, :])
            cidx = t * per_tile + c
            np_s[i, cidx] = kn[:, :LANES].astype(np_s.dtype)
            kp_s[i, cidx] = kn[:, LANES:].astype(kp_s.dtype)
            fl_s[i, pl.ds(cidx, 1), :] = flb[c0:c0 + 1, :]

    def prep_step(u, carry):
        prep(u)
        return carry

    lax.fori_loop(0, n_units, prep_step, 0)

    st_s[...] = jnp.zeros_like(st_s)

    def scan(c, carry):
        r = pl.ds(pl.multiple_of(c * CHUNK, CHUNK), CHUNK)
        for i in range(hb):
            cols = pl.ds(i * LANES, LANES)
            st = st_s[i]
            stb = st.astype(BF16)
            o = _dot(qp_s[r, cols], stb) + op_s[r, cols].astype(F32)
            st_s[i] = st * fl_s[i, pl.ds(c, 1), :] - _dot(kp_s[i, c], stb) + np_s[i, c]
            o_ref[r, cols] = _rms(o, nrm_ref[...]).astype(o_ref.dtype)
        return carry

    lax.fori_loop(0, s // CHUNK, scan, 0, unroll=SCAN_UNROLL)


def _deltanet(z3, dn_conv, g3, gt3, dn_norm, d_model, qkv_off, hb=4):
    b, s, _ = z3.shape
    dh = d_model // DN_HEADS
    assert dh == LANES and s % DN_TILE == 0
    bw = hb * dh
    nblk = d_model // bw
    n_chunks = s // CHUNK
    body = functools.partial(_dn_body, hb, dh ** -0.5)
    z_spec = lambda part: pl.BlockSpec(
        (None, s, bw), lambda i, h: (i, 0, (qkv_off + part * d_model) // bw + h))
    c_spec = lambda part: pl.BlockSpec((DN_CONV, bw), lambda i, h: (0, part * nblk + h))
    return pl.pallas_call(
        body,
        grid=(b, nblk),
        in_specs=[
            z_spec(0), z_spec(1), z_spec(2),
            c_spec(0), c_spec(1), c_spec(2),
            pl.BlockSpec((None, s, LANES), lambda i, h: (i, 0, 0)),
            pl.BlockSpec((None, LANES, s), lambda i, h: (i, 0, 0)),
            pl.BlockSpec((1, dh), lambda i, h: (0, 0)),
        ],
        out_specs=pl.BlockSpec((None, s, bw), lambda i, h: (i, 0, h)),
        out_shape=jax.ShapeDtypeStruct((b, s, d_model), BF16),
        scratch_shapes=[
            pltpu.VMEM((hb, n_chunks, dh, dh), BF16),
            pltpu.VMEM((hb, n_chunks, dh, dh), F32),
            pltpu.VMEM((s, bw), BF16),
            pltpu.VMEM((s, bw), BF16),
            pltpu.VMEM((hb, n_chunks, LANES), F32),
            pltpu.VMEM((hb, dh, dh), F32),
            pltpu.VMEM((3, DN_TILE + 8, LANES), F32),
        ],
        compiler_params=_cparams(("parallel", "parallel")),
        name="deltanet_mixer",
    )(z3, z3, z3, dn_conv, dn_conv, dn_conv, g3, gt3, dn_norm)


def _merge_body(og_ref, od_ref, sg_ref, sz_ref, ga_ref, gb_ref, x_ref, w_ref, g_ref, x1_ref, h2_ref):
    f = lambda r: r[...].astype(F32)
    mixed = f(ga_ref) * (f(og_ref) * f(sg_ref)) + f(gb_ref) * (f(od_ref) * f(sz_ref))
    x1 = x_ref[...] + _dot(mixed.astype(BF16), w_ref[...])
    x1_ref[...] = x1
    h2_ref[...] = _rms(x1, g_ref[...]).astype(BF16)


def _merge(o_gla, o_dn, z2d, act_cols, x2d, w_out, g_mlp, tm=256):
    t, d = x2d.shape
    assert all(c % d == 0 for c in act_cols)
    row = lambda c: pl.BlockSpec((tm, d), lambda i, c=c: (i, c))
    sg, sz, ga, gb = (c // d for c in act_cols)
    return pl.pallas_call(
        _merge_body,
        grid=(t // tm,),
        in_specs=[
            row(0), row(0), row(sg), row(sz), row(ga), row(gb), row(0),
            pl.BlockSpec((d, d), lambda i: (0, 0)),
            pl.BlockSpec((1, d), lambda i: (0, 0)),
        ],
        out_specs=[row(0), row(0)],
        out_shape=[jax.ShapeDtypeStruct((t, d), F32), jax.ShapeDtypeStruct((t, d), BF16)],
        compiler_params=_cparams(("parallel",)),
        name="merge_out_proj",
    )(o_gla, o_dn, z2d, z2d, z2d, z2d, x2d, w_out, g_mlp)


def _mlp_body(h_ref, wu_ref, wd_ref, x_ref, o_ref):
    @pl.when(pl.program_id(1) == 0)
    def _():
        o_ref[...] = x_ref[...]

    mid = jnp.square(jnp.maximum(_dot(h_ref[...], wu_ref[...]), 0.0)).astype(BF16)
    o_ref[...] += _dot(mid, wd_ref[...])


def _mlp(h2, w_up, w_down, x1, tm=512, tf=1024):
    t, d = x1.shape
    ff = w_up.shape[1]
    return pl.pallas_call(
        _mlp_body,
        grid=(t // tm, ff // tf),
        in_specs=[
            pl.BlockSpec((tm, d), lambda i, j: (i, 0)),
            pl.BlockSpec((d, tf), lambda i, j: (0, j)),
            pl.BlockSpec((tf, d), lambda i, j: (j, 0)),
            pl.BlockSpec((tm, d), lambda i, j: (i, 0)),
        ],
        out_specs=pl.BlockSpec((tm, d), lambda i, j: (i, 0)),
        out_shape=jax.ShapeDtypeStruct((t, d), F32),
        compiler_params=_cparams(("parallel", "arbitrary")),
        name="relu2_mlp",
    )(h2, w_up, w_down, x1)


def _ple_body(x_ref, p_ref, wg_ref, wp_ref, gp_ref, gf_ref, o_ref):
    x2 = x_ref[...]
    h3 = _rms(x2, gp_ref[...]).astype(BF16)
    gate = _sigmoid(_dot(h3, wg_ref[...]))
    proj = _dot(p_ref[...].astype(BF16), wp_ref[...])
    o_ref[...] = _rms(x2 + gate * proj, gf_ref[...])


def _ple(x2, p2d, w_gate, w_proj, g_ple, g_final, tm=256):
    t, d = x2.shape
    pd = p2d.shape[1]
    return pl.pallas_call(
        _ple_body,
        grid=(t // tm,),
        in_specs=[
            pl.BlockSpec((tm, d), lambda i: (i, 0)),
            pl.BlockSpec((tm, pd), lambda i: (i, 0)),
            pl.BlockSpec((d, d), lambda i: (0, 0)),
            pl.BlockSpec((pd, d), lambda i: (0, 0)),
            pl.BlockSpec((1, d), lambda i: (0, 0)),
            pl.BlockSpec((1, d), lambda i: (0, 0)),
        ],
        out_specs=pl.BlockSpec((tm, d), lambda i: (i, 0)),
        out_shape=jax.ShapeDtypeStruct((t, d), F32),
        compiler_params=_cparams(("parallel",)),
        name="ple_final_norm",
    )(x2, p2d, w_gate, w_proj, g_ple, g_final)


def _layer(x, p_i, g_mix, w_in, gla_w2, gla_b, gla_norm, dn_conv, dn_a_log, dn_dt_bias, dn_norm,
           w_out, g_mlp, w_up, w_down, g_ple, w_ple_gate, w_ple_proj, g_out):
    b, s, d = x.shape
    t = b * s
    gla_qk = d // 2
    dn_qkv = 3 * d
    o_q, o_k = 0, gla_qk
    o_v = 2 * gla_qk
    o_g = o_v + d
    o_lr = o_g + d
    o_dn = o_lr + GLA_LOWRANK
    o_z = o_dn + dn_qkv
    o_a = o_z + d
    o_b = o_a + DN_HEADS
    o_ga = o_b + DN_HEADS
    o_gb = o_ga + d
    assert (o_q, o_k, o_v) == (0, gla_qk, 2 * gla_qk)
    cs = lambda lo, n: w_in[:, lo:lo + n]
    w_big = jnp.concatenate(
        [cs(o_q, o_lr), cs(o_dn, o_a - o_dn), cs(o_ga, 2 * d)], axis=1).astype(BF16)
    w_small = jnp.concatenate(
        [cs(o_lr, GLA_LOWRANK), cs(o_a, DN_HEADS), cs(o_b, DN_HEADS),
         jnp.zeros((d, LANES - SM_GL), w_in.dtype)], axis=1).astype(BF16)
    n_g, n_dn, n_z = o_g, o_lr, o_lr + dn_qkv
    n_ga = n_z + d
    n_gb = n_ga + d
    segments = ((0, ACT_NONE), (n_g, ACT_SILU), (n_dn, ACT_NONE), (n_z, ACT_SILU), (n_ga, ACT_SIGMOID))
    qkv_off = n_dn

    x2d = x.reshape(t, d)
    z2d, zs2d = _in_proj(x2d, g_mix.reshape(1, d), w_big, w_small, segments)
    z3 = z2d.reshape(b, s, -1)
    zs3 = zs2d.reshape(b, s, LANES)

    pad = lambda v: jnp.zeros((1, LANES), F32).at[0, SM_A:SM_B].set(v.astype(F32))
    g3, gt3 = _gates(zs3, pad(-jnp.exp(dn_a_log.astype(F32))), pad(dn_dt_bias))

    w2p = jnp.zeros((LANES, gla_qk), F32).at[:GLA_LOWRANK].set(gla_w2).astype(BF16)
    o_gla = _gla(z3, zs3, w2p, gla_b.reshape(1, -1).astype(F32), gla_norm.reshape(1, -1).astype(F32), d)
    o_dn = _deltanet(z3, dn_conv.astype(F32), g3, gt3, dn_norm.reshape(1, -1).astype(F32), d, qkv_off)

    x1, h2 = _merge(o_gla.reshape(t, d), o_dn.reshape(t, d), z2d, (n_g, n_z, n_ga, n_gb), x2d,
                    w_out.astype(BF16), g_mlp.reshape(1, d))
    x2 = _mlp(h2, w_up.astype(BF16), w_down.astype(BF16), x1)
    out = _ple(x2, p_i.reshape(t, -1), w_ple_gate.astype(BF16), w_ple_proj.astype(BF16),
               g_ple.reshape(1, d), g_out.reshape(1, d))
    return out.reshape(b, s, d)


def kernel(x, p, g_mix, w_in, gla_w2, gla_b, gla_norm, dn_conv, dn_a_log, dn_dt_bias, dn_norm,
           w_out, g_mlp, w_up, w_down, g_ple, w_ple_gate, w_ple_proj, g_final):
    depth = w_in.shape[0]
    assert depth == 1, "the final rms_norm is fused into the last layer's kernel"
    return _layer(x, p[0], g_mix[0], w_in[0], gla_w2[0], gla_b[0], gla_norm[0], dn_conv[0],
                  dn_a_log[0], dn_dt_bias[0], dn_norm[0], w_out[0], g_mlp[0], w_up[0], w_down[0],
                  g_ple[0], w_ple_gate[0], w_ple_proj[0], g_final)
```

```python
import functools

import jax
import jax.numpy as jnp
from jax import lax
from jax.experimental import pallas as pl
from jax.experimental.pallas import tpu as pltpu

F32 = jnp.float32
BF16 = jnp.bfloat16

EPS = 1e-6
CHUNK = 64
GLA_HEADS = 4
GLA_LOWRANK = 16
GLA_TAU = 16.0
DN_HEADS = 16
DN_CONV = 4
LANES = 128
TILE = 256
HALF = 128
DN_TILE = 2048
SCAN_UNROLL = 4
VMEM_LIMIT = 56 * 1024 * 1024

SM_LR = 0
SM_A = GLA_LOWRANK
SM_B = SM_A + DN_HEADS
SM_GL = SM_B + DN_HEADS


def _cparams(sem):
    return pltpu.CompilerParams(dimension_semantics=sem, vmem_limit_bytes=VMEM_LIMIT)


def _dot(a, b):
    return jnp.dot(a, b, preferred_element_type=F32)


def _dot_nt(a, b):
    return lax.dot_general(a, b, (((1,), (1,)), ((), ())), preferred_element_type=F32)


def _dot_tn(a, b):
    return lax.dot_general(a, b, (((0,), (0,)), ((), ())), preferred_element_type=F32)


def _sigmoid(x):
    return 1.0 / (1.0 + jnp.exp(-x))


def _silu(x):
    return x * _sigmoid(x)


def _log_sigmoid(x):
    return jnp.minimum(x, 0.0) - jnp.log(1.0 + jnp.exp(-jnp.abs(x)))


def _softplus(x):
    return jnp.maximum(x, 0.0) + jnp.log(1.0 + jnp.exp(-jnp.abs(x)))


def _rms(x, g):
    return x * lax.rsqrt(jnp.mean(x * x, axis=-1, keepdims=True) + EPS) * g


def _chunk_cumsum(x, row):
    n = x.shape[0]
    shift = 1
    while shift < CHUNK:
        x = x + jnp.where(row >= shift, pltpu.roll(x, shift, 0), 0.0)
        shift *= 2
    del n
    return x


def _chunk_cumsum_mxu(tri, x):
    hi = x.astype(BF16)
    rest = x - hi.astype(F32)
    mid = rest.astype(BF16)
    lo = (rest - mid.astype(F32)).astype(BF16)
    return _dot(tri, hi) + _dot(tri, mid) + _dot(tri, lo)


def _chunk_rev_cumsum(x, row):
    n = x.shape[0]
    shift = 1
    while shift < CHUNK:
        x = x + jnp.where(row < CHUNK - shift, pltpu.roll(x, n - shift, 0), 0.0)
        shift *= 2
    return x


ACT_NONE, ACT_SILU, ACT_SIGMOID = 0, 1, 2
_ACT_FN = {ACT_NONE: lambda v: v, ACT_SILU: _silu, ACT_SIGMOID: _sigmoid}


def _in_proj_body(tile_acts, x_ref, g_ref, wb_ref, ws_ref, z_ref, zs_ref, h_ref):
    j = pl.program_id(1)

    @pl.when(j == 0)
    def _():
        hb = _rms(x_ref[...], g_ref[...]).astype(BF16)
        h_ref[...] = hb
        zs_ref[...] = _dot(hb, ws_ref[...])

    for act, fn in _ACT_FN.items():
        tiles = [jt for jt, a in enumerate(tile_acts) if a == act]
        if not tiles:
            continue
        hit = functools.reduce(lambda p, q: p | q, [j == jt for jt in tiles])

        @pl.when(hit)
        def _(fn=fn):
            z_ref[...] = fn(_dot(h_ref[...], wb_ref[...])).astype(BF16)


def _in_proj(x2d, g_mix, w_big, w_small, segments, tm=1024, tn=2048):
    t, d = x2d.shape
    n = w_big.shape[1]
    assert all(lo % tn == 0 for lo, _ in segments) and n % tn == 0
    starts = [lo // tn for lo, _ in segments] + [n // tn]
    tile_acts = tuple(act for (_, act), a, b in zip(segments, starts, starts[1:]) for _ in range(b - a))
    body = functools.partial(_in_proj_body, tile_acts)
    return pl.pallas_call(
        body,
        grid=(t // tm, n // tn),
        in_specs=[
            pl.BlockSpec((tm, d), lambda i, j: (i, 0)),
            pl.BlockSpec((1, d), lambda i, j: (0, 0)),
            pl.BlockSpec((d, tn), lambda i, j: (0, j)),
            pl.BlockSpec((d, LANES), lambda i, j: (0, 0)),
        ],
        out_specs=[
            pl.BlockSpec((tm, tn), lambda i, j: (i, j)),
            pl.BlockSpec((tm, LANES), lambda i, j: (i, 0)),
        ],
        out_shape=[
            jax.ShapeDtypeStruct((t, n), BF16),
            jax.ShapeDtypeStruct((t, LANES), F32),
        ],
        scratch_shapes=[pltpu.VMEM((tm, d), BF16)],
        compiler_params=_cparams(("parallel", "arbitrary")),
        name="in_proj",
    )(x2d, g_mix, w_big, w_small)


def _gates_body(zs_ref, aneg_ref, dtb_ref, g_ref, gt_ref):
    zs = zs_ref[...]
    s = zs.shape[0]
    lane = lax.broadcasted_iota(jnp.int32, zs.shape, 1)
    row = lax.broadcasted_iota(jnp.int32, zs.shape, 0) % CHUNK
    is_a = (lane >= SM_A) & (lane < SM_B)
    is_b = (lane >= SM_B) & (lane < SM_GL)
    g = jnp.where(is_a, aneg_ref[...] * _softplus(zs + dtb_ref[...]), 0.0)
    gcum = _chunk_cumsum(g, row)
    gtot = gcum + _chunk_rev_cumsum(g, row) - g
    beta = _sigmoid(zs)
    out = jnp.where(is_a, gcum, jnp.where(is_b, beta, 0.0))
    out = out + pltpu.roll(jnp.where(is_a, gtot, 0.0), SM_GL - SM_A, 1)
    del s
    g_ref[...] = out
    gt_ref[...] = out.T


def _gates(zs3, a_neg, dt_bias):
    b, s, _ = zs3.shape
    return pl.pallas_call(
        _gates_body,
        grid=(b,),
        in_specs=[
            pl.BlockSpec((None, s, LANES), lambda i: (i, 0, 0)),
            pl.BlockSpec((1, LANES), lambda i: (0, 0)),
            pl.BlockSpec((1, LANES), lambda i: (0, 0)),
        ],
        out_specs=[
            pl.BlockSpec((None, s, LANES), lambda i: (i, 0, 0)),
            pl.BlockSpec((None, LANES, s), lambda i: (i, 0, 0)),
        ],
        out_shape=[
            jax.ShapeDtypeStruct((b, s, LANES), F32),
            jax.ShapeDtypeStruct((b, LANES, s), F32),
        ],
        compiler_params=_cparams(("parallel",)),
        name="dn_gates",
    )(zs3, a_neg, dt_bias)


def _gla_body(hb, scale, q_ref, k_ref, v_ref, lr_ref, w2_ref, b_ref, nrm_ref, sg_ref, ga_ref, o_ref, st_ref):
    s = q_ref.shape[0]
    dk = q_ref.shape[1] // hb
    dv = v_ref.shape[1] // hb
    per = TILE // CHUNK
    heads = range(hb)
    kcol = [pl.ds(h * dk, dk) for h in heads]
    vcol = [pl.ds(h * dv, dv) for h in heads]
    st_ref[...] = jnp.zeros_like(st_ref)
    row = lax.broadcasted_iota(jnp.int32, (TILE, dk), 0) % CHUNK
    ri = lax.broadcasted_iota(jnp.int32, (TILE, TILE), 0)
    ci = lax.broadcasted_iota(jnp.int32, (TILE, TILE), 1)
    causal = ((ri // CHUNK) == (ci // CHUNK)) & (ci <= ri)

    def tile(t, carry):
        base = pl.multiple_of(t * TILE, TILE)
        r = pl.ds(base, TILE)
        lrb = lr_ref[r, :].astype(BF16)
        q = [q_ref[r, kcol[h]].astype(F32) * scale for h in heads]
        k = [k_ref[r, kcol[h]].astype(F32) for h in heads]
        v = [v_ref[r, vcol[h]] for h in heads]
        pre = [_dot(lrb, w2_ref[:, kcol[h]]) + b_ref[:, kcol[h]] for h in heads]
        bc = [_chunk_cumsum(_log_sigmoid(x) * (1.0 / GLA_TAU), row) for x in pre]
        q_in = [(q[h] * jnp.exp(bc[h])).astype(BF16) for h in heads]
        k_in = [(k[h] * jnp.exp(-bc[h])).astype(BF16) for h in heads]
        a = [jnp.where(causal, _dot_nt(q_in[h], k_in[h]), 0.0).astype(BF16) for h in heads]
        o_intra = [_dot(a[h], v[h]) for h in heads]
        for c in range(per):
            lo, hi = c * CHUNK, (c + 1) * CHUNK
            b_last = [bc[h][hi - 1:hi, :] for h in heads]
            k_dec = [(k[h][lo:hi, :] * jnp.exp(b_last[h] - bc[h][lo:hi, :])).astype(BF16) for h in heads]
            st = [st_ref[h] for h in heads]
            o = [o_intra[h][lo:hi, :] + _dot_nt(q_in[h][lo:hi, :], st[h].astype(BF16)) for h in heads]
            for h in heads:
                st_ref[h] = st[h] * jnp.exp(b_last[h]) + _dot_tn(v[h][lo:hi, :], k_dec[h])
            rc = pl.ds(base + lo, CHUNK)
            for h in heads:
                gate = sg_ref[rc, vcol[h]].astype(F32) * ga_ref[rc, vcol[h]].astype(F32)
                o_ref[rc, vcol[h]] = (_rms(o[h], nrm_ref[...]) * gate).astype(o_ref.dtype)
        return carry

    lax.fori_loop(0, s // TILE, tile, 0)


def _gla(z3, zs3, w2p, gla_b, gla_norm, d_model, gate_cols, hb=2):
    b, s, _ = z3.shape
    dk = d_model // (2 * GLA_HEADS)
    dv = d_model // GLA_HEADS
    assert s % TILE == 0 and GLA_HEADS % hb == 0
    qk_cols = GLA_HEADS * dk
    kw, vw = hb * dk, hb * dv
    sg_col, ga_col = gate_cols
    assert sg_col % vw == 0 and ga_col % vw == 0
    body = functools.partial(_gla_body, hb, dk ** -0.5)
    return pl.pallas_call(
        body,
        grid=(b, GLA_HEADS // hb),
        in_specs=[
            pl.BlockSpec((None, s, kw), lambda i, h: (i, 0, h)),
            pl.BlockSpec((None, s, kw), lambda i, h: (i, 0, qk_cols // kw + h)),
            pl.BlockSpec((None, s, vw), lambda i, h: (i, 0, 2 * qk_cols // vw + h)),
            pl.BlockSpec((None, s, LANES), lambda i, h: (i, 0, 0)),
            pl.BlockSpec((LANES, kw), lambda i, h: (0, h)),
            pl.BlockSpec((1, kw), lambda i, h: (0, h)),
            pl.BlockSpec((1, dv), lambda i, h: (0, 0)),
            pl.BlockSpec((None, s, vw), lambda i, h: (i, 0, sg_col // vw + h)),
            pl.BlockSpec((None, s, vw), lambda i, h: (i, 0, ga_col // vw + h)),
        ],
        out_specs=pl.BlockSpec((None, s, vw), lambda i, h: (i, 0, h)),
        out_shape=jax.ShapeDtypeStruct((b, s, GLA_HEADS * dv), BF16),
        scratch_shapes=[pltpu.VMEM((hb, dv, dk), F32)],
        compiler_params=_cparams(("parallel", "parallel")),
        name="gla_mixer",
    )(z3, z3, z3, zs3, w2p, gla_b, gla_norm, z3, z3)


def _conv_silu(x_ref, w_ref, cols, base, n_rows, first, buf):
    pbase = pl.multiple_of(jnp.maximum(base - 8, 0), 8)
    buf[pl.ds(0, 8), :] = jnp.where(first, 0.0, x_ref[pl.ds(pbase, 8), cols].astype(F32))
    buf[pl.ds(8, n_rows), :] = x_ref[pl.ds(base, n_rows), cols].astype(F32)
    w = w_ref[:, cols]
    acc = None
    for d in range(DN_CONV):
        term = buf[pl.ds(8 - d, n_rows), :] * w[DN_CONV - 1 - d:DN_CONV - d, :]
        acc = term if acc is None else acc + term
    return _silu(acc)


def _l2n(t):
    return t * lax.rsqrt(jnp.sum(t * t, axis=-1, keepdims=True) + EPS)


def _lane_pick(x, idx):
    lane = lax.broadcasted_iota(jnp.int32, x.shape, 1)
    return jnp.sum(jnp.where(lane == idx, x, 0.0), axis=-1, keepdims=True)


def _bdot(a, b):
    return lax.dot_general(a, b, (((2,), (1,)), ((0,), (0,))), preferred_element_type=F32)


def _bdot_nt(a, b):
    return lax.dot_general(a, b, (((2,), (2,)), ((0,), (0,))), preferred_element_type=F32)


def _dn_body(hb, qscale, q_ref, k_ref, v_ref, cq_ref, ck_ref, cv_ref, g_ref, gt_ref, nrm_ref,
             o_ref, kp_s, np_s, qp_s, op_s, fl_s, st_s, cbuf):
    s = q_ref.shape[0]
    n_half = DN_TILE // HALF
    per_tile = DN_TILE // CHUNK
    head0 = pl.program_id(1) * hb

    ri = lax.broadcasted_iota(jnp.int32, (HALF, HALF), 0)
    ci = lax.broadcasted_iota(jnp.int32, (HALF, HALF), 1)
    same = (ri // CHUNK) == (ci // CHUNK)
    incl = same & (ci <= ri)
    strict = same & (ci < ri)
    n_sq = CHUNK.bit_length() - 2
    b3 = lambda x: x.reshape(n_half, HALF, x.shape[-1])

    n_tiles = s // DN_TILE
    n_units = hb * n_tiles

    def unit_coords(u):
        u = jnp.asarray(u, jnp.int32)
        i = lax.div(u, n_tiles)
        t = lax.rem(u, n_tiles)
        return i, t, pl.ds(pl.multiple_of(i * LANES, LANES), LANES)

    def prep(u):
        i, t, cols = unit_coords(u)
        head = head0 + i
        base = pl.multiple_of(t * DN_TILE, DN_TILE)
        rows = pl.ds(base, DN_TILE)
        first = t == 0
        q = _l2n(_conv_silu(q_ref, cq_ref, cols, base, DN_TILE, first, cbuf.at[0])) * qscale
        k = _l2n(_conv_silu(k_ref, ck_ref, cols, base, DN_TILE, first, cbuf.at[1]))
        v = _conv_silu(v_ref, cv_ref, cols, base, DN_TILE, first, cbuf.at[2])
        gs = g_ref[rows, :]
        gc = _lane_pick(gs, SM_A + head)
        bt = _lane_pick(gs, SM_B + head)
        gl = _lane_pick(gs, SM_GL + head)
        eg = jnp.exp(gc)
        kb = k * bt
        vb = v * bt
        kbe = kb * eg
        qd = q * eg
        kd = (k * jnp.exp(gl - gc)).astype(BF16)
        flb = jnp.broadcast_to(jnp.exp(gl), (DN_TILE, LANES))
        grow_t = gt_ref[pl.ds(SM_A + head, 1), rows]
        grow = jnp.stack([grow_t[:, j * HALF:(j + 1) * HALF] for j in range(n_half)], axis=0)
        dec = jnp.exp(jnp.where(incl, b3(gc) - grow, -jnp.inf))
        kbf = b3(k).astype(BF16)
        a = jnp.where(strict, _bdot_nt(b3(kb).astype(BF16), kbf) * dec, 0.0)
        attn = (_bdot_nt(b3(q).astype(BF16), kbf) * dec).astype(BF16)
        p = -a
        r = p
        for _ in range(n_sq):
            pb = p.astype(BF16)
            p = _bdot(pb, pb)
            r = r + p + _bdot(r.astype(BF16), p.astype(BF16))
        rhs = jnp.concatenate([b3(vb), b3(kbe)], axis=2)
        uw = (rhs + _bdot(r.astype(BF16), rhs.astype(BF16))).astype(BF16)
        aw = _bdot(attn, uw)
        op_s[rows, cols] = aw[:, :, :LANES].reshape(DN_TILE, LANES).astype(op_s.dtype)
        qp_s[rows, cols] = (qd - aw[:, :, LANES:].reshape(DN_TILE, LANES)).astype(qp_s.dtype)
        uw2 = uw.reshape(DN_TILE, 2 * LANES)
        for c in range(per_tile):
            c0 = c * CHUNK
            kn = _dot_tn(kd[c0:c0 + CHUNK, :], uw2[c0:c0 + CHUNK, :])
            cidx = t * per_tile + c
            np_s[i, cidx] = kn[:, :LANES].astype(np_s.dtype)
            kp_s[i, cidx] = kn[:, LANES:].astype(kp_s.dtype)
            fl_s[i, pl.ds(cidx, 1), :] = flb[c0:c0 + 1, :]

    def prep_step(u, carry):
        prep(u)
        return carry

    lax.fori_loop(0, n_units, prep_step, 0)

    st_s[...] = jnp.zeros_like(st_s)

    def scan(c, carry):
        r = pl.ds(pl.multiple_of(c * CHUNK, CHUNK), CHUNK)
        for i in range(hb):
            cols = pl.ds(i * LANES, LANES)
            st = st_s[i]
            stb = st.astype(BF16)
            o = _dot(qp_s[r, cols], stb) + op_s[r, cols].astype(F32)
            st_s[i] = st * fl_s[i, pl.ds(c, 1), :] - _dot(kp_s[i, c], stb) + np_s[i, c]
            o_ref[r, cols] = _rms(o, nrm_ref[...]).astype(o_ref.dtype)
        return carry

    lax.fori_loop(0, s // CHUNK, scan, 0, unroll=SCAN_UNROLL)


def _deltanet(z3, dn_conv, g3, gt3, dn_norm, d_model, qkv_off, hb=4):
    b, s, _ = z3.shape
    dh = d_model // DN_HEADS
    assert dh == LANES and s % DN_TILE == 0
    bw = hb * dh
    nblk = d_model // bw
    n_chunks = s // CHUNK
    body = functools.partial(_dn_body, hb, dh ** -0.5)
    z_spec = lambda part: pl.BlockSpec(
        (None, s, bw), lambda i, h: (i, 0, (qkv_off + part * d_model) // bw + h))
    c_spec = lambda part: pl.BlockSpec((DN_CONV, bw), lambda i, h: (0, part * nblk + h))
    return pl.pallas_call(
        body,
        grid=(b, nblk),
        in_specs=[
            z_spec(0), z_spec(1), z_spec(2),
            c_spec(0), c_spec(1), c_spec(2),
            pl.BlockSpec((None, s, LANES), lambda i, h: (i, 0, 0)),
            pl.BlockSpec((None, LANES, s), lambda i, h: (i, 0, 0)),
            pl.BlockSpec((1, dh), lambda i, h: (0, 0)),
        ],
        out_specs=pl.BlockSpec((None, s, bw), lambda i, h: (i, 0, h)),
        out_shape=jax.ShapeDtypeStruct((b, s, d_model), BF16),
        scratch_shapes=[
            pltpu.VMEM((hb, n_chunks, dh, dh), BF16),
            pltpu.VMEM((hb, n_chunks, dh, dh), F32),
            pltpu.VMEM((s, bw), BF16),
            pltpu.VMEM((s, bw), BF16),
            pltpu.VMEM((hb, n_chunks, LANES), F32),
            pltpu.VMEM((hb, dh, dh), F32),
            pltpu.VMEM((3, DN_TILE + 8, LANES), F32),
        ],
        compiler_params=_cparams(("parallel", "parallel")),
        name="deltanet_mixer",
    )(z3, z3, z3, dn_conv, dn_conv, dn_conv, g3, gt3, dn_norm)


def _merge_body(og_ref, od_ref, sz_ref, gb_ref, x_ref, w_ref, g_ref, x1_ref, h2_ref):
    f = lambda r: r[...].astype(F32)
    mixed = f(og_ref) + f(gb_ref) * (f(od_ref) * f(sz_ref))
    x1 = x_ref[...] + _dot(mixed.astype(BF16), w_ref[...])
    x1_ref[...] = x1
    h2_ref[...] = _rms(x1, g_ref[...]).astype(BF16)


def _merge(o_gla, o_dn, z2d, act_cols, x2d, w_out, g_mlp, tm=256):
    t, d = x2d.shape
    assert all(c % d == 0 for c in act_cols)
    row = lambda c: pl.BlockSpec((tm, d), lambda i, c=c: (i, c))
    sz, gb = (c // d for c in act_cols)
    return pl.pallas_call(
        _merge_body,
        grid=(t // tm,),
        in_specs=[
            row(0), row(0), row(sz), row(gb), row(0),
            pl.BlockSpec((d, d), lambda i: (0, 0)),
            pl.BlockSpec((1, d), lambda i: (0, 0)),
        ],
        out_specs=[row(0), row(0)],
        out_shape=[jax.ShapeDtypeStruct((t, d), F32), jax.ShapeDtypeStruct((t, d), BF16)],
        compiler_params=_cparams(("parallel",)),
        name="merge_out_proj",
    )(o_gla, o_dn, z2d, z2d, x2d, w_out, g_mlp)


def _mlp_body(h_ref, wu_ref, wd_ref, x_ref, o_ref):
    @pl.when(pl.program_id(1) == 0)
    def _():
        o_ref[...] = x_ref[...]

    mid = jnp.square(jnp.maximum(_dot(h_ref[...], wu_ref[...]), 0.0)).astype(BF16)
    o_ref[...] += _dot(mid, wd_ref[...])


def _mlp(h2, w_up, w_down, x1, tm=512, tf=1024):
    t, d = x1.shape
    ff = w_up.shape[1]
    return pl.pallas_call(
        _mlp_body,
        grid=(t // tm, ff // tf),
        in_specs=[
            pl.BlockSpec((tm, d), lambda i, j: (i, 0)),
            pl.BlockSpec((d, tf), lambda i, j: (0, j)),
            pl.BlockSpec((tf, d), lambda i, j: (j, 0)),
            pl.BlockSpec((tm, d), lambda i, j: (i, 0)),
        ],
        out_specs=pl.BlockSpec((tm, d), lambda i, j: (i, 0)),
        out_shape=jax.ShapeDtypeStruct((t, d), F32),
        compiler_params=_cparams(("parallel", "arbitrary")),
        name="relu2_mlp",
    )(h2, w_up, w_down, x1)


def _ple_body(x_ref, p_ref, wg_ref, wp_ref, gp_ref, gf_ref, o_ref):
    x2 = x_ref[...]
    h3 = _rms(x2, gp_ref[...]).astype(BF16)
    gate = _sigmoid(_dot(h3, wg_ref[...]))
    proj = _dot(p_ref[...].astype(BF16), wp_ref[...])
    o_ref[...] = _rms(x2 + gate * proj, gf_ref[...])


def _ple(x2, p2d, w_gate, w_proj, g_ple, g_final, tm=256):
    t, d = x2.shape
    pd = p2d.shape[1]
    return pl.pallas_call(
        _ple_body,
        grid=(t // tm,),
        in_specs=[
            pl.BlockSpec((tm, d), lambda i: (i, 0)),
            pl.BlockSpec((tm, pd), lambda i: (i, 0)),
            pl.BlockSpec((d, d), lambda i: (0, 0)),
            pl.BlockSpec((pd, d), lambda i: (0, 0)),
            pl.BlockSpec((1, d), lambda i: (0, 0)),
            pl.BlockSpec((1, d), lambda i: (0, 0)),
        ],
        out_specs=pl.BlockSpec((tm, d), lambda i: (i, 0)),
        out_shape=jax.ShapeDtypeStruct((t, d), F32),
        compiler_params=_cparams(("parallel",)),
        name="ple_final_norm",
    )(x2, p2d, w_gate, w_proj, g_ple, g_final)


def _layer(x, p_i, g_mix, w_in, gla_w2, gla_b, gla_norm, dn_conv, dn_a_log, dn_dt_bias, dn_norm,
           w_out, g_mlp, w_up, w_down, g_ple, w_ple_gate, w_ple_proj, g_out):
    b, s, d = x.shape
    t = b * s
    gla_qk = d // 2
    dn_qkv = 3 * d
    o_q, o_k = 0, gla_qk
    o_v = 2 * gla_qk
    o_g = o_v + d
    o_lr = o_g + d
    o_dn = o_lr + GLA_LOWRANK
    o_z = o_dn + dn_qkv
    o_a = o_z + d
    o_b = o_a + DN_HEADS
    o_ga = o_b + DN_HEADS
    o_gb = o_ga + d
    assert (o_q, o_k, o_v) == (0, gla_qk, 2 * gla_qk)
    cs = lambda lo, n: w_in[:, lo:lo + n]
    w_big = jnp.concatenate(
        [cs(o_q, o_lr), cs(o_dn, o_a - o_dn), cs(o_ga, 2 * d)], axis=1).astype(BF16)
    w_small = jnp.concatenate(
        [cs(o_lr, GLA_LOWRANK), cs(o_a, DN_HEADS), cs(o_b, DN_HEADS),
         jnp.zeros((d, LANES - SM_GL), w_in.dtype)], axis=1).astype(BF16)
    n_g, n_dn, n_z = o_g, o_lr, o_lr + dn_qkv
    n_ga = n_z + d
    n_gb = n_ga + d
    segments = ((0, ACT_NONE), (n_g, ACT_SILU), (n_dn, ACT_NONE), (n_z, ACT_SILU), (n_ga, ACT_SIGMOID))
    qkv_off = n_dn

    x2d = x.reshape(t, d)
    z2d, zs2d = _in_proj(x2d, g_mix.reshape(1, d), w_big, w_small, segments)
    z3 = z2d.reshape(b, s, -1)
    zs3 = zs2d.reshape(b, s, LANES)

    pad = lambda v: jnp.zeros((1, LANES), F32).at[0, SM_A:SM_B].set(v.astype(F32))
    g3, gt3 = _gates(zs3, pad(-jnp.exp(dn_a_log.astype(F32))), pad(dn_dt_bias))

    w2p = jnp.zeros((LANES, gla_qk), F32).at[:GLA_LOWRANK].set(gla_w2).astype(BF16)
    o_gla = _gla(z3, zs3, w2p, gla_b.reshape(1, -1).astype(F32), gla_norm.reshape(1, -1).astype(F32), d,
                 (n_g, n_ga))
    o_dn = _deltanet(z3, dn_conv.astype(F32), g3, gt3, dn_norm.reshape(1, -1).astype(F32), d, qkv_off)

    x1, h2 = _merge(o_gla.reshape(t, d), o_dn.reshape(t, d), z2d, (n_z, n_gb), x2d,
                    w_out.astype(BF16), g_mlp.reshape(1, d))
    x2 = _mlp(h2, w_up.astype(BF16), w_down.astype(BF16), x1)
    out = _ple(x2, p_i.reshape(t, -1), w_ple_gate.astype(BF16), w_ple_proj.astype(BF16),
               g_ple.reshape(1, d), g_out.reshape(1, d))
    return out.reshape(b, s, d)


def kernel(x, p, g_mix, w_in, gla_w2, gla_b, gla_norm, dn_conv, dn_a_log, dn_dt_bias, dn_norm,
           w_out, g_mlp, w_up, w_down, g_ple, w_ple_gate, w_ple_proj, g_final):
    depth = w_in.shape[0]
    assert depth == 1, "the final rms_norm is fused into the last layer's kernel"
    return _layer(x, p[0], g_mix[0], w_in[0], gla_w2[0], gla_b[0], gla_norm[0], dn_conv[0],
                  dn_a_log[0], dn_dt_bias[0], dn_norm[0], w_out[0], g_mlp[0], w_up[0], w_down[0],
                  g_ple[0], w_ple_gate[0], w_ple_proj[0], g_final)
```

```python
import functools

import jax
import jax.numpy as jnp
from jax import lax
from jax.experimental import pallas as pl
from jax.experimental.pallas import tpu as pltpu

F32 = jnp.float32
BF16 = jnp.bfloat16

EPS = 1e-6
CHUNK = 64
GLA_HEADS = 4
GLA_LOWRANK = 16
GLA_TAU = 16.0
DN_HEADS = 16
DN_CONV = 4
LANES = 128
TILE = 256
HALF = 128
DN_TILE = 2048
SCAN_UNROLL = 4
VMEM_LIMIT = 56 * 1024 * 1024

SM_LR = 0
SM_A = GLA_LOWRANK
SM_B = SM_A + DN_HEADS
SM_GL = SM_B + DN_HEADS


def _cparams(sem):
    return pltpu.CompilerParams(dimension_semantics=sem, vmem_limit_bytes=VMEM_LIMIT)


def _dot(a, b):
    return jnp.dot(a, b, preferred_element_type=F32)


def _dot_nt(a, b):
    return lax.dot_general(a, b, (((1,), (1,)), ((), ())), preferred_element_type=F32)


def _dot_tn(a, b):
    return lax.dot_general(a, b, (((0,), (0,)), ((), ())), preferred_element_type=F32)


def _sigmoid(x):
    return 1.0 / (1.0 + jnp.exp(-x))


def _silu(x):
    return x * _sigmoid(x)


def _log_sigmoid(x):
    return jnp.minimum(x, 0.0) - jnp.log(1.0 + jnp.exp(-jnp.abs(x)))


def _softplus(x):
    return jnp.maximum(x, 0.0) + jnp.log(1.0 + jnp.exp(-jnp.abs(x)))


def _rms(x, g):
    return x * lax.rsqrt(jnp.mean(x * x, axis=-1, keepdims=True) + EPS) * g


def _chunk_cumsum(x, row):
    n = x.shape[0]
    shift = 1
    while shift < CHUNK:
        x = x + jnp.where(row >= shift, pltpu.roll(x, shift, 0), 0.0)
        shift *= 2
    del n
    return x


def _chunk_cumsum_mxu(tri, x):
    hi = x.astype(BF16)
    rest = x - hi.astype(F32)
    mid = rest.astype(BF16)
    lo = (rest - mid.astype(F32)).astype(BF16)
    return _dot(tri, hi) + _dot(tri, mid) + _dot(tri, lo)


def _chunk_rev_cumsum(x, row):
    n = x.shape[0]
    shift = 1
    while shift < CHUNK:
        x = x + jnp.where(row < CHUNK - shift, pltpu.roll(x, n - shift, 0), 0.0)
        shift *= 2
    return x


ACT_NONE, ACT_SILU, ACT_SIGMOID = 0, 1, 2
_ACT_FN = {ACT_NONE: lambda v: v, ACT_SILU: _silu, ACT_SIGMOID: _sigmoid}


def _in_proj_body(tile_acts, x_ref, g_ref, wb_ref, ws_ref, z_ref, zs_ref, h_ref):
    j = pl.program_id(1)

    @pl.when(j == 0)
    def _():
        hb = _rms(x_ref[...], g_ref[...]).astype(BF16)
        h_ref[...] = hb
        zs_ref[...] = _dot(hb, ws_ref[...])

    for act, fn in _ACT_FN.items():
        tiles = [jt for jt, a in enumerate(tile_acts) if a == act]
        if not tiles:
            continue
        hit = functools.reduce(lambda p, q: p | q, [j == jt for jt in tiles])

        @pl.when(hit)
        def _(fn=fn):
            z_ref[...] = fn(_dot(h_ref[...], wb_ref[...])).astype(BF16)


def _in_proj(x2d, g_mix, w_big, w_small, segments, tm=1024, tn=2048):
    t, d = x2d.shape
    n = w_big.shape[1]
    assert all(lo % tn == 0 for lo, _ in segments) and n % tn == 0
    starts = [lo // tn for lo, _ in segments] + [n // tn]
    tile_acts = tuple(act for (_, act), a, b in zip(segments, starts, starts[1:]) for _ in range(b - a))
    body = functools.partial(_in_proj_body, tile_acts)
    return pl.pallas_call(
        body,
        grid=(t // tm, n // tn),
        in_specs=[
            pl.BlockSpec((tm, d), lambda i, j: (i, 0)),
            pl.BlockSpec((1, d), lambda i, j: (0, 0)),
            pl.BlockSpec((d, tn), lambda i, j: (0, j)),
            pl.BlockSpec((d, LANES), lambda i, j: (0, 0)),
        ],
        out_specs=[
            pl.BlockSpec((tm, tn), lambda i, j: (i, j)),
            pl.BlockSpec((tm, LANES), lambda i, j: (i, 0)),
        ],
        out_shape=[
            jax.ShapeDtypeStruct((t, n), BF16),
            jax.ShapeDtypeStruct((t, LANES), F32),
        ],
        scratch_shapes=[pltpu.VMEM((tm, d), BF16)],
        compiler_params=_cparams(("parallel", "arbitrary")),
        name="in_proj",
    )(x2d, g_mix, w_big, w_small)


def _gates_body(zs_ref, aneg_ref, dtb_ref, g_ref, gt_ref):
    zs = zs_ref[...]
    s = zs.shape[0]
    lane = lax.broadcasted_iota(jnp.int32, zs.shape, 1)
    row = lax.broadcasted_iota(jnp.int32, zs.shape, 0) % CHUNK
    is_a = (lane >= SM_A) & (lane < SM_B)
    is_b = (lane >= SM_B) & (lane < SM_GL)
    g = jnp.where(is_a, aneg_ref[...] * _softplus(zs + dtb_ref[...]), 0.0)
    gcum = _chunk_cumsum(g, row)
    gtot = gcum + _chunk_rev_cumsum(g, row) - g
    beta = _sigmoid(zs)
    out = jnp.where(is_a, gcum, jnp.where(is_b, beta, 0.0))
    out = out + pltpu.roll(jnp.where(is_a, gtot, 0.0), SM_GL - SM_A, 1)
    del s
    g_ref[...] = out
    gt_ref[...] = out.T


def _gates(zs3, a_neg, dt_bias):
    b, s, _ = zs3.shape
    return pl.pallas_call(
        _gates_body,
        grid=(b,),
        in_specs=[
            pl.BlockSpec((None, s, LANES), lambda i: (i, 0, 0)),
            pl.BlockSpec((1, LANES), lambda i: (0, 0)),
            pl.BlockSpec((1, LANES), lambda i: (0, 0)),
        ],
        out_specs=[
            pl.BlockSpec((None, s, LANES), lambda i: (i, 0, 0)),
            pl.BlockSpec((None, LANES, s), lambda i: (i, 0, 0)),
        ],
        out_shape=[
            jax.ShapeDtypeStruct((b, s, LANES), F32),
            jax.ShapeDtypeStruct((b, LANES, s), F32),
        ],
        compiler_params=_cparams(("parallel",)),
        name="dn_gates",
    )(zs3, a_neg, dt_bias)


def _gla_body(hb, scale, q_ref, k_ref, v_ref, lr_ref, w2_ref, b_ref, nrm_ref, sg_ref, ga_ref, o_ref, st_ref):
    s = q_ref.shape[0]
    dk = q_ref.shape[1] // hb
    dv = v_ref.shape[1] // hb
    per = TILE // CHUNK
    heads = range(hb)
    kcol = [pl.ds(h * dk, dk) for h in heads]
    vcol = [pl.ds(h * dv, dv) for h in heads]

    @pl.when(pl.program_id(1) == 0)
    def _():
        st_ref[...] = jnp.zeros_like(st_ref)

    row = lax.broadcasted_iota(jnp.int32, (TILE, dk), 0) % CHUNK
    ri = lax.broadcasted_iota(jnp.int32, (TILE, TILE), 0)
    ci = lax.broadcasted_iota(jnp.int32, (TILE, TILE), 1)
    causal = ((ri // CHUNK) == (ci // CHUNK)) & (ci <= ri)

    def tile(t, carry):
        base = pl.multiple_of(t * TILE, TILE)
        r = pl.ds(base, TILE)
        lrb = lr_ref[r, :].astype(BF16)
        q = [q_ref[r, kcol[h]].astype(F32) * scale for h in heads]
        k = [k_ref[r, kcol[h]].astype(F32) for h in heads]
        v = [v_ref[r, vcol[h]] for h in heads]
        pre = [_dot(lrb, w2_ref[:, kcol[h]]) + b_ref[:, kcol[h]] for h in heads]
        bc = [_chunk_cumsum(_log_sigmoid(x) * (1.0 / GLA_TAU), row) for x in pre]
        q_in = [(q[h] * jnp.exp(bc[h])).astype(BF16) for h in heads]
        k_in = [(k[h] * jnp.exp(-bc[h])).astype(BF16) for h in heads]
        a = [jnp.where(causal, _dot_nt(q_in[h], k_in[h]), 0.0).astype(BF16) for h in heads]
        o_intra = [_dot(a[h], v[h]) for h in heads]
        for c in range(per):
            lo, hi = c * CHUNK, (c + 1) * CHUNK
            b_last = [bc[h][hi - 1:hi, :] for h in heads]
            k_dec = [(k[h][lo:hi, :] * jnp.exp(b_last[h] - bc[h][lo:hi, :])).astype(BF16) for h in heads]
            st = [st_ref[h] for h in heads]
            o = [o_intra[h][lo:hi, :] + _dot_nt(q_in[h][lo:hi, :], st[h].astype(BF16)) for h in heads]
            for h in heads:
                st_ref[h] = st[h] * jnp.exp(b_last[h]) + _dot_tn(v[h][lo:hi, :], k_dec[h])
            rc = pl.ds(base + lo, CHUNK)
            for h in heads:
                gate = sg_ref[rc, vcol[h]].astype(F32) * ga_ref[rc, vcol[h]].astype(F32)
                o_ref[rc, vcol[h]] = (_rms(o[h], nrm_ref[...]) * gate).astype(o_ref.dtype)
        return carry

    lax.fori_loop(0, s // TILE, tile, 0)


def _gla(z3, zs3, w2p, gla_b, gla_norm, d_model, gate_cols, hb=GLA_HEADS, ts=512):
    b, s, _ = z3.shape
    dk = d_model // (2 * GLA_HEADS)
    dv = d_model // GLA_HEADS
    assert s % ts == 0 and ts % TILE == 0 and hb == GLA_HEADS
    qk_cols = GLA_HEADS * dk
    kw, vw = hb * dk, hb * dv
    sg_col, ga_col = gate_cols
    assert sg_col % vw == 0 and ga_col % vw == 0
    body = functools.partial(_gla_body, hb, dk ** -0.5)
    return pl.pallas_call(
        body,
        grid=(b, s // ts),
        in_specs=[
            pl.BlockSpec((None, ts, kw), lambda i, t: (i, t, 0)),
            pl.BlockSpec((None, ts, kw), lambda i, t: (i, t, qk_cols // kw)),
            pl.BlockSpec((None, ts, vw), lambda i, t: (i, t, 2 * qk_cols // vw)),
            pl.BlockSpec((None, ts, LANES), lambda i, t: (i, t, 0)),
            pl.BlockSpec((LANES, kw), lambda i, t: (0, 0)),
            pl.BlockSpec((1, kw), lambda i, t: (0, 0)),
            pl.BlockSpec((1, dv), lambda i, t: (0, 0)),
            pl.BlockSpec((None, ts, vw), lambda i, t: (i, t, sg_col // vw)),
            pl.BlockSpec((None, ts, vw), lambda i, t: (i, t, ga_col // vw)),
        ],
        out_specs=pl.BlockSpec((None, ts, vw), lambda i, t: (i, t, 0)),
        out_shape=jax.ShapeDtypeStruct((b, s, GLA_HEADS * dv), BF16),
        scratch_shapes=[pltpu.VMEM((hb, dv, dk), F32)],
        compiler_params=_cparams(("parallel", "arbitrary")),
        name="gla_mixer",
    )(z3, z3, z3, zs3, w2p, gla_b, gla_norm, z3, z3)


def _conv_silu(x_ref, w_ref, cols, base, n_rows, first, buf):
    pbase = pl.multiple_of(jnp.maximum(base - 8, 0), 8)
    buf[pl.ds(0, 8), :] = jnp.where(first, 0.0, x_ref[pl.ds(pbase, 8), cols].astype(F32))
    buf[pl.ds(8, n_rows), :] = x_ref[pl.ds(base, n_rows), cols].astype(F32)
    w = w_ref[:, cols]
    acc = None
    for d in range(DN_CONV):
        term = buf[pl.ds(8 - d, n_rows), :] * w[DN_CONV - 1 - d:DN_CONV - d, :]
        acc = term if acc is None else acc + term
    return _silu(acc)


def _l2n(t):
    return t * lax.rsqrt(jnp.sum(t * t, axis=-1, keepdims=True) + EPS)


def _lane_pick(x, idx):
    lane = lax.broadcasted_iota(jnp.int32, x.shape, 1)
    return jnp.sum(jnp.where(lane == idx, x, 0.0), axis=-1, keepdims=True)


def _bdot(a, b):
    return lax.dot_general(a, b, (((2,), (1,)), ((0,), (0,))), preferred_element_type=F32)


def _bdot_nt(a, b):
    return lax.dot_general(a, b, (((2,), (2,)), ((0,), (0,))), preferred_element_type=F32)


def _dn_body(hb, qscale, q_ref, k_ref, v_ref, cq_ref, ck_ref, cv_ref, g_ref, gt_ref, nrm_ref,
             o_ref, kp_s, np_s, qp_s, op_s, fl_s, st_s, cbuf):
    s = q_ref.shape[0]
    n_half = DN_TILE // HALF
    per_tile = DN_TILE // CHUNK
    head0 = pl.program_id(1) * hb

    ri = lax.broadcasted_iota(jnp.int32, (HALF, HALF), 0)
    ci = lax.broadcasted_iota(jnp.int32, (HALF, HALF), 1)
    same = (ri // CHUNK) == (ci // CHUNK)
    incl = same & (ci <= ri)
    strict = same & (ci < ri)
    n_sq = CHUNK.bit_length() - 2
    b3 = lambda x: x.reshape(n_half, HALF, x.shape[-1])

    n_tiles = s // DN_TILE
    n_units = hb * n_tiles

    def unit_coords(u):
        u = jnp.asarray(u, jnp.int32)
        i = lax.div(u, n_tiles)
        t = lax.rem(u, n_tiles)
        return i, t, pl.ds(pl.multiple_of(i * LANES, LANES), LANES)

    def prep(u):
        i, t, cols = unit_coords(u)
        head = head0 + i
        base = pl.multiple_of(t * DN_TILE, DN_TILE)
        rows = pl.ds(base, DN_TILE)
        first = t == 0
        q = _l2n(_conv_silu(q_ref, cq_ref, cols, base, DN_TILE, first, cbuf.at[0])) * qscale
        k = _l2n(_conv_silu(k_ref, ck_ref, cols, base, DN_TILE, first, cbuf.at[1]))
        v = _conv_silu(v_ref, cv_ref, cols, base, DN_TILE, first, cbuf.at[2])
        gs = g_ref[rows, :]
        gc = _lane_pick(gs, SM_A + head)
        bt = _lane_pick(gs, SM_B + head)
        gl = _lane_pick(gs, SM_GL + head)
        eg = jnp.exp(gc)
        kb = k * bt
        vb = v * bt
        kbe = kb * eg
        qd = q * eg
        kd = (k * jnp.exp(gl - gc)).astype(BF16)
        flb = jnp.broadcast_to(jnp.exp(gl), (DN_TILE, LANES))
        grow_t = gt_ref[pl.ds(SM_A + head, 1), rows]
        grow = jnp.stack([grow_t[:, j * HALF:(j + 1) * HALF] for j in range(n_half)], axis=0)
        dec = jnp.exp(jnp.where(incl, b3(gc) - grow, -jnp.inf))
        kbf = b3(k).astype(BF16)
        a = jnp.where(strict, _bdot_nt(b3(kb).astype(BF16), kbf) * dec, 0.0)
        attn = (_bdot_nt(b3(q).astype(BF16), kbf) * dec).astype(BF16)
        p = -a
        r = p
        for _ in range(n_sq):
            pb = p.astype(BF16)
            p = _bdot(pb, pb)
            r = r + p + _bdot(r.astype(BF16), p.astype(BF16))
        rhs = jnp.concatenate([b3(vb), b3(kbe)], axis=2)
        uw = (rhs + _bdot(r.astype(BF16), rhs.astype(BF16))).astype(BF16)
        aw = _bdot(attn, uw)
        op_s[rows, cols] = aw[:, :, :LANES].reshape(DN_TILE, LANES).astype(op_s.dtype)
        qp_s[rows, cols] = (qd - aw[:, :, LANES:].reshape(DN_TILE, LANES)).astype(qp_s.dtype)
        uw2 = uw.reshape(DN_TILE, 2 * LANES)
        for c in range(per_tile):
            c0 = c * CHUNK
            kn = _dot_tn(kd[c0:c0 + CHUNK, :], uw2[c0:c0 + CHUNK, :])
            cidx = t * per_tile + c
            np_s[i, cidx] = kn[:, :LANES].astype(np_s.dtype)
            kp_s[i, cidx] = kn[:, LANES:].astype(kp_s.dtype)
            fl_s[i, pl.ds(cidx, 1), :] = flb[c0:c0 + 1, :]

    def prep_step(u, carry):
        prep(u)
        return carry

    lax.fori_loop(0, n_units, prep_step, 0)

    st_s[...] = jnp.zeros_like(st_s)

    def scan(c, carry):
        r = pl.ds(pl.multiple_of(c * CHUNK, CHUNK), CHUNK)
        for i in range(hb):
            cols = pl.ds(i * LANES, LANES)
            st = st_s[i]
            stb = st.astype(BF16)
            o = _dot(qp_s[r, cols], stb) + op_s[r, cols].astype(F32)
            st_s[i] = st * fl_s[i, pl.ds(c, 1), :] - _dot(kp_s[i, c], stb) + np_s[i, c]
            o_ref[r, cols] = _rms(o, nrm_ref[...]).astype(o_ref.dtype)
        return carry

    lax.fori_loop(0, s // CHUNK, scan, 0, unroll=SCAN_UNROLL)


def _deltanet(z3, dn_conv, g3, gt3, dn_norm, d_model, qkv_off, hb=4):
    b, s, _ = z3.shape
    dh = d_model // DN_HEADS
    assert dh == LANES and s % DN_TILE == 0
    bw = hb * dh
    nblk = d_model // bw
    n_chunks = s // CHUNK
    body = functools.partial(_dn_body, hb, dh ** -0.5)
    z_spec = lambda part: pl.BlockSpec(
        (None, s, bw), lambda i, h: (i, 0, (qkv_off + part * d_model) // bw + h))
    c_spec = lambda part: pl.BlockSpec((DN_CONV, bw), lambda i, h: (0, part * nblk + h))
    return pl.pallas_call(
        body,
        grid=(b, nblk),
        in_specs=[
            z_spec(0), z_spec(1), z_spec(2),
            c_spec(0), c_spec(1), c_spec(2),
            pl.BlockSpec((None, s, LANES), lambda i, h: (i, 0, 0)),
            pl.BlockSpec((None, LANES, s), lambda i, h: (i, 0, 0)),
            pl.BlockSpec((1, dh), lambda i, h: (0, 0)),
        ],
        out_specs=pl.BlockSpec((None, s, bw), lambda i, h: (i, 0, h)),
        out_shape=jax.ShapeDtypeStruct((b, s, d_model), BF16),
        scratch_shapes=[
            pltpu.VMEM((hb, n_chunks, dh, dh), BF16),
            pltpu.VMEM((hb, n_chunks, dh, dh), F32),
            pltpu.VMEM((s, bw), BF16),
            pltpu.VMEM((s, bw), BF16),
            pltpu.VMEM((hb, n_chunks, LANES), F32),
            pltpu.VMEM((hb, dh, dh), F32),
            pltpu.VMEM((3, DN_TILE + 8, LANES), F32),
        ],
        compiler_params=_cparams(("parallel", "parallel")),
        name="deltanet_mixer",
    )(z3, z3, z3, dn_conv, dn_conv, dn_conv, g3, gt3, dn_norm)


def _merge_body(og_ref, od_ref, sz_ref, gb_ref, x_ref, w_ref, g_ref, x1_ref, h2_ref):
    f = lambda r: r[...].astype(F32)
    mixed = f(og_ref) + f(gb_ref) * (f(od_ref) * f(sz_ref))
    x1 = x_ref[...] + _dot(mixed.astype(BF16), w_ref[...])
    x1_ref[...] = x1
    h2_ref[...] = _rms(x1, g_ref[...]).astype(BF16)


def _merge(o_gla, o_dn, z2d, act_cols, x2d, w_out, g_mlp, tm=256):
    t, d = x2d.shape
    assert all(c % d == 0 for c in act_cols)
    row = lambda c: pl.BlockSpec((tm, d), lambda i, c=c: (i, c))
    sz, gb = (c // d for c in act_cols)
    return pl.pallas_call(
        _merge_body,
        grid=(t // tm,),
        in_specs=[
            row(0), row(0), row(sz), row(gb), row(0),
            pl.BlockSpec((d, d), lambda i: (0, 0)),
            pl.BlockSpec((1, d), lambda i: (0, 0)),
        ],
        out_specs=[row(0), row(0)],
        out_shape=[jax.ShapeDtypeStruct((t, d), F32), jax.ShapeDtypeStruct((t, d), BF16)],
        compiler_params=_cparams(("parallel",)),
        name="merge_out_proj",
    )(o_gla, o_dn, z2d, z2d, x2d, w_out, g_mlp)


def _mlp_body(h_ref, wu_ref, wd_ref, x_ref, o_ref):
    @pl.when(pl.program_id(1) == 0)
    def _():
        o_ref[...] = x_ref[...]

    mid = jnp.square(jnp.maximum(_dot(h_ref[...], wu_ref[...]), 0.0)).astype(BF16)
    o_ref[...] += _dot(mid, wd_ref[...])


def _mlp(h2, w_up, w_down, x1, tm=512, tf=1024):
    t, d = x1.shape
    ff = w_up.shape[1]
    return pl.pallas_call(
        _mlp_body,
        grid=(t // tm, ff // tf),
        in_specs=[
            pl.BlockSpec((tm, d), lambda i, j: (i, 0)),
            pl.BlockSpec((d, tf), lambda i, j: (0, j)),
            pl.BlockSpec((tf, d), lambda i, j: (j, 0)),
            pl.BlockSpec((tm, d), lambda i, j: (i, 0)),
        ],
        out_specs=pl.BlockSpec((tm, d), lambda i, j: (i, 0)),
        out_shape=jax.ShapeDtypeStruct((t, d), F32),
        compiler_params=_cparams(("parallel", "arbitrary")),
        name="relu2_mlp",
    )(h2, w_up, w_down, x1)


def _ple_body(x_ref, p_ref, wg_ref, wp_ref, gp_ref, gf_ref, o_ref):
    x2 = x_ref[...]
    h3 = _rms(x2, gp_ref[...]).astype(BF16)
    gate = _sigmoid(_dot(h3, wg_ref[...]))
    proj = _dot(p_ref[...].astype(BF16), wp_ref[...])
    o_ref[...] = _rms(x2 + gate * proj, gf_ref[...])


def _ple(x2, p2d, w_gate, w_proj, g_ple, g_final, tm=512):
    t, d = x2.shape
    pd = p2d.shape[1]
    return pl.pallas_call(
        _ple_body,
        grid=(t // tm,),
        in_specs=[
            pl.BlockSpec((tm, d), lambda i: (i, 0)),
            pl.BlockSpec((tm, pd), lambda i: (i, 0)),
            pl.BlockSpec((d, d), lambda i: (0, 0)),
            pl.BlockSpec((pd, d), lambda i: (0, 0)),
            pl.BlockSpec((1, d), lambda i: (0, 0)),
            pl.BlockSpec((1, d), lambda i: (0, 0)),
        ],
        out_specs=pl.BlockSpec((tm, d), lambda i: (i, 0)),
        out_shape=jax.ShapeDtypeStruct((t, d), F32),
        compiler_params=_cparams(("parallel",)),
        name="ple_final_norm",
    )(x2, p2d, w_gate, w_proj, g_ple, g_final)


def _layer(x, p_i, g_mix, w_in, gla_w2, gla_b, gla_norm, dn_conv, dn_a_log, dn_dt_bias, dn_norm,
           w_out, g_mlp, w_up, w_down, g_ple, w_ple_gate, w_ple_proj, g_out):
    b, s, d = x.shape
    t = b * s
    gla_qk = d // 2
    dn_qkv = 3 * d
    o_q, o_k = 0, gla_qk
    o_v = 2 * gla_qk
    o_g = o_v + d
    o_lr = o_g + d
    o_dn = o_lr + GLA_LOWRANK
    o_z = o_dn + dn_qkv
    o_a = o_z + d
    o_b = o_a + DN_HEADS
    o_ga = o_b + DN_HEADS
    o_gb = o_ga + d
    assert (o_q, o_k, o_v) == (0, gla_qk, 2 * gla_qk)
    w_in16 = w_in.astype(BF16)
    cs = lambda lo, n: w_in16[:, lo:lo + n]
    w_big = jnp.concatenate(
        [cs(o_q, o_lr), cs(o_dn, o_a - o_dn), cs(o_ga, 2 * d)], axis=1)
    w_small = jnp.concatenate(
        [cs(o_lr, GLA_LOWRANK), cs(o_a, DN_HEADS), cs(o_b, DN_HEADS),
         jnp.zeros((d, LANES - SM_GL), BF16)], axis=1)
    n_g, n_dn, n_z = o_g, o_lr, o_lr + dn_qkv
    n_ga = n_z + d
    n_gb = n_ga + d
    segments = ((0, ACT_NONE), (n_g, ACT_SILU), (n_dn, ACT_NONE), (n_z, ACT_SILU), (n_ga, ACT_SIGMOID))
    qkv_off = n_dn

    x2d = x.reshape(t, d)
    z2d, zs2d = _in_proj(x2d, g_mix.reshape(1, d), w_big, w_small, segments)
    z3 = z2d.reshape(b, s, -1)
    zs3 = zs2d.reshape(b, s, LANES)

    pad = lambda v: jnp.zeros((1, LANES), F32).at[0, SM_A:SM_B].set(v.astype(F32))
    g3, gt3 = _gates(zs3, pad(-jnp.exp(dn_a_log.astype(F32))), pad(dn_dt_bias))

    w2p = jnp.zeros((LANES, gla_qk), F32).at[:GLA_LOWRANK].set(gla_w2).astype(BF16)
    o_gla = _gla(z3, zs3, w2p, gla_b.reshape(1, -1).astype(F32), gla_norm.reshape(1, -1).astype(F32), d,
                 (n_g, n_ga))
    o_dn = _deltanet(z3, dn_conv.astype(F32), g3, gt3, dn_norm.reshape(1, -1).astype(F32), d, qkv_off)

    x1, h2 = _merge(o_gla.reshape(t, d), o_dn.reshape(t, d), z2d, (n_z, n_gb), x2d,
                    w_out.astype(BF16), g_mlp.reshape(1, d))
    x2 = _mlp(h2, w_up.astype(BF16), w_down.astype(BF16), x1)
    out = _ple(x2, p_i.reshape(t, -1), w_ple_gate.astype(BF16), w_ple_proj.astype(BF16),
               g_ple.reshape(1, d), g_out.reshape(1, d))
    return out.reshape(b, s, d)


def kernel(x, p, g_mix, w_in, gla_w2, gla_b, gla_norm, dn_conv, dn_a_log, dn_dt_bias, dn_norm,
           w_out, g_mlp, w_up, w_down, g_ple, w_ple_gate, w_ple_proj, g_final):
    depth = w_in.shape[0]
    assert depth == 1, "the final rms_norm is fused into the last layer's kernel"
    return _layer(x, p[0], g_mix[0], w_in[0], gla_w2[0], gla_b[0], gla_norm[0], dn_conv[0],
                  dn_a_log[0], dn_dt_bias[0], dn_norm[0], w_out[0], g_mlp[0], w_up[0], w_down[0],
                  g_ple[0], w_ple_gate[0], w_ple_proj[0], g_final)
```

```python
import functools

import jax
import jax.numpy as jnp
from jax import lax
from jax.experimental import pallas as pl
from jax.experimental.pallas import tpu as pltpu

F32 = jnp.float32
BF16 = jnp.bfloat16

EPS = 1e-6
CHUNK = 64
GLA_HEADS = 4
GLA_LOWRANK = 16
GLA_TAU = 16.0
DN_HEADS = 16
DN_CONV = 4
LANES = 128
TILE = 256
HALF = 128
DN_TILE = 2048
SCAN_UNROLL = 8
VMEM_LIMIT = 56 * 1024 * 1024

SM_LR = 0
SM_A = GLA_LOWRANK
SM_B = SM_A + DN_HEADS
SM_GL = SM_B + DN_HEADS


def _cparams(sem):
    return pltpu.CompilerParams(dimension_semantics=sem, vmem_limit_bytes=VMEM_LIMIT)


def _dot(a, b):
    return jnp.dot(a, b, preferred_element_type=F32)


def _dot_nt(a, b):
    return lax.dot_general(a, b, (((1,), (1,)), ((), ())), preferred_element_type=F32)


def _dot_tn(a, b):
    return lax.dot_general(a, b, (((0,), (0,)), ((), ())), preferred_element_type=F32)


def _sigmoid(x):
    return 0.5 * jnp.tanh(0.5 * x) + 0.5


def _silu(x):
    return x * _sigmoid(x)


def _log_sigmoid(x):
    return jnp.minimum(x, 0.0) - jnp.log(1.0 + jnp.exp(-jnp.abs(x)))


def _softplus(x):
    return jnp.maximum(x, 0.0) + jnp.log(1.0 + jnp.exp(-jnp.abs(x)))


def _rms(x, g):
    return x * lax.rsqrt(jnp.mean(x * x, axis=-1, keepdims=True) + EPS) * g


def _chunk_cumsum(x, row):
    n = x.shape[0]
    shift = 1
    while shift < CHUNK:
        x = x + jnp.where(row >= shift, pltpu.roll(x, shift, 0), 0.0)
        shift *= 2
    del n
    return x


def _chunk_cumsum_mxu(tri, x):
    hi = x.astype(BF16)
    rest = x - hi.astype(F32)
    mid = rest.astype(BF16)
    lo = (rest - mid.astype(F32)).astype(BF16)
    return _dot(tri, hi) + _dot(tri, mid) + _dot(tri, lo)


def _chunk_rev_cumsum(x, row):
    n = x.shape[0]
    shift = 1
    while shift < CHUNK:
        x = x + jnp.where(row < CHUNK - shift, pltpu.roll(x, n - shift, 0), 0.0)
        shift *= 2
    return x


ACT_NONE, ACT_SILU, ACT_SIGMOID = 0, 1, 2
_ACT_FN = {ACT_NONE: lambda v: v, ACT_SILU: _silu, ACT_SIGMOID: _sigmoid}


def _in_proj_body(tile_acts, x_ref, g_ref, wb_ref, ws_ref, z_ref, zs_ref, h_ref):
    j = pl.program_id(1)

    @pl.when(j == 0)
    def _():
        hb = _rms(x_ref[...], g_ref[...]).astype(BF16)
        h_ref[...] = hb
        zs_ref[...] = _dot(hb, ws_ref[...])

    for act, fn in _ACT_FN.items():
        tiles = [jt for jt, a in enumerate(tile_acts) if a == act]
        if not tiles:
            continue
        hit = functools.reduce(lambda p, q: p | q, [j == jt for jt in tiles])

        @pl.when(hit)
        def _(fn=fn):
            z_ref[...] = fn(_dot(h_ref[...], wb_ref[...])).astype(BF16)


def _in_proj(x2d, g_mix, w_big, w_small, segments, tm=1024, tn=2048):
    t, d = x2d.shape
    n = w_big.shape[1]
    assert all(lo % tn == 0 for lo, _ in segments) and n % tn == 0
    starts = [lo // tn for lo, _ in segments] + [n // tn]
    tile_acts = tuple(act for (_, act), a, b in zip(segments, starts, starts[1:]) for _ in range(b - a))
    body = functools.partial(_in_proj_body, tile_acts)
    return pl.pallas_call(
        body,
        grid=(t // tm, n // tn),
        in_specs=[
            pl.BlockSpec((tm, d), lambda i, j: (i, 0)),
            pl.BlockSpec((1, d), lambda i, j: (0, 0)),
            pl.BlockSpec((d, tn), lambda i, j: (0, j)),
            pl.BlockSpec((d, LANES), lambda i, j: (0, 0)),
        ],
        out_specs=[
            pl.BlockSpec((tm, tn), lambda i, j: (i, j)),
            pl.BlockSpec((tm, LANES), lambda i, j: (i, 0)),
        ],
        out_shape=[
            jax.ShapeDtypeStruct((t, n), BF16),
            jax.ShapeDtypeStruct((t, LANES), F32),
        ],
        scratch_shapes=[pltpu.VMEM((tm, d), BF16)],
        compiler_params=_cparams(("parallel", "arbitrary")),
        name="in_proj",
    )(x2d, g_mix, w_big, w_small)


def _gates_body(zs_ref, aneg_ref, dtb_ref, g_ref, gt_ref):
    zs = zs_ref[...]
    s = zs.shape[0]
    lane = lax.broadcasted_iota(jnp.int32, zs.shape, 1)
    row = lax.broadcasted_iota(jnp.int32, zs.shape, 0) % CHUNK
    is_a = (lane >= SM_A) & (lane < SM_B)
    is_b = (lane >= SM_B) & (lane < SM_GL)
    g = jnp.where(is_a, aneg_ref[...] * _softplus(zs + dtb_ref[...]), 0.0)
    gcum = _chunk_cumsum(g, row)
    gtot = gcum + _chunk_rev_cumsum(g, row) - g
    beta = _sigmoid(zs)
    out = jnp.where(is_a, gcum, jnp.where(is_b, beta, 0.0))
    out = out + pltpu.roll(jnp.where(is_a, gtot, 0.0), SM_GL - SM_A, 1)
    del s
    g_ref[...] = out
    gt_ref[...] = out.T


def _gates(zs3, a_neg, dt_bias):
    b, s, _ = zs3.shape
    return pl.pallas_call(
        _gates_body,
        grid=(b,),
        in_specs=[
            pl.BlockSpec((None, s, LANES), lambda i: (i, 0, 0)),
            pl.BlockSpec((1, LANES), lambda i: (0, 0)),
            pl.BlockSpec((1, LANES), lambda i: (0, 0)),
        ],
        out_specs=[
            pl.BlockSpec((None, s, LANES), lambda i: (i, 0, 0)),
            pl.BlockSpec((None, LANES, s), lambda i: (i, 0, 0)),
        ],
        out_shape=[
            jax.ShapeDtypeStruct((b, s, LANES), F32),
            jax.ShapeDtypeStruct((b, LANES, s), F32),
        ],
        compiler_params=_cparams(("parallel",)),
        name="dn_gates",
    )(zs3, a_neg, dt_bias)


def _gla_body(hb, scale, q_ref, k_ref, v_ref, lr_ref, w2_ref, b_ref, nrm_ref, sg_ref, ga_ref, o_ref, st_ref):
    s = q_ref.shape[0]
    dk = q_ref.shape[1] // hb
    dv = v_ref.shape[1] // hb
    per = TILE // CHUNK
    heads = range(hb)
    kcol = [pl.ds(h * dk, dk) for h in heads]
    vcol = [pl.ds(h * dv, dv) for h in heads]

    @pl.when(pl.program_id(1) == 0)
    def _():
        st_ref[...] = jnp.zeros_like(st_ref)

    row = lax.broadcasted_iota(jnp.int32, (TILE, dk), 0) % CHUNK
    ri = lax.broadcasted_iota(jnp.int32, (TILE, TILE), 0)
    ci = lax.broadcasted_iota(jnp.int32, (TILE, TILE), 1)
    causal = ((ri // CHUNK) == (ci // CHUNK)) & (ci <= ri)

    def tile(t, carry):
        base = pl.multiple_of(t * TILE, TILE)
        r = pl.ds(base, TILE)
        lrb = lr_ref[r, :].astype(BF16)
        q = [q_ref[r, kcol[h]].astype(F32) * scale for h in heads]
        k = [k_ref[r, kcol[h]].astype(F32) for h in heads]
        v = [v_ref[r, vcol[h]] for h in heads]
        pre = [_dot(lrb, w2_ref[:, kcol[h]]) + b_ref[:, kcol[h]] for h in heads]
        bc = [_chunk_cumsum(_log_sigmoid(x) * (1.0 / GLA_TAU), row) for x in pre]
        q_in = [(q[h] * jnp.exp(bc[h])).astype(BF16) for h in heads]
        k_in = [(k[h] * jnp.exp(-bc[h])).astype(BF16) for h in heads]
        a = [jnp.where(causal, _dot_nt(q_in[h], k_in[h]), 0.0).astype(BF16) for h in heads]
        o_intra = [_dot(a[h], v[h]) for h in heads]
        for c in range(per):
            lo, hi = c * CHUNK, (c + 1) * CHUNK
            b_last = [bc[h][hi - 1:hi, :] for h in heads]
            k_dec = [(k[h][lo:hi, :] * jnp.exp(b_last[h] - bc[h][lo:hi, :])).astype(BF16) for h in heads]
            st = [st_ref[h] for h in heads]
            o = [o_intra[h][lo:hi, :] + _dot_nt(q_in[h][lo:hi, :], st[h].astype(BF16)) for h in heads]
            for h in heads:
                st_ref[h] = st[h] * jnp.exp(b_last[h]) + _dot_tn(v[h][lo:hi, :], k_dec[h])
            rc = pl.ds(base + lo, CHUNK)
            for h in heads:
                gate = sg_ref[rc, vcol[h]].astype(F32) * ga_ref[rc, vcol[h]].astype(F32)
                o_ref[rc, vcol[h]] = (_rms(o[h], nrm_ref[...]) * gate).astype(o_ref.dtype)
        return carry

    lax.fori_loop(0, s // TILE, tile, 0)


def _gla(z3, zs3, w2p, gla_b, gla_norm, d_model, gate_cols, hb=GLA_HEADS, ts=512):
    b, s, _ = z3.shape
    dk = d_model // (2 * GLA_HEADS)
    dv = d_model // GLA_HEADS
    assert s % ts == 0 and ts % TILE == 0 and hb == GLA_HEADS
    qk_cols = GLA_HEADS * dk
    kw, vw = hb * dk, hb * dv
    sg_col, ga_col = gate_cols
    assert sg_col % vw == 0 and ga_col % vw == 0
    body = functools.partial(_gla_body, hb, dk ** -0.5)
    return pl.pallas_call(
        body,
        grid=(b, s // ts),
        in_specs=[
            pl.BlockSpec((None, ts, kw), lambda i, t: (i, t, 0)),
            pl.BlockSpec((None, ts, kw), lambda i, t: (i, t, qk_cols // kw)),
            pl.BlockSpec((None, ts, vw), lambda i, t: (i, t, 2 * qk_cols // vw)),
            pl.BlockSpec((None, ts, LANES), lambda i, t: (i, t, 0)),
            pl.BlockSpec((LANES, kw), lambda i, t: (0, 0)),
            pl.BlockSpec((1, kw), lambda i, t: (0, 0)),
            pl.BlockSpec((1, dv), lambda i, t: (0, 0)),
            pl.BlockSpec((None, ts, vw), lambda i, t: (i, t, sg_col // vw)),
            pl.BlockSpec((None, ts, vw), lambda i, t: (i, t, ga_col // vw)),
        ],
        out_specs=pl.BlockSpec((None, ts, vw), lambda i, t: (i, t, 0)),
        out_shape=jax.ShapeDtypeStruct((b, s, GLA_HEADS * dv), BF16),
        scratch_shapes=[pltpu.VMEM((hb, dv, dk), F32)],
        compiler_params=_cparams(("parallel", "arbitrary")),
        name="gla_mixer",
    )(z3, z3, z3, zs3, w2p, gla_b, gla_norm, z3, z3)


def _conv_silu(x_ref, w_ref, cols, base, n_rows, first, buf):
    pbase = pl.multiple_of(jnp.maximum(base - 8, 0), 8)
    buf[pl.ds(0, 8), :] = jnp.where(first, 0.0, x_ref[pl.ds(pbase, 8), cols].astype(F32))
    buf[pl.ds(8, n_rows), :] = x_ref[pl.ds(base, n_rows), cols].astype(F32)
    w = w_ref[:, cols]
    acc = None
    for d in range(DN_CONV):
        term = buf[pl.ds(8 - d, n_rows), :] * w[DN_CONV - 1 - d:DN_CONV - d, :]
        acc = term if acc is None else acc + term
    return _silu(acc)


def _l2n(t):
    return t * lax.rsqrt(jnp.sum(t * t, axis=-1, keepdims=True) + EPS)


def _lane_pick(x, idx):
    lane = lax.broadcasted_iota(jnp.int32, x.shape, 1)
    return jnp.sum(jnp.where(lane == idx, x, 0.0), axis=-1, keepdims=True)


def _bdot(a, b):
    return lax.dot_general(a, b, (((2,), (1,)), ((0,), (0,))), preferred_element_type=F32)


def _bdot_nt(a, b):
    return lax.dot_general(a, b, (((2,), (2,)), ((0,), (0,))), preferred_element_type=F32)


def _dn_body(hb, qscale, q_ref, k_ref, v_ref, cq_ref, ck_ref, cv_ref, g_ref, gt_ref, nrm_ref,
             o_ref, kp_s, np_s, qp_s, op_s, fl_s, st_s, cbuf):
    s = q_ref.shape[0]
    n_half = DN_TILE // HALF
    per_tile = DN_TILE // CHUNK
    head0 = pl.program_id(1) * hb

    ri = lax.broadcasted_iota(jnp.int32, (HALF, HALF), 0)
    ci = lax.broadcasted_iota(jnp.int32, (HALF, HALF), 1)
    same = (ri // CHUNK) == (ci // CHUNK)
    incl = same & (ci <= ri)
    strict = same & (ci < ri)
    n_sq = CHUNK.bit_length() - 2
    b3 = lambda x: x.reshape(n_half, HALF, x.shape[-1])

    n_tiles = s // DN_TILE
    n_units = hb * n_tiles

    def unit_coords(u):
        u = jnp.asarray(u, jnp.int32)
        i = lax.div(u, n_tiles)
        t = lax.rem(u, n_tiles)
        return i, t, pl.ds(pl.multiple_of(i * LANES, LANES), LANES)

    def prep(u):
        i, t, cols = unit_coords(u)
        head = head0 + i
        base = pl.multiple_of(t * DN_TILE, DN_TILE)
        rows = pl.ds(base, DN_TILE)
        first = t == 0
        q = _l2n(_conv_silu(q_ref, cq_ref, cols, base, DN_TILE, first, cbuf.at[0])) * qscale
        k = _l2n(_conv_silu(k_ref, ck_ref, cols, base, DN_TILE, first, cbuf.at[1]))
        v = _conv_silu(v_ref, cv_ref, cols, base, DN_TILE, first, cbuf.at[2])
        gs = g_ref[rows, :]
        gc = _lane_pick(gs, SM_A + head)
        bt = _lane_pick(gs, SM_B + head)
        gl = _lane_pick(gs, SM_GL + head)
        eg = jnp.exp(gc)
        kb = k * bt
        vb = v * bt
        kbe = kb * eg
        qd = q * eg
        kd = (k * jnp.exp(gl - gc)).astype(BF16)
        flb = jnp.broadcast_to(jnp.exp(gl), (DN_TILE, LANES))
        grow_t = gt_ref[pl.ds(SM_A + head, 1), rows]
        grow = jnp.stack([grow_t[:, j * HALF:(j + 1) * HALF] for j in range(n_half)], axis=0)
        dec = jnp.exp(jnp.where(incl, b3(gc) - grow, -jnp.inf))
        kbf = b3(k).astype(BF16)
        a = jnp.where(strict, _bdot_nt(b3(kb).astype(BF16), kbf) * dec, 0.0)
        attn = (_bdot_nt(b3(q).astype(BF16), kbf) * dec).astype(BF16)
        p = -a
        r = p
        for _ in range(n_sq):
            pb = p.astype(BF16)
            p = _bdot(pb, pb)
            r = r + p + _bdot(r.astype(BF16), p.astype(BF16))
        rhs = jnp.concatenate([b3(vb), b3(kbe)], axis=2)
        uw = (rhs + _bdot(r.astype(BF16), rhs.astype(BF16))).astype(BF16)
        aw = _bdot(attn, uw)
        op_s[rows, cols] = aw[:, :, :LANES].reshape(DN_TILE, LANES).astype(op_s.dtype)
        qp_s[rows, cols] = (qd - aw[:, :, LANES:].reshape(DN_TILE, LANES)).astype(qp_s.dtype)
        uw2 = uw.reshape(DN_TILE, 2 * LANES)
        for c in range(per_tile):
            c0 = c * CHUNK
            kn = _dot_tn(kd[c0:c0 + CHUNK, :], uw2[c0:c0 + CHUNK, :])
            cidx = t * per_tile + c
            np_s[i, cidx] = kn[:, :LANES].astype(np_s.dtype)
            kp_s[i, cidx] = kn[:, LANES:].astype(kp_s.dtype)
            fl_s[i, pl.ds(cidx, 1), :] = flb[c0:c0 + 1, :]

    def prep_step(u, carry):
        prep(u)
        return carry

    lax.fori_loop(0, n_units, prep_step, 0)

    st_s[...] = jnp.zeros_like(st_s)

    def scan(c, carry):
        r = pl.ds(pl.multiple_of(c * CHUNK, CHUNK), CHUNK)
        for i in range(hb):
            cols = pl.ds(i * LANES, LANES)
            st = st_s[i]
            stb = st.astype(BF16)
            o = _dot(qp_s[r, cols], stb) + op_s[r, cols].astype(F32)
            st_s[i] = st * fl_s[i, pl.ds(c, 1), :] - _dot(kp_s[i, c], stb) + np_s[i, c]
            o_ref[r, cols] = _rms(o, nrm_ref[...]).astype(o_ref.dtype)
        return carry

    lax.fori_loop(0, s // CHUNK, scan, 0, unroll=SCAN_UNROLL)


def _deltanet(z3, dn_conv, g3, gt3, dn_norm, d_model, qkv_off, hb=4):
    b, s, _ = z3.shape
    dh = d_model // DN_HEADS
    assert dh == LANES and s % DN_TILE == 0
    bw = hb * dh
    nblk = d_model // bw
    n_chunks = s // CHUNK
    body = functools.partial(_dn_body, hb, dh ** -0.5)
    z_spec = lambda part: pl.BlockSpec(
        (None, s, bw), lambda i, h: (i, 0, (qkv_off + part * d_model) // bw + h))
    c_spec = lambda part: pl.BlockSpec((DN_CONV, bw), lambda i, h: (0, part * nblk + h))
    return pl.pallas_call(
        body,
        grid=(b, nblk),
        in_specs=[
            z_spec(0), z_spec(1), z_spec(2),
            c_spec(0), c_spec(1), c_spec(2),
            pl.BlockSpec((None, s, LANES), lambda i, h: (i, 0, 0)),
            pl.BlockSpec((None, LANES, s), lambda i, h: (i, 0, 0)),
            pl.BlockSpec((1, dh), lambda i, h: (0, 0)),
        ],
        out_specs=pl.BlockSpec((None, s, bw), lambda i, h: (i, 0, h)),
        out_shape=jax.ShapeDtypeStruct((b, s, d_model), BF16),
        scratch_shapes=[
            pltpu.VMEM((hb, n_chunks, dh, dh), BF16),
            pltpu.VMEM((hb, n_chunks, dh, dh), F32),
            pltpu.VMEM((s, bw), BF16),
            pltpu.VMEM((s, bw), BF16),
            pltpu.VMEM((hb, n_chunks, LANES), F32),
            pltpu.VMEM((hb, dh, dh), F32),
            pltpu.VMEM((3, DN_TILE + 8, LANES), F32),
        ],
        compiler_params=_cparams(("parallel", "parallel")),
        name="deltanet_mixer",
    )(z3, z3, z3, dn_conv, dn_conv, dn_conv, g3, gt3, dn_norm)


def _merge_body(og_ref, od_ref, sz_ref, gb_ref, x_ref, w_ref, g_ref, x1_ref, h2_ref):
    f = lambda r: r[...].astype(F32)
    mixed = f(og_ref) + f(gb_ref) * (f(od_ref) * f(sz_ref))
    x1 = x_ref[...] + _dot(mixed.astype(BF16), w_ref[...])
    x1_ref[...] = x1
    h2_ref[...] = _rms(x1, g_ref[...]).astype(BF16)


def _merge(o_gla, o_dn, z2d, act_cols, x2d, w_out, g_mlp, tm=256):
    t, d = x2d.shape
    assert all(c % d == 0 for c in act_cols)
    row = lambda c: pl.BlockSpec((tm, d), lambda i, c=c: (i, c))
    sz, gb = (c // d for c in act_cols)
    return pl.pallas_call(
        _merge_body,
        grid=(t // tm,),
        in_specs=[
            row(0), row(0), row(sz), row(gb), row(0),
            pl.BlockSpec((d, d), lambda i: (0, 0)),
            pl.BlockSpec((1, d), lambda i: (0, 0)),
        ],
        out_specs=[row(0), row(0)],
        out_shape=[jax.ShapeDtypeStruct((t, d), F32), jax.ShapeDtypeStruct((t, d), BF16)],
        compiler_params=_cparams(("parallel",)),
        name="merge_out_proj",
    )(o_gla, o_dn, z2d, z2d, x2d, w_out, g_mlp)


def _mlp_body(h_ref, wu_ref, wd_ref, x_ref, o_ref):
    @pl.when(pl.program_id(1) == 0)
    def _():
        o_ref[...] = x_ref[...]

    mid = jnp.square(jnp.maximum(_dot(h_ref[...], wu_ref[...]), 0.0)).astype(BF16)
    o_ref[...] += _dot(mid, wd_ref[...])


def _mlp(h2, w_up, w_down, x1, tm=512, tf=1024):
    t, d = x1.shape
    ff = w_up.shape[1]
    return pl.pallas_call(
        _mlp_body,
        grid=(t // tm, ff // tf),
        in_specs=[
            pl.BlockSpec((tm, d), lambda i, j: (i, 0)),
            pl.BlockSpec((d, tf), lambda i, j: (0, j)),
            pl.BlockSpec((tf, d), lambda i, j: (j, 0)),
            pl.BlockSpec((tm, d), lambda i, j: (i, 0)),
        ],
        out_specs=pl.BlockSpec((tm, d), lambda i, j: (i, 0)),
        out_shape=jax.ShapeDtypeStruct((t, d), F32),
        compiler_params=_cparams(("parallel", "arbitrary")),
        name="relu2_mlp",
    )(h2, w_up, w_down, x1)


def _ple_body(x_ref, p_ref, wg_ref, wp_ref, gp_ref, gf_ref, o_ref):
    x2 = x_ref[...]
    h3 = _rms(x2, gp_ref[...]).astype(BF16)
    gate = _sigmoid(_dot(h3, wg_ref[...]))
    proj = _dot(p_ref[...].astype(BF16), wp_ref[...])
    o_ref[...] = _rms(x2 + gate * proj, gf_ref[...])


def _ple(x2, p2d, w_gate, w_proj, g_ple, g_final, tm=512):
    t, d = x2.shape
    pd = p2d.shape[1]
    return pl.pallas_call(
        _ple_body,
        grid=(t // tm,),
        in_specs=[
            pl.BlockSpec((tm, d), lambda i: (i, 0)),
            pl.BlockSpec((tm, pd), lambda i: (i, 0)),
            pl.BlockSpec((d, d), lambda i: (0, 0)),
            pl.BlockSpec((pd, d), lambda i: (0, 0)),
            pl.BlockSpec((1, d), lambda i: (0, 0)),
            pl.BlockSpec((1, d), lambda i: (0, 0)),
        ],
        out_specs=pl.BlockSpec((tm, d), lambda i: (i, 0)),
        out_shape=jax.ShapeDtypeStruct((t, d), F32),
        compiler_params=_cparams(("parallel",)),
        name="ple_final_norm",
    )(x2, p2d, w_gate, w_proj, g_ple, g_final)


def _layer(x, p_i, g_mix, w_in, gla_w2, gla_b, gla_norm, dn_conv, dn_a_log, dn_dt_bias, dn_norm,
           w_out, g_mlp, w_up, w_down, g_ple, w_ple_gate, w_ple_proj, g_out):
    b, s, d = x.shape
    t = b * s
    gla_qk = d // 2
    dn_qkv = 3 * d
    o_q, o_k = 0, gla_qk
    o_v = 2 * gla_qk
    o_g = o_v + d
    o_lr = o_g + d
    o_dn = o_lr + GLA_LOWRANK
    o_z = o_dn + dn_qkv
    o_a = o_z + d
    o_b = o_a + DN_HEADS
    o_ga = o_b + DN_HEADS
    o_gb = o_ga + d
    assert (o_q, o_k, o_v) == (0, gla_qk, 2 * gla_qk)
    w_in16 = w_in.astype(BF16)
    cs = lambda lo, n: w_in16[:, lo:lo + n]
    w_big = jnp.concatenate(
        [cs(o_q, o_lr), cs(o_dn, o_a - o_dn), cs(o_ga, 2 * d)], axis=1)
    w_small = jnp.concatenate(
        [cs(o_lr, GLA_LOWRANK), cs(o_a, DN_HEADS), cs(o_b, DN_HEADS),
         jnp.zeros((d, LANES - SM_GL), BF16)], axis=1)
    n_g, n_dn, n_z = o_g, o_lr, o_lr + dn_qkv
    n_ga = n_z + d
    n_gb = n_ga + d
    segments = ((0, ACT_NONE), (n_g, ACT_SILU), (n_dn, ACT_NONE), (n_z, ACT_SILU), (n_ga, ACT_SIGMOID))
    qkv_off = n_dn

    x2d = x.reshape(t, d)
    z2d, zs2d = _in_proj(x2d, g_mix.reshape(1, d), w_big, w_small, segments)
    z3 = z2d.reshape(b, s, -1)
    zs3 = zs2d.reshape(b, s, LANES)

    pad = lambda v: jnp.zeros((1, LANES), F32).at[0, SM_A:SM_B].set(v.astype(F32))
    g3, gt3 = _gates(zs3, pad(-jnp.exp(dn_a_log.astype(F32))), pad(dn_dt_bias))

    w2p = jnp.zeros((LANES, gla_qk), F32).at[:GLA_LOWRANK].set(gla_w2).astype(BF16)
    o_gla = _gla(z3, zs3, w2p, gla_b.reshape(1, -1).astype(F32), gla_norm.reshape(1, -1).astype(F32), d,
                 (n_g, n_ga))
    o_dn = _deltanet(z3, dn_conv.astype(F32), g3, gt3, dn_norm.reshape(1, -1).astype(F32), d, qkv_off)

    x1, h2 = _merge(o_gla.reshape(t, d), o_dn.reshape(t, d), z2d, (n_z, n_gb), x2d,
                    w_out.astype(BF16), g_mlp.reshape(1, d))
    x2 = _mlp(h2, w_up.astype(BF16), w_down.astype(BF16), x1)
    out = _ple(x2, p_i.reshape(t, -1), w_ple_gate.astype(BF16), w_ple_proj.astype(BF16),
               g_ple.reshape(1, d), g_out.reshape(1, d))
    return out.reshape(b, s, d)


def kernel(x, p, g_mix, w_in, gla_w2, gla_b, gla_norm, dn_conv, dn_a_log, dn_dt_bias, dn_norm,
           w_out, g_mlp, w_up, w_down, g_ple, w_ple_gate, w_ple_proj, g_final):
    depth = w_in.shape[0]
    assert depth == 1, "the final rms_norm is fused into the last layer's kernel"
    return _layer(x, p[0], g_mix[0], w_in[0], gla_w2[0], gla_b[0], gla_norm[0], dn_conv[0],
                  dn_a_log[0], dn_dt_bias[0], dn_norm[0], w_out[0], g_mlp[0], w_up[0], w_down[0],
                  g_ple[0], w_ple_gate[0], w_ple_proj[0], g_final)
```

```python
import functools

import jax
import jax.numpy as jnp
from jax import lax
from jax.experimental import pallas as pl
from jax.experimental.pallas import tpu as pltpu

F32 = jnp.float32
BF16 = jnp.bfloat16

EPS = 1e-6
CHUNK = 64
GLA_HEADS = 4
GLA_LOWRANK = 16
GLA_TAU = 16.0
DN_HEADS = 16
DN_CONV = 4
LANES = 128
TILE = 256
HALF = 128
DN_TILE = 2048
SCAN_UNROLL = 8
VMEM_LIMIT = 56 * 1024 * 1024

SM_A = GLA_LOWRANK
SM_B = SM_A + DN_HEADS
SM_GL = SM_B + DN_HEADS


def _cparams(sem):
    return pltpu.CompilerParams(dimension_semantics=sem, vmem_limit_bytes=VMEM_LIMIT)


def _dot(a, b):
    return jnp.dot(a, b, preferred_element_type=F32)


def _dot_nt(a, b):
    return lax.dot_general(a, b, (((1,), (1,)), ((), ())), preferred_element_type=F32)


def _dot_tn(a, b):
    return lax.dot_general(a, b, (((0,), (0,)), ((), ())), preferred_element_type=F32)


def _sigmoid(x):
    return 0.5 * jnp.tanh(0.5 * x) + 0.5


def _silu(x):
    return x * _sigmoid(x)


def _log_sigmoid(x):
    return jnp.minimum(x, 0.0) - jnp.log(1.0 + jnp.exp(-jnp.abs(x)))


def _softplus(x):
    return jnp.maximum(x, 0.0) + jnp.log(1.0 + jnp.exp(-jnp.abs(x)))


def _rms(x, g):
    return x * lax.rsqrt(jnp.mean(x * x, axis=-1, keepdims=True) + EPS) * g


def _chunk_cumsum(x, row):
    shift = 1
    while shift < CHUNK:
        x = x + jnp.where(row >= shift, pltpu.roll(x, shift, 0), 0.0)
        shift *= 2
    return x


def _chunk_rev_cumsum(x, row):
    n = x.shape[0]
    shift = 1
    while shift < CHUNK:
        x = x + jnp.where(row < CHUNK - shift, pltpu.roll(x, n - shift, 0), 0.0)
        shift *= 2
    return x


ACT_NONE, ACT_SILU, ACT_SIGMOID = 0, 1, 2
_ACT_FN = {ACT_NONE: lambda v: v, ACT_SILU: _silu, ACT_SIGMOID: _sigmoid}


def _in_proj_body(tile_acts, x_ref, g_ref, wb_ref, ws_ref, z_ref, zs_ref, h_ref):
    j = pl.program_id(1)

    @pl.when(j == 0)
    def _():
        hb = _rms(x_ref[...], g_ref[...]).astype(BF16)
        h_ref[...] = hb
        zs_ref[...] = _dot(hb, ws_ref[...])

    for act, fn in _ACT_FN.items():
        tiles = [jt for jt, a in enumerate(tile_acts) if a == act]
        if not tiles:
            continue
        hit = functools.reduce(lambda p, q: p | q, [j == jt for jt in tiles])

        @pl.when(hit)
        def _(fn=fn):
            z_ref[...] = fn(_dot(h_ref[...], wb_ref[...])).astype(BF16)


def _in_proj(x2d, g_mix, w_big, w_small, segments, tm=1024, tn=2048):
    t, d = x2d.shape
    n = w_big.shape[1]
    assert all(lo % tn == 0 for lo, _ in segments) and n % tn == 0
    starts = [lo // tn for lo, _ in segments] + [n // tn]
    tile_acts = tuple(act for (_, act), a, b in zip(segments, starts, starts[1:]) for _ in range(b - a))
    body = functools.partial(_in_proj_body, tile_acts)
    return pl.pallas_call(
        body,
        grid=(t // tm, n // tn),
        in_specs=[
            pl.BlockSpec((tm, d), lambda i, j: (i, 0)),
            pl.BlockSpec((1, d), lambda i, j: (0, 0)),
            pl.BlockSpec((d, tn), lambda i, j: (0, j)),
            pl.BlockSpec((d, LANES), lambda i, j: (0, 0)),
        ],
        out_specs=[
            pl.BlockSpec((tm, tn), lambda i, j: (i, j)),
            pl.BlockSpec((tm, LANES), lambda i, j: (i, 0)),
        ],
        out_shape=[
            jax.ShapeDtypeStruct((t, n), BF16),
            jax.ShapeDtypeStruct((t, LANES), F32),
        ],
        scratch_shapes=[pltpu.VMEM((tm, d), BF16)],
        compiler_params=_cparams(("parallel", "arbitrary")),
        name="in_proj",
    )(x2d, g_mix, w_big, w_small)


def _gates_body(zs_ref, aneg_ref, dtb_ref, g_ref, gt_ref):
    zs = zs_ref[...]
    lane = lax.broadcasted_iota(jnp.int32, zs.shape, 1)
    row = lax.broadcasted_iota(jnp.int32, zs.shape, 0) % CHUNK
    is_a = (lane >= SM_A) & (lane < SM_B)
    is_b = (lane >= SM_B) & (lane < SM_GL)
    g = jnp.where(is_a, aneg_ref[...] * _softplus(zs + dtb_ref[...]), 0.0)
    gcum = _chunk_cumsum(g, row)
    gtot = gcum + _chunk_rev_cumsum(g, row) - g
    beta = _sigmoid(zs)
    out = jnp.where(is_a, gcum, jnp.where(is_b, beta, 0.0))
    out = out + pltpu.roll(jnp.where(is_a, gtot, 0.0), SM_GL - SM_A, 1)
    g_ref[...] = out
    gt_ref[...] = out.T


def _gates(zs3, a_neg, dt_bias):
    b, s, _ = zs3.shape
    return pl.pallas_call(
        _gates_body,
        grid=(b,),
        in_specs=[
            pl.BlockSpec((None, s, LANES), lambda i: (i, 0, 0)),
            pl.BlockSpec((1, LANES), lambda i: (0, 0)),
            pl.BlockSpec((1, LANES), lambda i: (0, 0)),
        ],
        out_specs=[
            pl.BlockSpec((None, s, LANES), lambda i: (i, 0, 0)),
            pl.BlockSpec((None, LANES, s), lambda i: (i, 0, 0)),
        ],
        out_shape=[
            jax.ShapeDtypeStruct((b, s, LANES), F32),
            jax.ShapeDtypeStruct((b, LANES, s), F32),
        ],
        compiler_params=_cparams(("parallel",)),
        name="dn_gates",
    )(zs3, a_neg, dt_bias)


def _gla_body(hb, scale, q_ref, k_ref, v_ref, lr_ref, w2_ref, b_ref, nrm_ref, sg_ref, ga_ref, o_ref, st_ref):
    s = q_ref.shape[0]
    dk = q_ref.shape[1] // hb
    dv = v_ref.shape[1] // hb
    per = TILE // CHUNK
    heads = range(hb)
    kcol = [pl.ds(h * dk, dk) for h in heads]
    vcol = [pl.ds(h * dv, dv) for h in heads]

    @pl.when(pl.program_id(1) == 0)
    def _():
        st_ref[...] = jnp.zeros_like(st_ref)

    row = lax.broadcasted_iota(jnp.int32, (TILE, dk), 0) % CHUNK
    ri = lax.broadcasted_iota(jnp.int32, (TILE, TILE), 0)
    ci = lax.broadcasted_iota(jnp.int32, (TILE, TILE), 1)
    causal = ((ri // CHUNK) == (ci // CHUNK)) & (ci <= ri)

    def tile(t, carry):
        base = pl.multiple_of(t * TILE, TILE)
        r = pl.ds(base, TILE)
        lrb = lr_ref[r, :].astype(BF16)
        q = [q_ref[r, kcol[h]].astype(F32) * scale for h in heads]
        k = [k_ref[r, kcol[h]].astype(F32) for h in heads]
        v = [v_ref[r, vcol[h]] for h in heads]
        pre = [_dot(lrb, w2_ref[:, kcol[h]]) + b_ref[:, kcol[h]] for h in heads]
        bc = [_chunk_cumsum(_log_sigmoid(x) * (1.0 / GLA_TAU), row) for x in pre]
        q_in = [(q[h] * jnp.exp(bc[h])).astype(BF16) for h in heads]
        k_in = [(k[h] * jnp.exp(-bc[h])).astype(BF16) for h in heads]
        a = [jnp.where(causal, _dot_nt(q_in[h], k_in[h]), 0.0).astype(BF16) for h in heads]
        o_intra = [_dot(a[h], v[h]) for h in heads]
        for c in range(per):
            lo, hi = c * CHUNK, (c + 1) * CHUNK
            b_last = [bc[h][hi - 1:hi, :] for h in heads]
            k_dec = [(k[h][lo:hi, :] * jnp.exp(b_last[h] - bc[h][lo:hi, :])).astype(BF16) for h in heads]
            st = [st_ref[h] for h in heads]
            o = [o_intra[h][lo:hi, :] + _dot_nt(q_in[h][lo:hi, :], st[h].astype(BF16)) for h in heads]
            for h in heads:
                st_ref[h] = st[h] * jnp.exp(b_last[h]) + _dot_tn(v[h][lo:hi, :], k_dec[h])
            rc = pl.ds(base + lo, CHUNK)
            for h in heads:
                gate = sg_ref[rc, vcol[h]].astype(F32) * ga_ref[rc, vcol[h]].astype(F32)
                o_ref[rc, vcol[h]] = (_rms(o[h], nrm_ref[...]) * gate).astype(o_ref.dtype)
        return carry

    lax.fori_loop(0, s // TILE, tile, 0)


def _gla(z3, zs3, w2p, gla_b, gla_norm, d_model, gate_cols, ts=512):
    hb = GLA_HEADS
    b, s, _ = z3.shape
    dk = d_model // (2 * GLA_HEADS)
    dv = d_model // GLA_HEADS
    assert s % ts == 0 and ts % TILE == 0
    qk_cols = GLA_HEADS * dk
    kw, vw = hb * dk, hb * dv
    sg_col, ga_col = gate_cols
    assert sg_col % vw == 0 and ga_col % vw == 0
    body = functools.partial(_gla_body, hb, dk ** -0.5)
    return pl.pallas_call(
        body,
        grid=(b, s // ts),
        in_specs=[
            pl.BlockSpec((None, ts, kw), lambda i, t: (i, t, 0)),
            pl.BlockSpec((None, ts, kw), lambda i, t: (i, t, qk_cols // kw)),
            pl.BlockSpec((None, ts, vw), lambda i, t: (i, t, 2 * qk_cols // vw)),
            pl.BlockSpec((None, ts, LANES), lambda i, t: (i, t, 0)),
            pl.BlockSpec((LANES, kw), lambda i, t: (0, 0)),
            pl.BlockSpec((1, kw), lambda i, t: (0, 0)),
            pl.BlockSpec((1, dv), lambda i, t: (0, 0)),
            pl.BlockSpec((None, ts, vw), lambda i, t: (i, t, sg_col // vw)),
            pl.BlockSpec((None, ts, vw), lambda i, t: (i, t, ga_col // vw)),
        ],
        out_specs=pl.BlockSpec((None, ts, vw), lambda i, t: (i, t, 0)),
        out_shape=jax.ShapeDtypeStruct((b, s, GLA_HEADS * dv), BF16),
        scratch_shapes=[pltpu.VMEM((hb, dv, dk), F32)],
        compiler_params=_cparams(("parallel", "arbitrary")),
        name="gla_mixer",
    )(z3, z3, z3, zs3, w2p, gla_b, gla_norm, z3, z3)


def _conv_silu(x_ref, w_ref, cols, base, n_rows, first, buf):
    pbase = pl.multiple_of(jnp.maximum(base - 8, 0), 8)
    buf[pl.ds(0, 8), :] = jnp.where(first, 0.0, x_ref[pl.ds(pbase, 8), cols].astype(F32))
    buf[pl.ds(8, n_rows), :] = x_ref[pl.ds(base, n_rows), cols].astype(F32)
    w = w_ref[:, cols]
    acc = None
    for d in range(DN_CONV):
        term = buf[pl.ds(8 - d, n_rows), :] * w[DN_CONV - 1 - d:DN_CONV - d, :]
        acc = term if acc is None else acc + term
    return _silu(acc)


def _l2n(t):
    return t * lax.rsqrt(jnp.sum(t * t, axis=-1, keepdims=True) + EPS)


def _lane_pick(x, idx):
    lane = lax.broadcasted_iota(jnp.int32, x.shape, 1)
    return jnp.sum(jnp.where(lane == idx, x, 0.0), axis=-1, keepdims=True)


def _bdot(a, b):
    return lax.dot_general(a, b, (((2,), (1,)), ((0,), (0,))), preferred_element_type=F32)


def _bdot_nt(a, b):
    return lax.dot_general(a, b, (((2,), (2,)), ((0,), (0,))), preferred_element_type=F32)


def _dn_body(hb, qscale, q_ref, k_ref, v_ref, cq_ref, ck_ref, cv_ref, g_ref, gt_ref, nrm_ref,
             o_ref, kp_s, np_s, qp_s, op_s, fl_s, st_s, cbuf):
    s = q_ref.shape[0]
    n_half = DN_TILE // HALF
    per_tile = DN_TILE // CHUNK
    head0 = pl.program_id(1) * hb

    ri = lax.broadcasted_iota(jnp.int32, (HALF, HALF), 0)
    ci = lax.broadcasted_iota(jnp.int32, (HALF, HALF), 1)
    same = (ri // CHUNK) == (ci // CHUNK)
    incl = same & (ci <= ri)
    strict = same & (ci < ri)
    n_sq = CHUNK.bit_length() - 2
    b3 = lambda x: x.reshape(n_half, HALF, x.shape[-1])

    n_tiles = s // DN_TILE
    n_units = hb * n_tiles

    def unit_coords(u):
        u = jnp.asarray(u, jnp.int32)
        i = lax.div(u, n_tiles)
        t = lax.rem(u, n_tiles)
        return i, t, pl.ds(pl.multiple_of(i * LANES, LANES), LANES)

    def prep(u):
        i, t, cols = unit_coords(u)
        head = head0 + i
        base = pl.multiple_of(t * DN_TILE, DN_TILE)
        rows = pl.ds(base, DN_TILE)
        first = t == 0
        q = _l2n(_conv_silu(q_ref, cq_ref, cols, base, DN_TILE, first, cbuf.at[0])) * qscale
        k = _l2n(_conv_silu(k_ref, ck_ref, cols, base, DN_TILE, first, cbuf.at[1]))
        v = _conv_silu(v_ref, cv_ref, cols, base, DN_TILE, first, cbuf.at[2])
        gs = g_ref[rows, :]
        gc = _lane_pick(gs, SM_A + head)
        bt = _lane_pick(gs, SM_B + head)
        gl = _lane_pick(gs, SM_GL + head)
        eg = jnp.exp(gc)
        kb = k * bt
        vb = v * bt
        kbe = kb * eg
        qd = q * eg
        kd = (k * jnp.exp(gl - gc)).astype(BF16)
        flb = jnp.broadcast_to(jnp.exp(gl), (DN_TILE, LANES))
        grow_t = gt_ref[pl.ds(SM_A + head, 1), rows]
        grow = jnp.stack([grow_t[:, j * HALF:(j + 1) * HALF] for j in range(n_half)], axis=0)
        dec = jnp.exp(jnp.where(incl, b3(gc) - grow, -jnp.inf))
        kbf = b3(k).astype(BF16)
        a = jnp.where(strict, _bdot_nt(b3(kb).astype(BF16), kbf) * dec, 0.0)
        attn = (_bdot_nt(b3(q).astype(BF16), kbf) * dec).astype(BF16)
        p = -a
        r = p
        for _ in range(n_sq):
            pb = p.astype(BF16)
            p = _bdot(pb, pb)
            r = r + p + _bdot(r.astype(BF16), p.astype(BF16))
        rhs = jnp.concatenate([b3(vb), b3(kbe)], axis=2)
        uw = (rhs + _bdot(r.astype(BF16), rhs.astype(BF16))).astype(BF16)
        aw = _bdot(attn, uw)
        op_s[rows, cols] = aw[:, :, :LANES].reshape(DN_TILE, LANES).astype(op_s.dtype)
        qp_s[rows, cols] = (qd - aw[:, :, LANES:].reshape(DN_TILE, LANES)).astype(qp_s.dtype)
        uw2 = uw.reshape(DN_TILE, 2 * LANES)
        for c in range(per_tile):
            c0 = c * CHUNK
            kn = _dot_tn(kd[c0:c0 + CHUNK, :], uw2[c0:c0 + CHUNK, :])
            cidx = t * per_tile + c
            np_s[i, cidx] = kn[:, :LANES].astype(np_s.dtype)
            kp_s[i, cidx] = kn[:, LANES:].astype(kp_s.dtype)
            fl_s[i, pl.ds(cidx, 1), :] = flb[c0:c0 + 1, :]

    def prep_step(u, carry):
        prep(u)
        return carry

    lax.fori_loop(0, n_units, prep_step, 0)

    st_s[...] = jnp.zeros_like(st_s)

    def scan(c, carry):
        r = pl.ds(pl.multiple_of(c * CHUNK, CHUNK), CHUNK)
        for i in range(hb):
            cols = pl.ds(i * LANES, LANES)
            st = st_s[i]
            stb = st.astype(BF16)
            o = _dot(qp_s[r, cols], stb) + op_s[r, cols].astype(F32)
            st_s[i] = st * fl_s[i, pl.ds(c, 1), :] - _dot(kp_s[i, c], stb) + np_s[i, c]
            o_ref[r, cols] = _rms(o, nrm_ref[...]).astype(o_ref.dtype)
        return carry

    lax.fori_loop(0, s // CHUNK, scan, 0, unroll=SCAN_UNROLL)


def _deltanet(z3, dn_conv, g3, gt3, dn_norm, d_model, qkv_off, hb=4):
    b, s, _ = z3.shape
    dh = d_model // DN_HEADS
    assert dh == LANES and s % DN_TILE == 0
    bw = hb * dh
    nblk = d_model // bw
    n_chunks = s // CHUNK
    body = functools.partial(_dn_body, hb, dh ** -0.5)
    z_spec = lambda part: pl.BlockSpec(
        (None, s, bw), lambda i, h: (i, 0, (qkv_off + part * d_model) // bw + h))
    c_spec = lambda part: pl.BlockSpec((DN_CONV, bw), lambda i, h: (0, part * nblk + h))
    return pl.pallas_call(
        body,
        grid=(b, nblk),
        in_specs=[
            z_spec(0), z_spec(1), z_spec(2),
            c_spec(0), c_spec(1), c_spec(2),
            pl.BlockSpec((None, s, LANES), lambda i, h: (i, 0, 0)),
            pl.BlockSpec((None, LANES, s), lambda i, h: (i, 0, 0)),
            pl.BlockSpec((1, dh), lambda i, h: (0, 0)),
        ],
        out_specs=pl.BlockSpec((None, s, bw), lambda i, h: (i, 0, h)),
        out_shape=jax.ShapeDtypeStruct((b, s, d_model), BF16),
        scratch_shapes=[
            pltpu.VMEM((hb, n_chunks, dh, dh), BF16),
            pltpu.VMEM((hb, n_chunks, dh, dh), F32),
            pltpu.VMEM((s, bw), BF16),
            pltpu.VMEM((s, bw), BF16),
            pltpu.VMEM((hb, n_chunks, LANES), F32),
            pltpu.VMEM((hb, dh, dh), F32),
            pltpu.VMEM((3, DN_TILE + 8, LANES), F32),
        ],
        compiler_params=_cparams(("parallel", "parallel")),
        name="deltanet_mixer",
    )(z3, z3, z3, dn_conv, dn_conv, dn_conv, g3, gt3, dn_norm)


def _merge_body(og_ref, od_ref, sz_ref, gb_ref, x_ref, w_ref, g_ref, x1_ref, h2_ref):
    f = lambda r: r[...].astype(F32)
    mixed = f(og_ref) + f(gb_ref) * (f(od_ref) * f(sz_ref))
    x1 = x_ref[...] + _dot(mixed.astype(BF16), w_ref[...])
    x1_ref[...] = x1
    h2_ref[...] = _rms(x1, g_ref[...]).astype(BF16)


def _merge(o_gla, o_dn, z2d, act_cols, x2d, w_out, g_mlp, tm=512):
    t, d = x2d.shape
    assert all(c % d == 0 for c in act_cols)
    row = lambda c: pl.BlockSpec((tm, d), lambda i, c=c: (i, c))
    sz, gb = (c // d for c in act_cols)
    return pl.pallas_call(
        _merge_body,
        grid=(t // tm,),
        in_specs=[
            row(0), row(0), row(sz), row(gb), row(0),
            pl.BlockSpec((d, d), lambda i: (0, 0), pipeline_mode=pl.Buffered(1)),
            pl.BlockSpec((1, d), lambda i: (0, 0)),
        ],
        out_specs=[row(0), row(0)],
        out_shape=[jax.ShapeDtypeStruct((t, d), F32), jax.ShapeDtypeStruct((t, d), BF16)],
        compiler_params=_cparams(("parallel",)),
        name="merge_out_proj",
    )(o_gla, o_dn, z2d, z2d, x2d, w_out, g_mlp)


def _mlp_body(h_ref, wu_ref, wd_ref, x_ref, o_ref):
    @pl.when(pl.program_id(1) == 0)
    def _():
        o_ref[...] = x_ref[...]

    mid = jnp.square(jnp.maximum(_dot(h_ref[...], wu_ref[...]), 0.0)).astype(BF16)
    o_ref[...] += _dot(mid, wd_ref[...])


def _mlp(h2, w_up, w_down, x1, tm=512, tf=1024):
    t, d = x1.shape
    ff = w_up.shape[1]
    return pl.pallas_call(
        _mlp_body,
        grid=(t // tm, ff // tf),
        in_specs=[
            pl.BlockSpec((tm, d), lambda i, j: (i, 0)),
            pl.BlockSpec((d, tf), lambda i, j: (0, j)),
            pl.BlockSpec((tf, d), lambda i, j: (j, 0)),
            pl.BlockSpec((tm, d), lambda i, j: (i, 0)),
        ],
        out_specs=pl.BlockSpec((tm, d), lambda i, j: (i, 0)),
        out_shape=jax.ShapeDtypeStruct((t, d), F32),
        compiler_params=_cparams(("parallel", "arbitrary")),
        name="relu2_mlp",
    )(h2, w_up, w_down, x1)


def _ple_body(x_ref, p_ref, wg_ref, wp_ref, gp_ref, gf_ref, o_ref):
    x2 = x_ref[...]
    h3 = _rms(x2, gp_ref[...]).astype(BF16)
    gate = _sigmoid(_dot(h3, wg_ref[...]))
    proj = _dot(p_ref[...].astype(BF16), wp_ref[...])
    o_ref[...] = _rms(x2 + gate * proj, gf_ref[...])


def _ple(x2, p2d, w_gate, w_proj, g_ple, g_final, tm=512):
    t, d = x2.shape
    pd = p2d.shape[1]
    return pl.pallas_call(
        _ple_body,
        grid=(t // tm,),
        in_specs=[
            pl.BlockSpec((tm, d), lambda i: (i, 0)),
            pl.BlockSpec((tm, pd), lambda i: (i, 0)),
            pl.BlockSpec((d, d), lambda i: (0, 0)),
            pl.BlockSpec((pd, d), lambda i: (0, 0)),
            pl.BlockSpec((1, d), lambda i: (0, 0)),
            pl.BlockSpec((1, d), lambda i: (0, 0)),
        ],
        out_specs=pl.BlockSpec((tm, d), lambda i: (i, 0)),
        out_shape=jax.ShapeDtypeStruct((t, d), F32),
        compiler_params=_cparams(("parallel",)),
        name="ple_final_norm",
    )(x2, p2d, w_gate, w_proj, g_ple, g_final)


def _layer(x, p_i, g_mix, w_in, gla_w2, gla_b, gla_norm, dn_conv, dn_a_log, dn_dt_bias, dn_norm,
           w_out, g_mlp, w_up, w_down, g_ple, w_ple_gate, w_ple_proj, g_out):
    b, s, d = x.shape
    t = b * s
    gla_qk = d // 2
    dn_qkv = 3 * d
    o_q, o_k = 0, gla_qk
    o_v = 2 * gla_qk
    o_g = o_v + d
    o_lr = o_g + d
    o_dn = o_lr + GLA_LOWRANK
    o_z = o_dn + dn_qkv
    o_a = o_z + d
    o_b = o_a + DN_HEADS
    o_ga = o_b + DN_HEADS
    o_gb = o_ga + d
    assert (o_q, o_k, o_v) == (0, gla_qk, 2 * gla_qk)
    w_in16 = w_in.astype(BF16)
    cs = lambda lo, n: w_in16[:, lo:lo + n]
    w_big = jnp.concatenate(
        [cs(o_q, o_lr), cs(o_dn, o_a - o_dn), cs(o_ga, 2 * d)], axis=1)
    w_small = jnp.concatenate(
        [cs(o_lr, GLA_LOWRANK), cs(o_a, DN_HEADS), cs(o_b, DN_HEADS),
         jnp.zeros((d, LANES - SM_GL), BF16)], axis=1)
    n_g, n_dn, n_z = o_g, o_lr, o_lr + dn_qkv
    n_ga = n_z + d
    n_gb = n_ga + d
    segments = ((0, ACT_NONE), (n_g, ACT_SILU), (n_dn, ACT_NONE), (n_z, ACT_SILU), (n_ga, ACT_SIGMOID))
    qkv_off = n_dn

    x2d = x.reshape(t, d)
    z2d, zs2d = _in_proj(x2d, g_mix.reshape(1, d), w_big, w_small, segments)
    z3 = z2d.reshape(b, s, -1)
    zs3 = zs2d.reshape(b, s, LANES)

    pad = lambda v: jnp.zeros((1, LANES), F32).at[0, SM_A:SM_B].set(v.astype(F32))
    g3, gt3 = _gates(zs3, pad(-jnp.exp(dn_a_log.astype(F32))), pad(dn_dt_bias))

    w2p = jnp.zeros((LANES, gla_qk), F32).at[:GLA_LOWRANK].set(gla_w2).astype(BF16)
    o_gla = _gla(z3, zs3, w2p, gla_b.reshape(1, -1).astype(F32), gla_norm.reshape(1, -1).astype(F32), d,
                 (n_g, n_ga))
    o_dn = _deltanet(z3, dn_conv.astype(F32), g3, gt3, dn_norm.reshape(1, -1).astype(F32), d, qkv_off)

    x1, h2 = _merge(o_gla.reshape(t, d), o_dn.reshape(t, d), z2d, (n_z, n_gb), x2d,
                    w_out.astype(BF16), g_mlp.reshape(1, d))
    x2 = _mlp(h2, w_up.astype(BF16), w_down.astype(BF16), x1)
    out = _ple(x2, p_i.reshape(t, -1), w_ple_gate.astype(BF16), w_ple_proj.astype(BF16),
               g_ple.reshape(1, d), g_out.reshape(1, d))
    return out.reshape(b, s, d)


def kernel(x, p, g_mix, w_in, gla_w2, gla_b, gla_norm, dn_conv, dn_a_log, dn_dt_bias, dn_norm,
           w_out, g_mlp, w_up, w_down, g_ple, w_ple_gate, w_ple_proj, g_final):
    depth = w_in.shape[0]
    assert depth == 1, "the final rms_norm is fused into the last layer's kernel"
    return _layer(x, p[0], g_mix[0], w_in[0], gla_w2[0], gla_b[0], gla_norm[0], dn_conv[0],
                  dn_a_log[0], dn_dt_bias[0], dn_norm[0], w_out[0], g_mlp[0], w_up[0], w_down[0],
                  g_ple[0], w_ple_gate[0], w_ple_proj[0], g_final)
```

```python
import functools

import jax
import jax.numpy as jnp
from jax import lax
from jax.experimental import pallas as pl
from jax.experimental.pallas import tpu as pltpu

F32 = jnp.float32
BF16 = jnp.bfloat16

EPS = 1e-6
CHUNK = 64
GLA_HEADS = 4
GLA_LOWRANK = 16
GLA_TAU = 16.0
DN_HEADS = 16
DN_CONV = 4
LANES = 128
TILE = 256
DN_CHUNK = 128
HALF = DN_CHUNK
DN_TILE = 2048
SUB = 64
SCAN_UNROLL = 8
VMEM_LIMIT = 56 * 1024 * 1024

SM_A = GLA_LOWRANK
SM_B = SM_A + DN_HEADS
SM_GL = SM_B + DN_HEADS


def _cparams(sem):
    return pltpu.CompilerParams(dimension_semantics=sem, vmem_limit_bytes=VMEM_LIMIT)


def _dot(a, b):
    return jnp.dot(a, b, preferred_element_type=F32)


def _dot_nt(a, b):
    return lax.dot_general(a, b, (((1,), (1,)), ((), ())), preferred_element_type=F32)


def _dot_tn(a, b):
    return lax.dot_general(a, b, (((0,), (0,)), ((), ())), preferred_element_type=F32)


def _sigmoid(x):
    return 0.5 * jnp.tanh(0.5 * x) + 0.5


def _silu(x):
    return x * _sigmoid(x)


def _log_sigmoid(x):
    return jnp.minimum(x, 0.0) - jnp.log(1.0 + jnp.exp(-jnp.abs(x)))


def _softplus(x):
    return jnp.maximum(x, 0.0) + jnp.log(1.0 + jnp.exp(-jnp.abs(x)))


def _rms(x, g):
    return x * lax.rsqrt(jnp.mean(x * x, axis=-1, keepdims=True) + EPS) * g


def _chunk_cumsum(x, row, chunk=CHUNK):
    shift = 1
    while shift < chunk:
        x = x + jnp.where(row >= shift, pltpu.roll(x, shift, 0), 0.0)
        shift *= 2
    return x


def _chunk_rev_cumsum(x, row, chunk=CHUNK):
    n = x.shape[0]
    shift = 1
    while shift < chunk:
        x = x + jnp.where(row < chunk - shift, pltpu.roll(x, n - shift, 0), 0.0)
        shift *= 2
    return x


ACT_NONE, ACT_SILU, ACT_SIGMOID = 0, 1, 2
_ACT_FN = {ACT_NONE: lambda v: v, ACT_SILU: _silu, ACT_SIGMOID: _sigmoid}


def _in_proj_body(tile_acts, x_ref, g_ref, wb_ref, ws_ref, z_ref, zs_ref, h_ref):
    j = pl.program_id(1)

    @pl.when(j == 0)
    def _():
        hb = _rms(x_ref[...], g_ref[...]).astype(BF16)
        h_ref[...] = hb
        zs_ref[...] = _dot(hb, ws_ref[...])

    for act, fn in _ACT_FN.items():
        tiles = [jt for jt, a in enumerate(tile_acts) if a == act]
        if not tiles:
            continue
        hit = functools.reduce(lambda p, q: p | q, [j == jt for jt in tiles])

        @pl.when(hit)
        def _(fn=fn):
            z_ref[...] = fn(_dot(h_ref[...], wb_ref[...])).astype(BF16)


def _in_proj(x2d, g_mix, w_big, w_small, segments, tm=1024, tn=2048):
    t, d = x2d.shape
    n = w_big.shape[1]
    assert all(lo % tn == 0 for lo, _ in segments) and n % tn == 0
    starts = [lo // tn for lo, _ in segments] + [n // tn]
    tile_acts = tuple(act for (_, act), a, b in zip(segments, starts, starts[1:]) for _ in range(b - a))
    body = functools.partial(_in_proj_body, tile_acts)
    return pl.pallas_call(
        body,
        grid=(t // tm, n // tn),
        in_specs=[
            pl.BlockSpec((tm, d), lambda i, j: (i, 0)),
            pl.BlockSpec((1, d), lambda i, j: (0, 0)),
            pl.BlockSpec((d, tn), lambda i, j: (0, j)),
            pl.BlockSpec((d, LANES), lambda i, j: (0, 0)),
        ],
        out_specs=[
            pl.BlockSpec((tm, tn), lambda i, j: (i, j)),
            pl.BlockSpec((tm, LANES), lambda i, j: (i, 0)),
        ],
        out_shape=[
            jax.ShapeDtypeStruct((t, n), BF16),
            jax.ShapeDtypeStruct((t, LANES), F32),
        ],
        scratch_shapes=[pltpu.VMEM((tm, d), BF16)],
        compiler_params=_cparams(("parallel", "arbitrary")),
        name="in_proj",
    )(x2d, g_mix, w_big, w_small)


def _gates_body(zs_ref, aneg_ref, dtb_ref, g_ref, gt_ref):
    zs = zs_ref[...]
    lane = lax.broadcasted_iota(jnp.int32, zs.shape, 1)
    row = lax.broadcasted_iota(jnp.int32, zs.shape, 0) % DN_CHUNK
    is_a = (lane >= SM_A) & (lane < SM_B)
    is_b = (lane >= SM_B) & (lane < SM_GL)
    g = jnp.where(is_a, aneg_ref[...] * _softplus(zs + dtb_ref[...]), 0.0)
    gcum = _chunk_cumsum(g, row, DN_CHUNK)
    gtot = gcum + _chunk_rev_cumsum(g, row, DN_CHUNK) - g
    beta = _sigmoid(zs)
    out = jnp.where(is_a, gcum, jnp.where(is_b, beta, 0.0))
    out = out + pltpu.roll(jnp.where(is_a, gtot, 0.0), SM_GL - SM_A, 1)
    g_ref[...] = out
    gt_ref[...] = out.T


def _gates(zs3, a_neg, dt_bias):
    b, s, _ = zs3.shape
    return pl.pallas_call(
        _gates_body,
        grid=(b,),
        in_specs=[
            pl.BlockSpec((None, s, LANES), lambda i: (i, 0, 0)),
            pl.BlockSpec((1, LANES), lambda i: (0, 0)),
            pl.BlockSpec((1, LANES), lambda i: (0, 0)),
        ],
        out_specs=[
            pl.BlockSpec((None, s, LANES), lambda i: (i, 0, 0)),
            pl.BlockSpec((None, LANES, s), lambda i: (i, 0, 0)),
        ],
        out_shape=[
            jax.ShapeDtypeStruct((b, s, LANES), F32),
            jax.ShapeDtypeStruct((b, LANES, s), F32),
        ],
        compiler_params=_cparams(("parallel",)),
        name="dn_gates",
    )(zs3, a_neg, dt_bias)


def _gla_body(hb, scale, q_ref, k_ref, v_ref, lr_ref, w2_ref, b_ref, nrm_ref, sg_ref, ga_ref, o_ref, st_ref):
    s = q_ref.shape[0]
    dk = q_ref.shape[1] // hb
    dv = v_ref.shape[1] // hb
    per = TILE // CHUNK
    heads = range(hb)
    kcol = [pl.ds(h * dk, dk) for h in heads]
    vcol = [pl.ds(h * dv, dv) for h in heads]

    @pl.when(pl.program_id(1) == 0)
    def _():
        st_ref[...] = jnp.zeros_like(st_ref)

    row = lax.broadcasted_iota(jnp.int32, (TILE, dk), 0) % CHUNK
    ri = lax.broadcasted_iota(jnp.int32, (TILE, TILE), 0)
    ci = lax.broadcasted_iota(jnp.int32, (TILE, TILE), 1)
    causal = ((ri // CHUNK) == (ci // CHUNK)) & (ci <= ri)

    def tile(t, carry):
        base = pl.multiple_of(t * TILE, TILE)
        r = pl.ds(base, TILE)
        lrb = lr_ref[r, :].astype(BF16)
        q = [q_ref[r, kcol[h]].astype(F32) * scale for h in heads]
        k = [k_ref[r, kcol[h]].astype(F32) for h in heads]
        v = [v_ref[r, vcol[h]] for h in heads]
        pre = [_dot(lrb, w2_ref[:, kcol[h]]) + b_ref[:, kcol[h]] for h in heads]
        bc = [_chunk_cumsum(_log_sigmoid(x) * (1.0 / GLA_TAU), row) for x in pre]
        q_in = [(q[h] * jnp.exp(bc[h])).astype(BF16) for h in heads]
        k_in = [(k[h] * jnp.exp(-bc[h])).astype(BF16) for h in heads]
        a = [jnp.where(causal, _dot_nt(q_in[h], k_in[h]), 0.0).astype(BF16) for h in heads]
        o_intra = [_dot(a[h], v[h]) for h in heads]
        for c in range(per):
            lo, hi = c * CHUNK, (c + 1) * CHUNK
            b_last = [bc[h][hi - 1:hi, :] for h in heads]
            k_dec = [(k[h][lo:hi, :] * jnp.exp(b_last[h] - bc[h][lo:hi, :])).astype(BF16) for h in heads]
            st = [st_ref[h] for h in heads]
            o = [o_intra[h][lo:hi, :] + _dot_nt(q_in[h][lo:hi, :], st[h].astype(BF16)) for h in heads]
            for h in heads:
                st_ref[h] = st[h] * jnp.exp(b_last[h]) + _dot_tn(v[h][lo:hi, :], k_dec[h])
            rc = pl.ds(base + lo, CHUNK)
            for h in heads:
                gate = sg_ref[rc, vcol[h]].astype(F32) * ga_ref[rc, vcol[h]].astype(F32)
                o_ref[rc, vcol[h]] = (_rms(o[h], nrm_ref[...]) * gate).astype(o_ref.dtype)
        return carry

    lax.fori_loop(0, s // TILE, tile, 0)


def _gla(z3, zs3, w2p, gla_b, gla_norm, d_model, gate_cols, ts=512):
    hb = GLA_HEADS
    b, s, _ = z3.shape
    dk = d_model // (2 * GLA_HEADS)
    dv = d_model // GLA_HEADS
    assert s % ts == 0 and ts % TILE == 0
    qk_cols = GLA_HEADS * dk
    kw, vw = hb * dk, hb * dv
    sg_col, ga_col = gate_cols
    assert sg_col % vw == 0 and ga_col % vw == 0
    body = functools.partial(_gla_body, hb, dk ** -0.5)
    return pl.pallas_call(
        body,
        grid=(b, s // ts),
        in_specs=[
            pl.BlockSpec((None, ts, kw), lambda i, t: (i, t, 0)),
            pl.BlockSpec((None, ts, kw), lambda i, t: (i, t, qk_cols // kw)),
            pl.BlockSpec((None, ts, vw), lambda i, t: (i, t, 2 * qk_cols // vw)),
            pl.BlockSpec((None, ts, LANES), lambda i, t: (i, t, 0)),
            pl.BlockSpec((LANES, kw), lambda i, t: (0, 0)),
            pl.BlockSpec((1, kw), lambda i, t: (0, 0)),
            pl.BlockSpec((1, dv), lambda i, t: (0, 0)),
            pl.BlockSpec((None, ts, vw), lambda i, t: (i, t, sg_col // vw)),
            pl.BlockSpec((None, ts, vw), lambda i, t: (i, t, ga_col // vw)),
        ],
        out_specs=pl.BlockSpec((None, ts, vw), lambda i, t: (i, t, 0)),
        out_shape=jax.ShapeDtypeStruct((b, s, GLA_HEADS * dv), BF16),
        scratch_shapes=[pltpu.VMEM((hb, dv, dk), F32)],
        compiler_params=_cparams(("parallel", "arbitrary")),
        name="gla_mixer",
    )(z3, z3, z3, zs3, w2p, gla_b, gla_norm, z3, z3)


def _conv_silu(x_ref, w_ref, cols, base, n_rows, first, buf):
    pbase = pl.multiple_of(jnp.maximum(base - 8, 0), 8)
    buf[pl.ds(0, 8), :] = jnp.where(first, 0.0, x_ref[pl.ds(pbase, 8), cols].astype(F32))
    buf[pl.ds(8, n_rows), :] = x_ref[pl.ds(base, n_rows), cols].astype(F32)
    w = w_ref[:, cols]
    acc = None
    for d in range(DN_CONV):
        term = buf[pl.ds(8 - d, n_rows), :] * w[DN_CONV - 1 - d:DN_CONV - d, :]
        acc = term if acc is None else acc + term
    return _silu(acc)


def _l2n(t):
    return t * lax.rsqrt(jnp.sum(t * t, axis=-1, keepdims=True) + EPS)


def _lane_pick(x, idx):
    lane = lax.broadcasted_iota(jnp.int32, x.shape, 1)
    return jnp.sum(jnp.where(lane == idx, x, 0.0), axis=-1, keepdims=True)


def _bdot(a, b):
    return lax.dot_general(a, b, (((2,), (1,)), ((0,), (0,))), preferred_element_type=F32)


def _bdot_nt(a, b):
    return lax.dot_general(a, b, (((2,), (2,)), ((0,), (0,))), preferred_element_type=F32)


def _dn_body(hb, qscale, q_ref, k_ref, v_ref, cq_ref, ck_ref, cv_ref, g_ref, gt_ref, nrm_ref,
             o_ref, kp_s, np_s, qp_s, op_s, fl_s, st_s, cbuf):
    s = q_ref.shape[0]
    n_half = DN_TILE // HALF
    per_tile = DN_TILE // DN_CHUNK
    head0 = pl.program_id(1) * hb

    ri = lax.broadcasted_iota(jnp.int32, (HALF, HALF), 0)
    ci = lax.broadcasted_iota(jnp.int32, (HALF, HALF), 1)
    incl = ci <= ri
    sub = (ri // SUB) == (ci // SUB)
    strict_in = sub & (ci < ri)
    strict_off = (~sub) & (ci < ri)
    n_sq = SUB.bit_length() - 2
    b3 = lambda x: x.reshape(n_half, HALF, x.shape[-1])

    n_tiles = s // DN_TILE
    n_units = hb * n_tiles

    def unit_coords(u):
        u = jnp.asarray(u, jnp.int32)
        i = lax.div(u, n_tiles)
        t = lax.rem(u, n_tiles)
        return i, t, pl.ds(pl.multiple_of(i * LANES, LANES), LANES)

    def prep(u):
        i, t, cols = unit_coords(u)
        head = head0 + i
        base = pl.multiple_of(t * DN_TILE, DN_TILE)
        rows = pl.ds(base, DN_TILE)
        first = t == 0
        q = _l2n(_conv_silu(q_ref, cq_ref, cols, base, DN_TILE, first, cbuf.at[0])) * qscale
        k = _l2n(_conv_silu(k_ref, ck_ref, cols, base, DN_TILE, first, cbuf.at[1]))
        v = _conv_silu(v_ref, cv_ref, cols, base, DN_TILE, first, cbuf.at[2])
        gs = g_ref[rows, :]
        gc = _lane_pick(gs, SM_A + head)
        bt = _lane_pick(gs, SM_B + head)
        gl = _lane_pick(gs, SM_GL + head)
        eg = jnp.exp(gc)
        kb = k * bt
        vb = v * bt
        kbe = kb * eg
        qd = q * eg
        kd = (k * jnp.exp(gl - gc)).astype(BF16)
        flb = jnp.broadcast_to(jnp.exp(gl), (DN_TILE, LANES))
        grow_t = gt_ref[pl.ds(SM_A + head, 1), rows]
        grow = jnp.stack([grow_t[:, j * HALF:(j + 1) * HALF] for j in range(n_half)], axis=0)
        dec = jnp.exp(jnp.where(incl, b3(gc) - grow, -jnp.inf))
        kbf = b3(k).astype(BF16)
        a_full = _bdot_nt(b3(kb).astype(BF16), kbf) * dec
        a = jnp.where(strict_in, a_full, 0.0)
        attn = (_bdot_nt(b3(q).astype(BF16), kbf) * dec).astype(BF16)
        p = -a
        r = p
        for _ in range(n_sq):
            pb = p.astype(BF16)
            p = _bdot(pb, pb)
            r = r + p + _bdot(r.astype(BF16), p.astype(BF16))
        nb = jnp.where(strict_off, a_full, 0.0)
        rb = r.astype(BF16)
        tn = nb + _bdot(rb, nb.astype(BF16))
        r = r - (tn + _bdot(tn.astype(BF16), rb))
        rhs = jnp.concatenate([b3(vb), b3(kbe)], axis=2)
        uw = (rhs + _bdot(r.astype(BF16), rhs.astype(BF16))).astype(BF16)
        aw = _bdot(attn, uw)
        op_s[rows, cols] = aw[:, :, :LANES].reshape(DN_TILE, LANES).astype(op_s.dtype)
        qp_s[rows, cols] = (qd - aw[:, :, LANES:].reshape(DN_TILE, LANES)).astype(qp_s.dtype)
        uw2 = uw.reshape(DN_TILE, 2 * LANES)
        for c in range(per_tile):
            c0 = c * DN_CHUNK
            kn = _dot_tn(kd[c0:c0 + DN_CHUNK, :], uw2[c0:c0 + DN_CHUNK, :])
            cidx = t * per_tile + c
            np_s[i, cidx] = kn[:, :LANES].astype(np_s.dtype)
            kp_s[i, cidx] = kn[:, LANES:].astype(kp_s.dtype)
            fl_s[i, pl.ds(cidx, 1), :] = flb[c0:c0 + 1, :]

    def prep_step(u, carry):
        prep(u)
        return carry

    lax.fori_loop(0, n_units, prep_step, 0)

    st_s[...] = jnp.zeros_like(st_s)

    def scan(c, carry):
        r = pl.ds(pl.multiple_of(c * DN_CHUNK, DN_CHUNK), DN_CHUNK)
        for i in range(hb):
            cols = pl.ds(i * LANES, LANES)
            st = st_s[i]
            stb = st.astype(BF16)
            o = _dot(qp_s[r, cols], stb) + op_s[r, cols].astype(F32)
            st_s[i] = st * fl_s[i, pl.ds(c, 1), :] - _dot(kp_s[i, c], stb) + np_s[i, c]
            o_ref[r, cols] = _rms(o, nrm_ref[...]).astype(o_ref.dtype)
        return carry

    lax.fori_loop(0, s // DN_CHUNK, scan, 0, unroll=SCAN_UNROLL)


def _deltanet(z3, dn_conv, g3, gt3, dn_norm, d_model, qkv_off, hb=4):
    b, s, _ = z3.shape
    dh = d_model // DN_HEADS
    assert dh == LANES and s % DN_TILE == 0
    bw = hb * dh
    nblk = d_model // bw
    n_chunks = s // DN_CHUNK
    body = functools.partial(_dn_body, hb, dh ** -0.5)
    z_spec = lambda part: pl.BlockSpec(
        (None, s, bw), lambda i, h: (i, 0, (qkv_off + part * d_model) // bw + h))
    c_spec = lambda part: pl.BlockSpec((DN_CONV, bw), lambda i, h: (0, part * nblk + h))
    return pl.pallas_call(
        body,
        grid=(b, nblk),
        in_specs=[
            z_spec(0), z_spec(1), z_spec(2),
            c_spec(0), c_spec(1), c_spec(2),
            pl.BlockSpec((None, s, LANES), lambda i, h: (i, 0, 0)),
            pl.BlockSpec((None, LANES, s), lambda i, h: (i, 0, 0)),
            pl.BlockSpec((1, dh), lambda i, h: (0, 0)),
        ],
        out_specs=pl.BlockSpec((None, s, bw), lambda i, h: (i, 0, h)),
        out_shape=jax.ShapeDtypeStruct((b, s, d_model), BF16),
        scratch_shapes=[
            pltpu.VMEM((hb, n_chunks, dh, dh), BF16),
            pltpu.VMEM((hb, n_chunks, dh, dh), F32),
            pltpu.VMEM((s, bw), BF16),
            pltpu.VMEM((s, bw), BF16),
            pltpu.VMEM((hb, n_chunks, LANES), F32),
            pltpu.VMEM((hb, dh, dh), F32),
            pltpu.VMEM((3, DN_TILE + 8, LANES), F32),
        ],
        compiler_params=_cparams(("parallel", "parallel")),
        name="deltanet_mixer",
    )(z3, z3, z3, dn_conv, dn_conv, dn_conv, g3, gt3, dn_norm)


def _merge_body(og_ref, od_ref, sz_ref, gb_ref, x_ref, w_ref, g_ref, x1_ref, h2_ref):
    f = lambda r: r[...].astype(F32)
    mixed = f(og_ref) + f(gb_ref) * (f(od_ref) * f(sz_ref))
    x1 = x_ref[...] + _dot(mixed.astype(BF16), w_ref[...])
    x1_ref[...] = x1
    h2_ref[...] = _rms(x1, g_ref[...]).astype(BF16)


def _merge(o_gla, o_dn, z2d, act_cols, x2d, w_out, g_mlp, tm=512):
    t, d = x2d.shape
    assert all(c % d == 0 for c in act_cols)
    row = lambda c: pl.BlockSpec((tm, d), lambda i, c=c: (i, c))
    sz, gb = (c // d for c in act_cols)
    return pl.pallas_call(
        _merge_body,
        grid=(t // tm,),
        in_specs=[
            row(0), row(0), row(sz), row(gb), row(0),
            pl.BlockSpec((d, d), lambda i: (0, 0), pipeline_mode=pl.Buffered(1)),
            pl.BlockSpec((1, d), lambda i: (0, 0)),
        ],
        out_specs=[row(0), row(0)],
        out_shape=[jax.ShapeDtypeStruct((t, d), F32), jax.ShapeDtypeStruct((t, d), BF16)],
        compiler_params=_cparams(("parallel",)),
        name="merge_out_proj",
    )(o_gla, o_dn, z2d, z2d, x2d, w_out, g_mlp)


def _mlp_body(h_ref, wu_ref, wd_ref, x_ref, o_ref):
    @pl.when(pl.program_id(1) == 0)
    def _():
        o_ref[...] = x_ref[...]

    mid = jnp.square(jnp.maximum(_dot(h_ref[...], wu_ref[...]), 0.0)).astype(BF16)
    o_ref[...] += _dot(mid, wd_ref[...])


def _mlp(h2, w_up, w_down, x1, tm=512, tf=1024):
    t, d = x1.shape
    ff = w_up.shape[1]
    return pl.pallas_call(
        _mlp_body,
        grid=(t // tm, ff // tf),
        in_specs=[
            pl.BlockSpec((tm, d), lambda i, j: (i, 0)),
            pl.BlockSpec((d, tf), lambda i, j: (0, j)),
            pl.BlockSpec((tf, d), lambda i, j: (j, 0)),
            pl.BlockSpec((tm, d), lambda i, j: (i, 0)),
        ],
        out_specs=pl.BlockSpec((tm, d), lambda i, j: (i, 0)),
        out_shape=jax.ShapeDtypeStruct((t, d), F32),
        compiler_params=_cparams(("parallel", "arbitrary")),
        name="relu2_mlp",
    )(h2, w_up, w_down, x1)


def _ple_body(x_ref, p_ref, wg_ref, wp_ref, gp_ref, gf_ref, o_ref):
    x2 = x_ref[...]
    h3 = _rms(x2, gp_ref[...]).astype(BF16)
    gate = _sigmoid(_dot(h3, wg_ref[...]))
    proj = _dot(p_ref[...].astype(BF16), wp_ref[...])
    o_ref[...] = _rms(x2 + gate * proj, gf_ref[...])


def _ple(x2, p2d, w_gate, w_proj, g_ple, g_final, tm=512):
    t, d = x2.shape
    pd = p2d.shape[1]
    return pl.pallas_call(
        _ple_body,
        grid=(t // tm,),
        in_specs=[
            pl.BlockSpec((tm, d), lambda i: (i, 0)),
            pl.BlockSpec((tm, pd), lambda i: (i, 0)),
            pl.BlockSpec((d, d), lambda i: (0, 0)),
            pl.BlockSpec((pd, d), lambda i: (0, 0)),
            pl.BlockSpec((1, d), lambda i: (0, 0)),
            pl.BlockSpec((1, d), lambda i: (0, 0)),
        ],
        out_specs=pl.BlockSpec((tm, d), lambda i: (i, 0)),
        out_shape=jax.ShapeDtypeStruct((t, d), F32),
        compiler_params=_cparams(("parallel",)),
        name="ple_final_norm",
    )(x2, p2d, w_gate, w_proj, g_ple, g_final)


def _layer(x, p_i, g_mix, w_in, gla_w2, gla_b, gla_norm, dn_conv, dn_a_log, dn_dt_bias, dn_norm,
           w_out, g_mlp, w_up, w_down, g_ple, w_ple_gate, w_ple_proj, g_out):
    b, s, d = x.shape
    t = b * s
    gla_qk = d // 2
    dn_qkv = 3 * d
    o_q, o_k = 0, gla_qk
    o_v = 2 * gla_qk
    o_g = o_v + d
    o_lr = o_g + d
    o_dn = o_lr + GLA_LOWRANK
    o_z = o_dn + dn_qkv
    o_a = o_z + d
    o_b = o_a + DN_HEADS
    o_ga = o_b + DN_HEADS
    o_gb = o_ga + d
    assert (o_q, o_k, o_v) == (0, gla_qk, 2 * gla_qk)
    w_in16 = w_in.astype(BF16)
    cs = lambda lo, n: w_in16[:, lo:lo + n]
    w_big = jnp.concatenate(
        [cs(o_q, o_lr), cs(o_dn, o_a - o_dn), cs(o_ga, 2 * d)], axis=1)
    w_small = jnp.concatenate(
        [cs(o_lr, GLA_LOWRANK), cs(o_a, DN_HEADS), cs(o_b, DN_HEADS),
         jnp.zeros((d, LANES - SM_GL), BF16)], axis=1)
    n_g, n_dn, n_z = o_g, o_lr, o_lr + dn_qkv
    n_ga = n_z + d
    n_gb = n_ga + d
    segments = ((0, ACT_NONE), (n_g, ACT_SILU), (n_dn, ACT_NONE), (n_z, ACT_SILU), (n_ga, ACT_SIGMOID))
    qkv_off = n_dn

    x2d = x.reshape(t, d)
    z2d, zs2d = _in_proj(x2d, g_mix.reshape(1, d), w_big, w_small, segments)
    z3 = z2d.reshape(b, s, -1)
    zs3 = zs2d.reshape(b, s, LANES)

    pad = lambda v: jnp.zeros((1, LANES), F32).at[0, SM_A:SM_B].set(v.astype(F32))
    g3, gt3 = _gates(zs3, pad(-jnp.exp(dn_a_log.astype(F32))), pad(dn_dt_bias))

    w2p = jnp.zeros((LANES, gla_qk), F32).at[:GLA_LOWRANK].set(gla_w2).astype(BF16)
    o_gla = _gla(z3, zs3, w2p, gla_b.reshape(1, -1).astype(F32), gla_norm.reshape(1, -1).astype(F32), d,
                 (n_g, n_ga))
    o_dn = _deltanet(z3, dn_conv.astype(F32), g3, gt3, dn_norm.reshape(1, -1).astype(F32), d, qkv_off)

    x1, h2 = _merge(o_gla.reshape(t, d), o_dn.reshape(t, d), z2d, (n_z, n_gb), x2d,
                    w_out.astype(BF16), g_mlp.reshape(1, d))
    x2 = _mlp(h2, w_up.astype(BF16), w_down.astype(BF16), x1)
    out = _ple(x2, p_i.reshape(t, -1), w_ple_gate.astype(BF16), w_ple_proj.astype(BF16),
               g_ple.reshape(1, d), g_out.reshape(1, d))
    return out.reshape(b, s, d)


def kernel(x, p, g_mix, w_in, gla_w2, gla_b, gla_norm, dn_conv, dn_a_log, dn_dt_bias, dn_norm,
           w_out, g_mlp, w_up, w_down, g_ple, w_ple_gate, w_ple_proj, g_final):
    depth = w_in.shape[0]
    assert depth == 1, "the final rms_norm is fused into the last layer's kernel"
    return _layer(x, p[0], g_mix[0], w_in[0], gla_w2[0], gla_b[0], gla_norm[0], dn_conv[0],
                  dn_a_log[0], dn_dt_bias[0], dn_norm[0], w_out[0], g_mlp[0], w_up[0], w_down[0],
                  g_ple[0], w_ple_gate[0], w_ple_proj[0], g_final)
```

```python
import functools

import jax
import jax.numpy as jnp
from jax import lax
from jax.experimental import pallas as pl
from jax.experimental.pallas import tpu as pltpu

F32 = jnp.float32
BF16 = jnp.bfloat16

EPS = 1e-6
CHUNK = 64
GLA_HEADS = 4
GLA_LOWRANK = 16
GLA_TAU = 16.0
DN_HEADS = 16
DN_CONV = 4
LANES = 128
TILE = 256
DN_CHUNK = 128
HALF = DN_CHUNK
DN_TILE = 2048
SUB = 64
SCAN_UNROLL = 8
VMEM_LIMIT = 56 * 1024 * 1024

SM_A = GLA_LOWRANK
SM_B = SM_A + DN_HEADS
SM_GL = SM_B + DN_HEADS


def _cparams(sem):
    return pltpu.CompilerParams(dimension_semantics=sem, vmem_limit_bytes=VMEM_LIMIT)


def _dot(a, b):
    return jnp.dot(a, b, preferred_element_type=F32)


def _dot_nt(a, b):
    return lax.dot_general(a, b, (((1,), (1,)), ((), ())), preferred_element_type=F32)


def _dot_tn(a, b):
    return lax.dot_general(a, b, (((0,), (0,)), ((), ())), preferred_element_type=F32)


def _sigmoid(x):
    return 0.5 * jnp.tanh(0.5 * x) + 0.5


def _silu(x):
    return x * _sigmoid(x)


def _log_sigmoid(x):
    return jnp.minimum(x, 0.0) - jnp.log(1.0 + jnp.exp(-jnp.abs(x)))


def _softplus(x):
    return jnp.maximum(x, 0.0) + jnp.log(1.0 + jnp.exp(-jnp.abs(x)))


def _rms(x, g):
    return x * lax.rsqrt(jnp.mean(x * x, axis=-1, keepdims=True) + EPS) * g


def _chunk_cumsum(x, row, chunk=CHUNK):
    shift = 1
    while shift < chunk:
        x = x + jnp.where(row >= shift, pltpu.roll(x, shift, 0), 0.0)
        shift *= 2
    return x


def _chunk_rev_cumsum(x, row, chunk=CHUNK):
    n = x.shape[0]
    shift = 1
    while shift < chunk:
        x = x + jnp.where(row < chunk - shift, pltpu.roll(x, n - shift, 0), 0.0)
        shift *= 2
    return x


def _repack_body(shifts, w_ref, nxt_ref, o_ref):
    j = pl.program_id(1)
    tn = o_ref.shape[1]
    for shift in sorted(set(shifts)):
        hit = functools.reduce(lambda p, q: p | q, [j == jt for jt, sh in enumerate(shifts) if sh == shift])

        @pl.when(hit)
        def _(shift=shift):
            if shift == 0:
                o_ref[...] = w_ref[...]
            else:
                o_ref[...] = jnp.concatenate([w_ref[:, shift:], nxt_ref[:, :shift]], axis=1)


def _repack(w16, cuts, tr=256, tn=2048):
    d, n_src = w16.shape
    n_out = n_src - sum(w for _, w in cuts)
    assert n_out % tn == 0 and d % tr == 0
    shifts = tuple(sum(w for s, w in cuts if s <= jt * tn + sum(w2 for s2, w2 in cuts if s2 < s)) for jt in range(n_out // tn))
    assert max(shifts) < LANES
    assert n_out + shifts[-1] == n_src
    return pl.pallas_call(
        functools.partial(_repack_body, shifts),
        grid=(d // tr, n_out // tn),
        in_specs=[
            pl.BlockSpec((tr, tn), lambda i, j: (i, j)),
            pl.BlockSpec((tr, LANES), lambda i, j: (i, (j + 1) * (tn // LANES))),
        ],
        out_specs=pl.BlockSpec((tr, tn), lambda i, j: (i, j)),
        out_shape=jax.ShapeDtypeStruct((d, n_out), w16.dtype),
        compiler_params=_cparams(("parallel", "parallel")),
        name="w_in_repack",
    )(w16, w16)


ACT_NONE, ACT_SILU, ACT_SIGMOID = 0, 1, 2
_ACT_FN = {ACT_NONE: lambda v: v, ACT_SILU: _silu, ACT_SIGMOID: _sigmoid}


def _in_proj_body(tile_acts, x_ref, g_ref, wb_ref, ws_ref, z_ref, zs_ref, h_ref):
    j = pl.program_id(1)

    @pl.when(j == 0)
    def _():
        hb = _rms(x_ref[...], g_ref[...]).astype(BF16)
        h_ref[...] = hb
        zs_ref[...] = _dot(hb, ws_ref[...])

    for act, fn in _ACT_FN.items():
        tiles = [jt for jt, a in enumerate(tile_acts) if a == act]
        if not tiles:
            continue
        hit = functools.reduce(lambda p, q: p | q, [j == jt for jt in tiles])

        @pl.when(hit)
        def _(fn=fn):
            z_ref[...] = fn(_dot(h_ref[...], wb_ref[...])).astype(BF16)


def _in_proj(x2d, g_mix, w_big, w_small, segments, tm=1024, tn=2048):
    t, d = x2d.shape
    n = w_big.shape[1]
    assert all(lo % tn == 0 for lo, _ in segments) and n % tn == 0
    starts = [lo // tn for lo, _ in segments] + [n // tn]
    tile_acts = tuple(act for (_, act), a, b in zip(segments, starts, starts[1:]) for _ in range(b - a))
    body = functools.partial(_in_proj_body, tile_acts)
    return pl.pallas_call(
        body,
        grid=(t // tm, n // tn),
        in_specs=[
            pl.BlockSpec((tm, d), lambda i, j: (i, 0)),
            pl.BlockSpec((1, d), lambda i, j: (0, 0)),
            pl.BlockSpec((d, tn), lambda i, j: (0, j)),
            pl.BlockSpec((d, LANES), lambda i, j: (0, 0)),
        ],
        out_specs=[
            pl.BlockSpec((tm, tn), lambda i, j: (i, j)),
            pl.BlockSpec((tm, LANES), lambda i, j: (i, 0)),
        ],
        out_shape=[
            jax.ShapeDtypeStruct((t, n), BF16),
            jax.ShapeDtypeStruct((t, LANES), F32),
        ],
        scratch_shapes=[pltpu.VMEM((tm, d), BF16)],
        compiler_params=_cparams(("parallel", "arbitrary")),
        name="in_proj",
    )(x2d, g_mix, w_big, w_small)


def _gates_body(zs_ref, aneg_ref, dtb_ref, g_ref, gt_ref):
    zs = zs_ref[...]
    lane = lax.broadcasted_iota(jnp.int32, zs.shape, 1)
    row = lax.broadcasted_iota(jnp.int32, zs.shape, 0) % DN_CHUNK
    is_a = (lane >= SM_A) & (lane < SM_B)
    is_b = (lane >= SM_B) & (lane < SM_GL)
    g = jnp.where(is_a, aneg_ref[...] * _softplus(zs + dtb_ref[...]), 0.0)
    gcum = _chunk_cumsum(g, row, DN_CHUNK)
    gtot = gcum + _chunk_rev_cumsum(g, row, DN_CHUNK) - g
    beta = _sigmoid(zs)
    out = jnp.where(is_a, gcum, jnp.where(is_b, beta, 0.0))
    out = out + pltpu.roll(jnp.where(is_a, gtot, 0.0), SM_GL - SM_A, 1)
    g_ref[...] = out
    gt_ref[...] = out.T


def _gates(zs3, a_neg, dt_bias):
    b, s, _ = zs3.shape
    return pl.pallas_call(
        _gates_body,
        grid=(b,),
        in_specs=[
            pl.BlockSpec((None, s, LANES), lambda i: (i, 0, 0)),
            pl.BlockSpec((1, LANES), lambda i: (0, 0)),
            pl.BlockSpec((1, LANES), lambda i: (0, 0)),
        ],
        out_specs=[
            pl.BlockSpec((None, s, LANES), lambda i: (i, 0, 0)),
            pl.BlockSpec((None, LANES, s), lambda i: (i, 0, 0)),
        ],
        out_shape=[
            jax.ShapeDtypeStruct((b, s, LANES), F32),
            jax.ShapeDtypeStruct((b, LANES, s), F32),
        ],
        compiler_params=_cparams(("parallel",)),
        name="dn_gates",
    )(zs3, a_neg, dt_bias)


def _gla_body(hb, scale, q_ref, k_ref, v_ref, lr_ref, w2_ref, b_ref, nrm_ref, sg_ref, ga_ref, o_ref, st_ref):
    s = q_ref.shape[0]
    dk = q_ref.shape[1] // hb
    dv = v_ref.shape[1] // hb
    per = TILE // CHUNK
    heads = range(hb)
    kcol = [pl.ds(h * dk, dk) for h in heads]
    vcol = [pl.ds(h * dv, dv) for h in heads]

    @pl.when(pl.program_id(1) == 0)
    def _():
        st_ref[...] = jnp.zeros_like(st_ref)

    row = lax.broadcasted_iota(jnp.int32, (TILE, dk), 0) % CHUNK
    ri = lax.broadcasted_iota(jnp.int32, (TILE, TILE), 0)
    ci = lax.broadcasted_iota(jnp.int32, (TILE, TILE), 1)
    causal = ((ri // CHUNK) == (ci // CHUNK)) & (ci <= ri)

    def tile(t, carry):
        base = pl.multiple_of(t * TILE, TILE)
        r = pl.ds(base, TILE)
        lrb = lr_ref[r, :].astype(BF16)
        q = [q_ref[r, kcol[h]].astype(F32) * scale for h in heads]
        k = [k_ref[r, kcol[h]].astype(F32) for h in heads]
        v = [v_ref[r, vcol[h]] for h in heads]
        pre = [_dot(lrb, w2_ref[:, kcol[h]]) + b_ref[:, kcol[h]] for h in heads]
        bc = [_chunk_cumsum(_log_sigmoid(x) * (1.0 / GLA_TAU), row) for x in pre]
        q_in = [(q[h] * jnp.exp(bc[h])).astype(BF16) for h in heads]
        k_in = [(k[h] * jnp.exp(-bc[h])).astype(BF16) for h in heads]
        a = [jnp.where(causal, _dot_nt(q_in[h], k_in[h]), 0.0).astype(BF16) for h in heads]
        o_intra = [_dot(a[h], v[h]) for h in heads]
        for c in range(per):
            lo, hi = c * CHUNK, (c + 1) * CHUNK
            b_last = [bc[h][hi - 1:hi, :] for h in heads]
            k_dec = [(k[h][lo:hi, :] * jnp.exp(b_last[h] - bc[h][lo:hi, :])).astype(BF16) for h in heads]
            st = [st_ref[h] for h in heads]
            o = [o_intra[h][lo:hi, :] + _dot_nt(q_in[h][lo:hi, :], st[h].astype(BF16)) for h in heads]
            for h in heads:
                st_ref[h] = st[h] * jnp.exp(b_last[h]) + _dot_tn(v[h][lo:hi, :], k_dec[h])
            rc = pl.ds(base + lo, CHUNK)
            for h in heads:
                gate = sg_ref[rc, vcol[h]].astype(F32) * ga_ref[rc, vcol[h]].astype(F32)
                o_ref[rc, vcol[h]] = (_rms(o[h], nrm_ref[...]) * gate).astype(o_ref.dtype)
        return carry

    lax.fori_loop(0, s // TILE, tile, 0)


def _gla(z3, zs3, w2p, gla_b, gla_norm, d_model, gate_cols, ts=512):
    hb = GLA_HEADS
    b, s, _ = z3.shape
    dk = d_model // (2 * GLA_HEADS)
    dv = d_model // GLA_HEADS
    assert s % ts == 0 and ts % TILE == 0
    qk_cols = GLA_HEADS * dk
    kw, vw = hb * dk, hb * dv
    sg_col, ga_col = gate_cols
    assert sg_col % vw == 0 and ga_col % vw == 0
    body = functools.partial(_gla_body, hb, dk ** -0.5)
    return pl.pallas_call(
        body,
        grid=(b, s // ts),
        in_specs=[
            pl.BlockSpec((None, ts, kw), lambda i, t: (i, t, 0)),
            pl.BlockSpec((None, ts, kw), lambda i, t: (i, t, qk_cols // kw)),
            pl.BlockSpec((None, ts, vw), lambda i, t: (i, t, 2 * qk_cols // vw)),
            pl.BlockSpec((None, ts, LANES), lambda i, t: (i, t, 0)),
            pl.BlockSpec((LANES, kw), lambda i, t: (0, 0)),
            pl.BlockSpec((1, kw), lambda i, t: (0, 0)),
            pl.BlockSpec((1, dv), lambda i, t: (0, 0)),
            pl.BlockSpec((None, ts, vw), lambda i, t: (i, t, sg_col // vw)),
            pl.BlockSpec((None, ts, vw), lambda i, t: (i, t, ga_col // vw)),
        ],
        out_specs=pl.BlockSpec((None, ts, vw), lambda i, t: (i, t, 0)),
        out_shape=jax.ShapeDtypeStruct((b, s, GLA_HEADS * dv), BF16),
        scratch_shapes=[pltpu.VMEM((hb, dv, dk), F32)],
        compiler_params=_cparams(("parallel", "arbitrary")),
        name="gla_mixer",
    )(z3, z3, z3, zs3, w2p, gla_b, gla_norm, z3, z3)


def _conv_silu(x_ref, w_ref, cols, base, n_rows, first, buf):
    pbase = pl.multiple_of(jnp.maximum(base - 8, 0), 8)
    buf[pl.ds(0, 8), :] = jnp.where(first, 0.0, x_ref[pl.ds(pbase, 8), cols].astype(F32))
    buf[pl.ds(8, n_rows), :] = x_ref[pl.ds(base, n_rows), cols].astype(F32)
    w = w_ref[:, cols]
    acc = None
    for d in range(DN_CONV):
        term = buf[pl.ds(8 - d, n_rows), :] * w[DN_CONV - 1 - d:DN_CONV - d, :]
        acc = term if acc is None else acc + term
    return _silu(acc)


def _l2n(t):
    return t * lax.rsqrt(jnp.sum(t * t, axis=-1, keepdims=True) + EPS)


def _lane_pick(x, idx):
    lane = lax.broadcasted_iota(jnp.int32, x.shape, 1)
    return jnp.sum(jnp.where(lane == idx, x, 0.0), axis=-1, keepdims=True)


def _bdot(a, b):
    return lax.dot_general(a, b, (((2,), (1,)), ((0,), (0,))), preferred_element_type=F32)


def _bdot_nt(a, b):
    return lax.dot_general(a, b, (((2,), (2,)), ((0,), (0,))), preferred_element_type=F32)


def _dn_body(hb, qscale, q_ref, k_ref, v_ref, cq_ref, ck_ref, cv_ref, g_ref, gt_ref, nrm_ref,
             o_ref, kp_s, np_s, qp_s, op_s, fl_s, st_s, cbuf):
    s = q_ref.shape[0]
    n_half = DN_TILE // HALF
    per_tile = DN_TILE // DN_CHUNK
    head0 = pl.program_id(1) * hb

    ri = lax.broadcasted_iota(jnp.int32, (HALF, HALF), 0)
    ci = lax.broadcasted_iota(jnp.int32, (HALF, HALF), 1)
    incl = ci <= ri
    sub = (ri // SUB) == (ci // SUB)
    strict_in = sub & (ci < ri)
    strict_off = (~sub) & (ci < ri)
    n_sq = SUB.bit_length() - 2
    b3 = lambda x: x.reshape(n_half, HALF, x.shape[-1])

    n_tiles = s // DN_TILE
    n_units = hb * n_tiles

    def unit_coords(u):
        u = jnp.asarray(u, jnp.int32)
        i = lax.div(u, n_tiles)
        t = lax.rem(u, n_tiles)
        return i, t, pl.ds(pl.multiple_of(i * LANES, LANES), LANES)

    def prep(u):
        i, t, cols = unit_coords(u)
        head = head0 + i
        base = pl.multiple_of(t * DN_TILE, DN_TILE)
        rows = pl.ds(base, DN_TILE)
        first = t == 0
        q = _l2n(_conv_silu(q_ref, cq_ref, cols, base, DN_TILE, first, cbuf.at[0])) * qscale
        k = _l2n(_conv_silu(k_ref, ck_ref, cols, base, DN_TILE, first, cbuf.at[1]))
        v = _conv_silu(v_ref, cv_ref, cols, base, DN_TILE, first, cbuf.at[2])
        gs = g_ref[rows, :]
        gc = _lane_pick(gs, SM_A + head)
        bt = _lane_pick(gs, SM_B + head)
        gl = _lane_pick(gs, SM_GL + head)
        eg = jnp.exp(gc)
        kb = k * bt
        vb = v * bt
        kbe = kb * eg
        qd = q * eg
        kd = (k * jnp.exp(gl - gc)).astype(BF16)
        flb = jnp.broadcast_to(jnp.exp(gl), (DN_TILE, LANES))
        grow_t = gt_ref[pl.ds(SM_A + head, 1), rows]
        grow = jnp.stack([grow_t[:, j * HALF:(j + 1) * HALF] for j in range(n_half)], axis=0)
        dec = jnp.exp(jnp.where(incl, b3(gc) - grow, -jnp.inf))
        kbf = b3(k).astype(BF16)
        a_full = _bdot_nt(b3(kb).astype(BF16), kbf) * dec
        a = jnp.where(strict_in, a_full, 0.0)
        attn = (_bdot_nt(b3(q).astype(BF16), kbf) * dec).astype(BF16)
        p = -a
        r = p
        for _ in range(n_sq):
            pb = p.astype(BF16)
            p = _bdot(pb, pb)
            r = r + p + _bdot(r.astype(BF16), p.astype(BF16))
        nb = jnp.where(strict_off, a_full, 0.0)
        rb = r.astype(BF16)
        tn = nb + _bdot(rb, nb.astype(BF16))
        r = r - (tn + _bdot(tn.astype(BF16), rb))
        rhs = jnp.concatenate([b3(vb), b3(kbe)], axis=2)
        uw = (rhs + _bdot(r.astype(BF16), rhs.astype(BF16))).astype(BF16)
        aw = _bdot(attn, uw)
        op_s[rows, cols] = aw[:, :, :LANES].reshape(DN_TILE, LANES).astype(op_s.dtype)
        qp_s[rows, cols] = (qd - aw[:, :, LANES:].reshape(DN_TILE, LANES)).astype(qp_s.dtype)
        uw2 = uw.reshape(DN_TILE, 2 * LANES)
        for c in range(per_tile):
            c0 = c * DN_CHUNK
            kn = _dot_tn(kd[c0:c0 + DN_CHUNK, :], uw2[c0:c0 + DN_CHUNK, :])
            cidx = t * per_tile + c
            np_s[i, cidx] = kn[:, :LANES].astype(np_s.dtype)
            kp_s[i, cidx] = kn[:, LANES:].astype(kp_s.dtype)
            fl_s[i, pl.ds(cidx, 1), :] = flb[c0:c0 + 1, :]

    def prep_step(u, carry):
        prep(u)
        return carry

    lax.fori_loop(0, n_units, prep_step, 0)

    st_s[...] = jnp.zeros_like(st_s)

    def scan(c, carry):
        r = pl.ds(pl.multiple_of(c * DN_CHUNK, DN_CHUNK), DN_CHUNK)
        for i in range(hb):
            cols = pl.ds(i * LANES, LANES)
            st = st_s[i]
            stb = st.astype(BF16)
            o = _dot(qp_s[r, cols], stb) + op_s[r, cols].astype(F32)
            st_s[i] = st * fl_s[i, pl.ds(c, 1), :] - _dot(kp_s[i, c], stb) + np_s[i, c]
            o_ref[r, cols] = _rms(o, nrm_ref[...]).astype(o_ref.dtype)
        return carry

    lax.fori_loop(0, s // DN_CHUNK, scan, 0, unroll=SCAN_UNROLL)


def _deltanet(z3, dn_conv, g3, gt3, dn_norm, d_model, qkv_off, hb=4):
    b, s, _ = z3.shape
    dh = d_model // DN_HEADS
    assert dh == LANES and s % DN_TILE == 0
    bw = hb * dh
    nblk = d_model // bw
    n_chunks = s // DN_CHUNK
    body = functools.partial(_dn_body, hb, dh ** -0.5)
    z_spec = lambda part: pl.BlockSpec(
        (None, s, bw), lambda i, h: (i, 0, (qkv_off + part * d_model) // bw + h))
    c_spec = lambda part: pl.BlockSpec((DN_CONV, bw), lambda i, h: (0, part * nblk + h))
    return pl.pallas_call(
        body,
        grid=(b, nblk),
        in_specs=[
            z_spec(0), z_spec(1), z_spec(2),
            c_spec(0), c_spec(1), c_spec(2),
            pl.BlockSpec((None, s, LANES), lambda i, h: (i, 0, 0)),
            pl.BlockSpec((None, LANES, s), lambda i, h: (i, 0, 0)),
            pl.BlockSpec((1, dh), lambda i, h: (0, 0)),
        ],
        out_specs=pl.BlockSpec((None, s, bw), lambda i, h: (i, 0, h)),
        out_shape=jax.ShapeDtypeStruct((b, s, d_model), BF16),
        scratch_shapes=[
            pltpu.VMEM((hb, n_chunks, dh, dh), BF16),
            pltpu.VMEM((hb, n_chunks, dh, dh), F32),
            pltpu.VMEM((s, bw), BF16),
            pltpu.VMEM((s, bw), BF16),
            pltpu.VMEM((hb, n_chunks, LANES), F32),
            pltpu.VMEM((hb, dh, dh), F32),
            pltpu.VMEM((3, DN_TILE + 8, LANES), F32),
        ],
        compiler_params=_cparams(("parallel", "parallel")),
        name="deltanet_mixer",
    )(z3, z3, z3, dn_conv, dn_conv, dn_conv, g3, gt3, dn_norm)


def _merge_body(og_ref, od_ref, sz_ref, gb_ref, x_ref, w_ref, g_ref, x1_ref, h2_ref):
    f = lambda r: r[...].astype(F32)
    mixed = f(og_ref) + f(gb_ref) * (f(od_ref) * f(sz_ref))
    x1 = x_ref[...] + _dot(mixed.astype(BF16), w_ref[...])
    x1_ref[...] = x1
    h2_ref[...] = _rms(x1, g_ref[...]).astype(BF16)


def _merge(o_gla, o_dn, z2d, act_cols, x2d, w_out, g_mlp, tm=512):
    t, d = x2d.shape
    assert all(c % d == 0 for c in act_cols)
    row = lambda c: pl.BlockSpec((tm, d), lambda i, c=c: (i, c))
    sz, gb = (c // d for c in act_cols)
    return pl.pallas_call(
        _merge_body,
        grid=(t // tm,),
        in_specs=[
            row(0), row(0), row(sz), row(gb), row(0),
            pl.BlockSpec((d, d), lambda i: (0, 0), pipeline_mode=pl.Buffered(1)),
            pl.BlockSpec((1, d), lambda i: (0, 0)),
        ],
        out_specs=[row(0), row(0)],
        out_shape=[jax.ShapeDtypeStruct((t, d), F32), jax.ShapeDtypeStruct((t, d), BF16)],
        compiler_params=_cparams(("parallel",)),
        name="merge_out_proj",
    )(o_gla, o_dn, z2d, z2d, x2d, w_out, g_mlp)


def _mlp_body(h_ref, wu_ref, wd_ref, x_ref, o_ref):
    @pl.when(pl.program_id(1) == 0)
    def _():
        o_ref[...] = x_ref[...]

    mid = jnp.square(jnp.maximum(_dot(h_ref[...], wu_ref[...]), 0.0)).astype(BF16)
    o_ref[...] += _dot(mid, wd_ref[...])


def _mlp(h2, w_up, w_down, x1, tm=512, tf=1024):
    t, d = x1.shape
    ff = w_up.shape[1]
    return pl.pallas_call(
        _mlp_body,
        grid=(t // tm, ff // tf),
        in_specs=[
            pl.BlockSpec((tm, d), lambda i, j: (i, 0)),
            pl.BlockSpec((d, tf), lambda i, j: (0, j)),
            pl.BlockSpec((tf, d), lambda i, j: (j, 0)),
            pl.BlockSpec((tm, d), lambda i, j: (i, 0)),
        ],
        out_specs=pl.BlockSpec((tm, d), lambda i, j: (i, 0)),
        out_shape=jax.ShapeDtypeStruct((t, d), F32),
        compiler_params=_cparams(("parallel", "arbitrary")),
        name="relu2_mlp",
    )(h2, w_up, w_down, x1)


def _ple_body(x_ref, p_ref, wg_ref, wp_ref, gp_ref, gf_ref, o_ref):
    x2 = x_ref[...]
    h3 = _rms(x2, gp_ref[...]).astype(BF16)
    gate = _sigmoid(_dot(h3, wg_ref[...]))
    proj = _dot(p_ref[...].astype(BF16), wp_ref[...])
    o_ref[...] = _rms(x2 + gate * proj, gf_ref[...])


def _ple(x2, p2d, w_gate, w_proj, g_ple, g_final, tm=512):
    t, d = x2.shape
    pd = p2d.shape[1]
    return pl.pallas_call(
        _ple_body,
        grid=(t // tm,),
        in_specs=[
            pl.BlockSpec((tm, d), lambda i: (i, 0)),
            pl.BlockSpec((tm, pd), lambda i: (i, 0)),
            pl.BlockSpec((d, d), lambda i: (0, 0)),
            pl.BlockSpec((pd, d), lambda i: (0, 0)),
            pl.BlockSpec((1, d), lambda i: (0, 0)),
            pl.BlockSpec((1, d), lambda i: (0, 0)),
        ],
        out_specs=pl.BlockSpec((tm, d), lambda i: (i, 0)),
        out_shape=jax.ShapeDtypeStruct((t, d), F32),
        compiler_params=_cparams(("parallel",)),
        name="ple_final_norm",
    )(x2, p2d, w_gate, w_proj, g_ple, g_final)


def _layer(x, p_i, g_mix, w_in, gla_w2, gla_b, gla_norm, dn_conv, dn_a_log, dn_dt_bias, dn_norm,
           w_out, g_mlp, w_up, w_down, g_ple, w_ple_gate, w_ple_proj, g_out):
    b, s, d = x.shape
    t = b * s
    gla_qk = d // 2
    dn_qkv = 3 * d
    o_q, o_k = 0, gla_qk
    o_v = 2 * gla_qk
    o_g = o_v + d
    o_lr = o_g + d
    o_dn = o_lr + GLA_LOWRANK
    o_z = o_dn + dn_qkv
    o_a = o_z + d
    o_b = o_a + DN_HEADS
    o_ga = o_b + DN_HEADS
    o_gb = o_ga + d
    assert (o_q, o_k, o_v) == (0, gla_qk, 2 * gla_qk)
    w_in16 = w_in.astype(BF16)
    cs = lambda lo, n: w_in16[:, lo:lo + n]
    w_big = _repack(w_in16, ((o_lr, GLA_LOWRANK), (o_a, 2 * DN_HEADS)))
    w_small = jnp.concatenate(
        [cs(o_lr, GLA_LOWRANK), cs(o_a, DN_HEADS), cs(o_b, DN_HEADS),
         jnp.zeros((d, LANES - SM_GL), BF16)], axis=1)
    n_g, n_dn, n_z = o_g, o_lr, o_lr + dn_qkv
    n_ga = n_z + d
    n_gb = n_ga + d
    segments = ((0, ACT_NONE), (n_g, ACT_SILU), (n_dn, ACT_NONE), (n_z, ACT_SILU), (n_ga, ACT_SIGMOID))
    qkv_off = n_dn

    x2d = x.reshape(t, d)
    z2d, zs2d = _in_proj(x2d, g_mix.reshape(1, d), w_big, w_small, segments)
    z3 = z2d.reshape(b, s, -1)
    zs3 = zs2d.reshape(b, s, LANES)

    pad = lambda v: jnp.zeros((1, LANES), F32).at[0, SM_A:SM_B].set(v.astype(F32))
    g3, gt3 = _gates(zs3, pad(-jnp.exp(dn_a_log.astype(F32))), pad(dn_dt_bias))

    w2p = jnp.zeros((LANES, gla_qk), F32).at[:GLA_LOWRANK].set(gla_w2).astype(BF16)
    o_gla = _gla(z3, zs3, w2p, gla_b.reshape(1, -1).astype(F32), gla_norm.reshape(1, -1).astype(F32), d,
                 (n_g, n_ga))
    o_dn = _deltanet(z3, dn_conv.astype(F32), g3, gt3, dn_norm.reshape(1, -1).astype(F32), d, qkv_off)

    x1, h2 = _merge(o_gla.reshape(t, d), o_dn.reshape(t, d), z2d, (n_z, n_gb), x2d,
                    w_out.astype(BF16), g_mlp.reshape(1, d))
    x2 = _mlp(h2, w_up.astype(BF16), w_down.astype(BF16), x1)
    out = _ple(x2, p_i.reshape(t, -1), w_ple_gate.astype(BF16), w_ple_proj.astype(BF16),
               g_ple.reshape(1, d), g_out.reshape(1, d))
    return out.reshape(b, s, d)


def kernel(x, p, g_mix, w_in, gla_w2, gla_b, gla_norm, dn_conv, dn_a_log, dn_dt_bias, dn_norm,
           w_out, g_mlp, w_up, w_down, g_ple, w_ple_gate, w_ple_proj, g_final):
    depth = w_in.shape[0]
    assert depth == 1, "the final rms_norm is fused into the last layer's kernel"
    return _layer(x, p[0], g_mix[0], w_in[0], gla_w2[0], gla_b[0], gla_norm[0], dn_conv[0],
                  dn_a_log[0], dn_dt_bias[0], dn_norm[0], w_out[0], g_mlp[0], w_up[0], w_down[0],
                  g_ple[0], w_ple_gate[0], w_ple_proj[0], g_final)
```

```python
import functools

import jax
import jax.numpy as jnp
from jax import lax
from jax.experimental import pallas as pl
from jax.experimental.pallas import tpu as pltpu

F32 = jnp.float32
BF16 = jnp.bfloat16

EPS = 1e-6
CHUNK = 64
GLA_HEADS = 4
GLA_LOWRANK = 16
GLA_TAU = 16.0
DN_HEADS = 16
DN_CONV = 4
LANES = 128
TILE = 256
DN_CHUNK = 128
HALF = DN_CHUNK
DN_TILE = 2048
SUB = 64
SCAN_UNROLL = 8
VMEM_LIMIT = 56 * 1024 * 1024

SM_A = GLA_LOWRANK
SM_B = SM_A + DN_HEADS
SM_GL = SM_B + DN_HEADS


def _cparams(sem):
    return pltpu.CompilerParams(dimension_semantics=sem, vmem_limit_bytes=VMEM_LIMIT)


def _dot(a, b):
    return jnp.dot(a, b, preferred_element_type=F32)


def _dot_nt(a, b):
    return lax.dot_general(a, b, (((1,), (1,)), ((), ())), preferred_element_type=F32)


def _dot_tn(a, b):
    return lax.dot_general(a, b, (((0,), (0,)), ((), ())), preferred_element_type=F32)


def _sigmoid(x):
    return 0.5 * jnp.tanh(0.5 * x) + 0.5


def _silu(x):
    return x * _sigmoid(x)


def _log_sigmoid(x):
    return jnp.minimum(x, 0.0) - jnp.log(1.0 + jnp.exp(-jnp.abs(x)))


def _softplus(x):
    return jnp.maximum(x, 0.0) + jnp.log(1.0 + jnp.exp(-jnp.abs(x)))


def _rms(x, g):
    return x * lax.rsqrt(jnp.mean(x * x, axis=-1, keepdims=True) + EPS) * g


def _chunk_cumsum(x, row, chunk=CHUNK):
    shift = 1
    while shift < chunk:
        x = x + jnp.where(row >= shift, pltpu.roll(x, shift, 0), 0.0)
        shift *= 2
    return x


def _chunk_rev_cumsum(x, row, chunk=CHUNK):
    n = x.shape[0]
    shift = 1
    while shift < chunk:
        x = x + jnp.where(row < chunk - shift, pltpu.roll(x, n - shift, 0), 0.0)
        shift *= 2
    return x


def _repack_body(shifts, w_ref, nxt_ref, o_ref):
    j = pl.program_id(1)
    tn = o_ref.shape[1]
    for shift in sorted(set(shifts)):
        hit = functools.reduce(lambda p, q: p | q, [j == jt for jt, sh in enumerate(shifts) if sh == shift])

        @pl.when(hit)
        def _(shift=shift):
            if shift == 0:
                o_ref[...] = w_ref[...]
            else:
                o_ref[...] = jnp.concatenate([w_ref[:, shift:], nxt_ref[:, :shift]], axis=1)


def _repack(w16, cuts, tr=1024, tn=2048):
    d, n_src = w16.shape
    n_out = n_src - sum(w for _, w in cuts)
    assert n_out % tn == 0 and d % tr == 0
    shifts = tuple(sum(w for s, w in cuts if s <= jt * tn + sum(w2 for s2, w2 in cuts if s2 < s)) for jt in range(n_out // tn))
    assert max(shifts) < LANES
    assert n_out + shifts[-1] == n_src
    return pl.pallas_call(
        functools.partial(_repack_body, shifts),
        grid=(d // tr, n_out // tn),
        in_specs=[
            pl.BlockSpec((tr, tn), lambda i, j: (i, j)),
            pl.BlockSpec((tr, LANES), lambda i, j: (i, (j + 1) * (tn // LANES))),
        ],
        out_specs=pl.BlockSpec((tr, tn), lambda i, j: (i, j)),
        out_shape=jax.ShapeDtypeStruct((d, n_out), w16.dtype),
        compiler_params=_cparams(("parallel", "parallel")),
        name="w_in_repack",
    )(w16, w16)


ACT_NONE, ACT_SILU, ACT_SIGMOID = 0, 1, 2
_ACT_FN = {ACT_NONE: lambda v: v, ACT_SILU: _silu, ACT_SIGMOID: _sigmoid}


def _in_proj_body(tile_acts, x_ref, g_ref, wb_ref, ws_ref, z_ref, zs_ref, h_ref):
    j = pl.program_id(1)

    @pl.when(j == 0)
    def _():
        hb = _rms(x_ref[...], g_ref[...]).astype(BF16)
        h_ref[...] = hb
        zs_ref[...] = _dot(hb, ws_ref[...])

    for act, fn in _ACT_FN.items():
        tiles = [jt for jt, a in enumerate(tile_acts) if a == act]
        if not tiles:
            continue
        hit = functools.reduce(lambda p, q: p | q, [j == jt for jt in tiles])

        @pl.when(hit)
        def _(fn=fn):
            z_ref[...] = fn(_dot(h_ref[...], wb_ref[...])).astype(BF16)


def _in_proj(x2d, g_mix, w_big, w_small, segments, tm=1024, tn=2048):
    t, d = x2d.shape
    n = w_big.shape[1]
    assert all(lo % tn == 0 for lo, _ in segments) and n % tn == 0
    starts = [lo // tn for lo, _ in segments] + [n // tn]
    tile_acts = tuple(act for (_, act), a, b in zip(segments, starts, starts[1:]) for _ in range(b - a))
    body = functools.partial(_in_proj_body, tile_acts)
    return pl.pallas_call(
        body,
        grid=(t // tm, n // tn),
        in_specs=[
            pl.BlockSpec((tm, d), lambda i, j: (i, 0)),
            pl.BlockSpec((1, d), lambda i, j: (0, 0)),
            pl.BlockSpec((d, tn), lambda i, j: (0, j)),
            pl.BlockSpec((d, LANES), lambda i, j: (0, 0)),
        ],
        out_specs=[
            pl.BlockSpec((tm, tn), lambda i, j: (i, j)),
            pl.BlockSpec((tm, LANES), lambda i, j: (i, 0)),
        ],
        out_shape=[
            jax.ShapeDtypeStruct((t, n), BF16),
            jax.ShapeDtypeStruct((t, LANES), F32),
        ],
        scratch_shapes=[pltpu.VMEM((tm, d), BF16)],
        compiler_params=_cparams(("parallel", "arbitrary")),
        name="in_proj",
    )(x2d, g_mix, w_big, w_small)


def _gates_body(zs_ref, aneg_ref, dtb_ref, g_ref, gt_ref):
    zs = zs_ref[...]
    lane = lax.broadcasted_iota(jnp.int32, zs.shape, 1)
    row = lax.broadcasted_iota(jnp.int32, zs.shape, 0) % DN_CHUNK
    is_a = (lane >= SM_A) & (lane < SM_B)
    is_b = (lane >= SM_B) & (lane < SM_GL)
    g = jnp.where(is_a, aneg_ref[...] * _softplus(zs + dtb_ref[...]), 0.0)
    gcum = _chunk_cumsum(g, row, DN_CHUNK)
    gtot = gcum + _chunk_rev_cumsum(g, row, DN_CHUNK) - g
    beta = _sigmoid(zs)
    out = jnp.where(is_a, gcum, jnp.where(is_b, beta, 0.0))
    out = out + pltpu.roll(jnp.where(is_a, gtot, 0.0), SM_GL - SM_A, 1)
    g_ref[...] = out
    gt_ref[...] = out.T


def _gates(zs3, a_neg, dt_bias):
    b, s, _ = zs3.shape
    return pl.pallas_call(
        _gates_body,
        grid=(b,),
        in_specs=[
            pl.BlockSpec((None, s, LANES), lambda i: (i, 0, 0)),
            pl.BlockSpec((1, LANES), lambda i: (0, 0)),
            pl.BlockSpec((1, LANES), lambda i: (0, 0)),
        ],
        out_specs=[
            pl.BlockSpec((None, s, LANES), lambda i: (i, 0, 0)),
            pl.BlockSpec((None, LANES, s), lambda i: (i, 0, 0)),
        ],
        out_shape=[
            jax.ShapeDtypeStruct((b, s, LANES), F32),
            jax.ShapeDtypeStruct((b, LANES, s), F32),
        ],
        compiler_params=_cparams(("parallel",)),
        name="dn_gates",
    )(zs3, a_neg, dt_bias)


def _gla_body(hb, scale, q_ref, k_ref, v_ref, lr_ref, w2_ref, b_ref, nrm_ref, sg_ref, ga_ref, o_ref, st_ref):
    s = q_ref.shape[0]
    dk = q_ref.shape[1] // hb
    dv = v_ref.shape[1] // hb
    per = TILE // CHUNK
    heads = range(hb)
    kcol = [pl.ds(h * dk, dk) for h in heads]
    vcol = [pl.ds(h * dv, dv) for h in heads]

    @pl.when(pl.program_id(1) == 0)
    def _():
        st_ref[...] = jnp.zeros_like(st_ref)

    row = lax.broadcasted_iota(jnp.int32, (TILE, dk), 0) % CHUNK
    ri = lax.broadcasted_iota(jnp.int32, (TILE, TILE), 0)
    ci = lax.broadcasted_iota(jnp.int32, (TILE, TILE), 1)
    causal = ((ri // CHUNK) == (ci // CHUNK)) & (ci <= ri)
    cross = ((ri // CHUNK) == (ci // CHUNK) + 1) & ((ri // (2 * CHUNK)) == (ci // (2 * CHUNK)))

    def tile(t, carry):
        base = pl.multiple_of(t * TILE, TILE)
        r = pl.ds(base, TILE)
        lrb = lr_ref[r, :].astype(BF16)
        q = [q_ref[r, kcol[h]].astype(F32) * scale for h in heads]
        k = [k_ref[r, kcol[h]].astype(F32) for h in heads]
        v = [v_ref[r, vcol[h]] for h in heads]
        pre = [_dot(lrb, w2_ref[:, kcol[h]]) + b_ref[:, kcol[h]] for h in heads]
        bc = [_chunk_cumsum(_log_sigmoid(x) * (1.0 / GLA_TAU), row) for x in pre]
        q_dec = [q[h] * jnp.exp(bc[h]) for h in heads]
        q_in = [x.astype(BF16) for x in q_dec]
        k_in = [(k[h] * jnp.exp(-bc[h])).astype(BF16) for h in heads]
        bl = [jnp.concatenate([jnp.broadcast_to(bc[h][(c + 1) * CHUNK - 1:(c + 1) * CHUNK, :], (CHUNK, dk))
                               for c in range(per)], axis=0) for h in heads]
        k_dec = [k[h] * jnp.exp(bl[h] - bc[h]) for h in heads]
        a = [(jnp.where(causal, _dot_nt(q_in[h], k_in[h]), 0.0)
              + jnp.where(cross, _dot_nt(q_in[h], k_dec[h].astype(BF16)), 0.0)).astype(BF16) for h in heads]
        o_intra = [_dot(a[h], v[h]) for h in heads]
        for m in range(per // 2):
            p0, p1, p2 = 2 * m * CHUNK, (2 * m + 1) * CHUNK, (2 * m + 2) * CHUNK
            f1 = [jnp.exp(bc[h][p1 - 1:p1, :]) for h in heads]
            f2 = [jnp.exp(bc[h][p2 - 1:p2, :]) for h in heads]
            qs = [jnp.concatenate([q_in[h][p0:p1, :], (q_dec[h][p1:p2, :] * f1[h]).astype(BF16)], axis=0)
                  for h in heads]
            kd = [jnp.concatenate([(k_dec[h][p0:p1, :] * f2[h]).astype(BF16), k_dec[h][p1:p2, :].astype(BF16)],
                                  axis=0) for h in heads]
            st = [st_ref[h] for h in heads]
            o = [o_intra[h][p0:p2, :] + _dot_nt(qs[h], st[h].astype(BF16)) for h in heads]
            for h in heads:
                st_ref[h] = st[h] * (f1[h] * f2[h]) + _dot_tn(v[h][p0:p2, :], kd[h])
            rc = pl.ds(base + p0, 2 * CHUNK)
            for h in heads:
                gate = sg_ref[rc, vcol[h]].astype(F32) * ga_ref[rc, vcol[h]].astype(F32)
                o_ref[rc, vcol[h]] = (_rms(o[h], nrm_ref[...]) * gate).astype(o_ref.dtype)
        return carry

    lax.fori_loop(0, s // TILE, tile, 0)


def _gla(z3, zs3, w2p, gla_b, gla_norm, d_model, gate_cols, ts=512):
    hb = GLA_HEADS
    b, s, _ = z3.shape
    dk = d_model // (2 * GLA_HEADS)
    dv = d_model // GLA_HEADS
    assert s % ts == 0 and ts % TILE == 0
    qk_cols = GLA_HEADS * dk
    kw, vw = hb * dk, hb * dv
    sg_col, ga_col = gate_cols
    assert sg_col % vw == 0 and ga_col % vw == 0
    body = functools.partial(_gla_body, hb, dk ** -0.5)
    return pl.pallas_call(
        body,
        grid=(b, s // ts),
        in_specs=[
            pl.BlockSpec((None, ts, kw), lambda i, t: (i, t, 0)),
            pl.BlockSpec((None, ts, kw), lambda i, t: (i, t, qk_cols // kw)),
            pl.BlockSpec((None, ts, vw), lambda i, t: (i, t, 2 * qk_cols // vw)),
            pl.BlockSpec((None, ts, LANES), lambda i, t: (i, t, 0)),
            pl.BlockSpec((LANES, kw), lambda i, t: (0, 0)),
            pl.BlockSpec((1, kw), lambda i, t: (0, 0)),
            pl.BlockSpec((1, dv), lambda i, t: (0, 0)),
            pl.BlockSpec((None, ts, vw), lambda i, t: (i, t, sg_col // vw)),
            pl.BlockSpec((None, ts, vw), lambda i, t: (i, t, ga_col // vw)),
        ],
        out_specs=pl.BlockSpec((None, ts, vw), lambda i, t: (i, t, 0)),
        out_shape=jax.ShapeDtypeStruct((b, s, GLA_HEADS * dv), BF16),
        scratch_shapes=[pltpu.VMEM((hb, dv, dk), F32)],
        compiler_params=_cparams(("parallel", "arbitrary")),
        name="gla_mixer",
    )(z3, z3, z3, zs3, w2p, gla_b, gla_norm, z3, z3)


def _conv_silu(x_ref, w_ref, cols, base, n_rows, first, buf):
    pbase = pl.multiple_of(jnp.maximum(base - 8, 0), 8)
    buf[pl.ds(0, 8), :] = jnp.where(first, 0.0, x_ref[pl.ds(pbase, 8), cols].astype(F32))
    buf[pl.ds(8, n_rows), :] = x_ref[pl.ds(base, n_rows), cols].astype(F32)
    w = w_ref[:, cols]
    acc = None
    for d in range(DN_CONV):
        term = buf[pl.ds(8 - d, n_rows), :] * w[DN_CONV - 1 - d:DN_CONV - d, :]
        acc = term if acc is None else acc + term
    return _silu(acc)


def _l2n(t):
    return t * lax.rsqrt(jnp.sum(t * t, axis=-1, keepdims=True) + EPS)


def _lane_pick(x, idx):
    lane = lax.broadcasted_iota(jnp.int32, x.shape, 1)
    return jnp.sum(jnp.where(lane == idx, x, 0.0), axis=-1, keepdims=True)


def _bdot(a, b):
    return lax.dot_general(a, b, (((2,), (1,)), ((0,), (0,))), preferred_element_type=F32)


def _bdot_nt(a, b):
    return lax.dot_general(a, b, (((2,), (2,)), ((0,), (0,))), preferred_element_type=F32)


def _dn_body(hb, qscale, q_ref, k_ref, v_ref, cq_ref, ck_ref, cv_ref, g_ref, gt_ref, nrm_ref,
             o_ref, kp_s, np_s, qp_s, op_s, fl_s, st_s, cbuf):
    s = q_ref.shape[0]
    n_half = DN_TILE // HALF
    per_tile = DN_TILE // DN_CHUNK
    head0 = pl.program_id(1) * hb

    ri = lax.broadcasted_iota(jnp.int32, (HALF, HALF), 0)
    ci = lax.broadcasted_iota(jnp.int32, (HALF, HALF), 1)
    incl = ci <= ri
    sub = (ri // SUB) == (ci // SUB)
    strict_in = sub & (ci < ri)
    strict_off = (~sub) & (ci < ri)
    n_sq = SUB.bit_length() - 2
    b3 = lambda x: x.reshape(n_half, HALF, x.shape[-1])

    n_tiles = s // DN_TILE
    n_units = hb * n_tiles

    def unit_coords(u):
        u = jnp.asarray(u, jnp.int32)
        i = lax.div(u, n_tiles)
        t = lax.rem(u, n_tiles)
        return i, t, pl.ds(pl.multiple_of(i * LANES, LANES), LANES)

    def prep(u):
        i, t, cols = unit_coords(u)
        head = head0 + i
        base = pl.multiple_of(t * DN_TILE, DN_TILE)
        rows = pl.ds(base, DN_TILE)
        first = t == 0
        q = _l2n(_conv_silu(q_ref, cq_ref, cols, base, DN_TILE, first, cbuf.at[0])) * qscale
        k = _l2n(_conv_silu(k_ref, ck_ref, cols, base, DN_TILE, first, cbuf.at[1]))
        v = _conv_silu(v_ref, cv_ref, cols, base, DN_TILE, first, cbuf.at[2])
        gs = g_ref[rows, :]
        gc = _lane_pick(gs, SM_A + head)
        bt = _lane_pick(gs, SM_B + head)
        gl = _lane_pick(gs, SM_GL + head)
        eg = jnp.exp(gc)
        kb = k * bt
        vb = v * bt
        kbe = kb * eg
        qd = q * eg
        kd = (k * jnp.exp(gl - gc)).astype(BF16)
        flb = jnp.broadcast_to(jnp.exp(gl), (DN_TILE, LANES))
        grow_t = gt_ref[pl.ds(SM_A + head, 1), rows]
        grow = jnp.stack([grow_t[:, j * HALF:(j + 1) * HALF] for j in range(n_half)], axis=0)
        dec = jnp.exp(jnp.where(incl, b3(gc) - grow, -jnp.inf))
        kbf = b3(k).astype(BF16)
        a_full = _bdot_nt(b3(kb).astype(BF16), kbf) * dec
        a = jnp.where(strict_in, a_full, 0.0)
        attn = (_bdot_nt(b3(q).astype(BF16), kbf) * dec).astype(BF16)
        p = -a
        r = p
        for _ in range(n_sq):
            pb = p.astype(BF16)
            p = _bdot(pb, pb)
            r = r + p + _bdot(r.astype(BF16), p.astype(BF16))
        nb = jnp.where(strict_off, a_full, 0.0)
        rb = r.astype(BF16)
        tn = nb + _bdot(rb, nb.astype(BF16))
        r = r - (tn + _bdot(tn.astype(BF16), rb))
        rhs = jnp.concatenate([b3(vb), b3(kbe)], axis=2)
        uw = (rhs + _bdot(r.astype(BF16), rhs.astype(BF16))).astype(BF16)
        aw = _bdot(attn, uw)
        op_s[rows, cols] = aw[:, :, :LANES].reshape(DN_TILE, LANES).astype(op_s.dtype)
        qp_s[rows, cols] = (qd - aw[:, :, LANES:].reshape(DN_TILE, LANES)).astype(qp_s.dtype)
        uw2 = uw.reshape(DN_TILE, 2 * LANES)
        for c in range(per_tile):
            c0 = c * DN_CHUNK
            kn = _dot_tn(kd[c0:c0 + DN_CHUNK, :], uw2[c0:c0 + DN_CHUNK, :])
            cidx = t * per_tile + c
            np_s[i, cidx] = kn[:, :LANES].astype(np_s.dtype)
            kp_s[i, cidx] = kn[:, LANES:].astype(kp_s.dtype)
            fl_s[i, pl.ds(cidx, 1), :] = flb[c0:c0 + 1, :]

    def prep_step(u, carry):
        prep(u)
        return carry

    lax.fori_loop(0, n_units, prep_step, 0)

    st_s[...] = jnp.zeros_like(st_s)

    def scan(c, carry):
        r = pl.ds(pl.multiple_of(c * DN_CHUNK, DN_CHUNK), DN_CHUNK)
        for i in range(hb):
            cols = pl.ds(i * LANES, LANES)
            st = st_s[i]
            stb = st.astype(BF16)
            o = _dot(qp_s[r, cols], stb) + op_s[r, cols].astype(F32)
            st_s[i] = st * fl_s[i, pl.ds(c, 1), :] - _dot(kp_s[i, c], stb) + np_s[i, c]
            o_ref[r, cols] = _rms(o, nrm_ref[...]).astype(o_ref.dtype)
        return carry

    lax.fori_loop(0, s // DN_CHUNK, scan, 0, unroll=SCAN_UNROLL)


def _deltanet(z3, dn_conv, g3, gt3, dn_norm, d_model, qkv_off, hb=4):
    b, s, _ = z3.shape
    dh = d_model // DN_HEADS
    assert dh == LANES and s % DN_TILE == 0
    bw = hb * dh
    nblk = d_model // bw
    n_chunks = s // DN_CHUNK
    body = functools.partial(_dn_body, hb, dh ** -0.5)
    z_spec = lambda part: pl.BlockSpec(
        (None, s, bw), lambda i, h: (i, 0, (qkv_off + part * d_model) // bw + h))
    c_spec = lambda part: pl.BlockSpec((DN_CONV, bw), lambda i, h: (0, part * nblk + h))
    return pl.pallas_call(
        body,
        grid=(b, nblk),
        in_specs=[
            z_spec(0), z_spec(1), z_spec(2),
            c_spec(0), c_spec(1), c_spec(2),
            pl.BlockSpec((None, s, LANES), lambda i, h: (i, 0, 0)),
            pl.BlockSpec((None, LANES, s), lambda i, h: (i, 0, 0)),
            pl.BlockSpec((1, dh), lambda i, h: (0, 0)),
        ],
        out_specs=pl.BlockSpec((None, s, bw), lambda i, h: (i, 0, h)),
        out_shape=jax.ShapeDtypeStruct((b, s, d_model), BF16),
        scratch_shapes=[
            pltpu.VMEM((hb, n_chunks, dh, dh), BF16),
            pltpu.VMEM((hb, n_chunks, dh, dh), F32),
            pltpu.VMEM((s, bw), BF16),
            pltpu.VMEM((s, bw), BF16),
            pltpu.VMEM((hb, n_chunks, LANES), F32),
            pltpu.VMEM((hb, dh, dh), F32),
            pltpu.VMEM((3, DN_TILE + 8, LANES), F32),
        ],
        compiler_params=_cparams(("parallel", "parallel")),
        name="deltanet_mixer",
    )(z3, z3, z3, dn_conv, dn_conv, dn_conv, g3, gt3, dn_norm)


def _merge_body(og_ref, od_ref, sz_ref, gb_ref, x_ref, w_ref, g_ref, x1_ref, h2_ref):
    f = lambda r: r[...].astype(F32)
    mixed = f(og_ref) + f(gb_ref) * (f(od_ref) * f(sz_ref))
    x1 = x_ref[...] + _dot(mixed.astype(BF16), w_ref[...])
    x1_ref[...] = x1
    h2_ref[...] = _rms(x1, g_ref[...]).astype(BF16)


def _merge(o_gla, o_dn, z2d, act_cols, x2d, w_out, g_mlp, tm=512):
    t, d = x2d.shape
    assert all(c % d == 0 for c in act_cols)
    row = lambda c: pl.BlockSpec((tm, d), lambda i, c=c: (i, c))
    sz, gb = (c // d for c in act_cols)
    return pl.pallas_call(
        _merge_body,
        grid=(t // tm,),
        in_specs=[
            row(0), row(0), row(sz), row(gb), row(0),
            pl.BlockSpec((d, d), lambda i: (0, 0), pipeline_mode=pl.Buffered(1)),
            pl.BlockSpec((1, d), lambda i: (0, 0)),
        ],
        out_specs=[row(0), row(0)],
        out_shape=[jax.ShapeDtypeStruct((t, d), F32), jax.ShapeDtypeStruct((t, d), BF16)],
        compiler_params=_cparams(("parallel",)),
        name="merge_out_proj",
    )(o_gla, o_dn, z2d, z2d, x2d, w_out, g_mlp)


def _mlp_body(h_ref, wu_ref, wd_ref, x_ref, o_ref):
    @pl.when(pl.program_id(1) == 0)
    def _():
        o_ref[...] = x_ref[...]

    mid = jnp.square(jnp.maximum(_dot(h_ref[...], wu_ref[...]), 0.0)).astype(BF16)
    o_ref[...] += _dot(mid, wd_ref[...])


def _mlp(h2, w_up, w_down, x1, tm=512, tf=1024):
    t, d = x1.shape
    ff = w_up.shape[1]
    return pl.pallas_call(
        _mlp_body,
        grid=(t // tm, ff // tf),
        in_specs=[
            pl.BlockSpec((tm, d), lambda i, j: (i, 0)),
            pl.BlockSpec((d, tf), lambda i, j: (0, j)),
            pl.BlockSpec((tf, d), lambda i, j: (j, 0)),
            pl.BlockSpec((tm, d), lambda i, j: (i, 0)),
        ],
        out_specs=pl.BlockSpec((tm, d), lambda i, j: (i, 0)),
        out_shape=jax.ShapeDtypeStruct((t, d), F32),
        compiler_params=_cparams(("parallel", "arbitrary")),
        name="relu2_mlp",
    )(h2, w_up, w_down, x1)


def _ple_body(x_ref, p_ref, wg_ref, wp_ref, gp_ref, gf_ref, o_ref):
    x2 = x_ref[...]
    h3 = _rms(x2, gp_ref[...]).astype(BF16)
    gate = _sigmoid(_dot(h3, wg_ref[...]))
    proj = _dot(p_ref[...].astype(BF16), wp_ref[...])
    o_ref[...] = _rms(x2 + gate * proj, gf_ref[...])


def _ple(x2, p2d, w_gate, w_proj, g_ple, g_final, tm=512):
    t, d = x2.shape
    pd = p2d.shape[1]
    return pl.pallas_call(
        _ple_body,
        grid=(t // tm,),
        in_specs=[
            pl.BlockSpec((tm, d), lambda i: (i, 0)),
            pl.BlockSpec((tm, pd), lambda i: (i, 0)),
            pl.BlockSpec((d, d), lambda i: (0, 0)),
            pl.BlockSpec((pd, d), lambda i: (0, 0)),
            pl.BlockSpec((1, d), lambda i: (0, 0)),
            pl.BlockSpec((1, d), lambda i: (0, 0)),
        ],
        out_specs=pl.BlockSpec((tm, d), lambda i: (i, 0)),
        out_shape=jax.ShapeDtypeStruct((t, d), F32),
        compiler_params=_cparams(("parallel",)),
        name="ple_final_norm",
    )(x2, p2d, w_gate, w_proj, g_ple, g_final)


def _layer(x, p_i, g_mix, w_in, gla_w2, gla_b, gla_norm, dn_conv, dn_a_log, dn_dt_bias, dn_norm,
           w_out, g_mlp, w_up, w_down, g_ple, w_ple_gate, w_ple_proj, g_out):
    b, s, d = x.shape
    t = b * s
    gla_qk = d // 2
    dn_qkv = 3 * d
    o_q, o_k = 0, gla_qk
    o_v = 2 * gla_qk
    o_g = o_v + d
    o_lr = o_g + d
    o_dn = o_lr + GLA_LOWRANK
    o_z = o_dn + dn_qkv
    o_a = o_z + d
    o_b = o_a + DN_HEADS
    o_ga = o_b + DN_HEADS
    o_gb = o_ga + d
    assert (o_q, o_k, o_v) == (0, gla_qk, 2 * gla_qk)
    w_in16 = w_in.astype(BF16)
    cs = lambda lo, n: w_in16[:, lo:lo + n]
    w_big = _repack(w_in16, ((o_lr, GLA_LOWRANK), (o_a, 2 * DN_HEADS)))
    w_small = jnp.concatenate(
        [cs(o_lr, GLA_LOWRANK), cs(o_a, DN_HEADS), cs(o_b, DN_HEADS),
         jnp.zeros((d, LANES - SM_GL), BF16)], axis=1)
    n_g, n_dn, n_z = o_g, o_lr, o_lr + dn_qkv
    n_ga = n_z + d
    n_gb = n_ga + d
    segments = ((0, ACT_NONE), (n_g, ACT_SILU), (n_dn, ACT_NONE), (n_z, ACT_SILU), (n_ga, ACT_SIGMOID))
    qkv_off = n_dn

    x2d = x.reshape(t, d)
    z2d, zs2d = _in_proj(x2d, g_mix.reshape(1, d), w_big, w_small, segments)
    z3 = z2d.reshape(b, s, -1)
    zs3 = zs2d.reshape(b, s, LANES)

    pad = lambda v: jnp.zeros((1, LANES), F32).at[0, SM_A:SM_B].set(v.astype(F32))
    g3, gt3 = _gates(zs3, pad(-jnp.exp(dn_a_log.astype(F32))), pad(dn_dt_bias))

    w2p = jnp.zeros((LANES, gla_qk), F32).at[:GLA_LOWRANK].set(gla_w2).astype(BF16)
    o_gla = _gla(z3, zs3, w2p, gla_b.reshape(1, -1).astype(F32), gla_norm.reshape(1, -1).astype(F32), d,
                 (n_g, n_ga))
    o_dn = _deltanet(z3, dn_conv.astype(F32), g3, gt3, dn_norm.reshape(1, -1).astype(F32), d, qkv_off)

    x1, h2 = _merge(o_gla.reshape(t, d), o_dn.reshape(t, d), z2d, (n_z, n_gb), x2d,
                    w_out.astype(BF16), g_mlp.reshape(1, d))
    x2 = _mlp(h2, w_up.astype(BF16), w_down.astype(BF16), x1)
    out = _ple(x2, p_i.reshape(t, -1), w_ple_gate.astype(BF16), w_ple_proj.astype(BF16),
               g_ple.reshape(1, d), g_out.reshape(1, d))
    return out.reshape(b, s, d)


def kernel(x, p, g_mix, w_in, gla_w2, gla_b, gla_norm, dn_conv, dn_a_log, dn_dt_bias, dn_norm,
           w_out, g_mlp, w_up, w_down, g_ple, w_ple_gate, w_ple_proj, g_final):
    depth = w_in.shape[0]
    assert depth == 1, "the final rms_norm is fused into the last layer's kernel"
    return _layer(x, p[0], g_mix[0], w_in[0], gla_w2[0], gla_b[0], gla_norm[0], dn_conv[0],
                  dn_a_log[0], dn_dt_bias[0], dn_norm[0], w_out[0], g_mlp[0], w_up[0], w_down[0],
                  g_ple[0], w_ple_gate[0], w_ple_proj[0], g_final)
```

```python
import functools

import jax
import jax.numpy as jnp
from jax import lax
from jax.experimental import pallas as pl
from jax.experimental.pallas import tpu as pltpu

F32 = jnp.float32
BF16 = jnp.bfloat16

EPS = 1e-6
CHUNK = 64
GLA_HEADS = 4
GLA_LOWRANK = 16
GLA_TAU = 16.0
DN_HEADS = 16
DN_CONV = 4
LANES = 128
TILE = 256
DN_CHUNK = 128
HALF = DN_CHUNK
DN_TILE = 2048
SUB = 64
SCAN_UNROLL = 8
VMEM_LIMIT = 56 * 1024 * 1024

SM_A = GLA_LOWRANK
SM_B = SM_A + DN_HEADS
SM_GL = SM_B + DN_HEADS


def _cparams(sem):
    return pltpu.CompilerParams(dimension_semantics=sem, vmem_limit_bytes=VMEM_LIMIT)


def _dot(a, b):
    return jnp.dot(a, b, preferred_element_type=F32)


def _dot_nt(a, b):
    return lax.dot_general(a, b, (((1,), (1,)), ((), ())), preferred_element_type=F32)


def _dot_tn(a, b):
    return lax.dot_general(a, b, (((0,), (0,)), ((), ())), preferred_element_type=F32)


def _sigmoid(x):
    return 0.5 * jnp.tanh(0.5 * x) + 0.5


def _silu_half(h):
    return h * jnp.tanh(h) + h


def _silu(x):
    return _silu_half(0.5 * x)


def _log_sigmoid(x):
    return jnp.minimum(x, 0.0) - jnp.log(1.0 + jnp.exp(-jnp.abs(x)))


def _softplus(x):
    return jnp.maximum(x, 0.0) + jnp.log(1.0 + jnp.exp(-jnp.abs(x)))


def _rms(x, g):
    return x * lax.rsqrt(jnp.mean(x * x, axis=-1, keepdims=True) + EPS) * g


def _chunk_cumsum(x, row, chunk=CHUNK):
    shift = 1
    while shift < chunk:
        x = x + jnp.where(row >= shift, pltpu.roll(x, shift, 0), 0.0)
        shift *= 2
    return x


def _chunk_rev_cumsum(x, row, chunk=CHUNK):
    n = x.shape[0]
    shift = 1
    while shift < chunk:
        x = x + jnp.where(row < chunk - shift, pltpu.roll(x, n - shift, 0), 0.0)
        shift *= 2
    return x


def _repack_body(shifts, w_ref, nxt_ref, o_ref):
    j = pl.program_id(1)
    tn = o_ref.shape[1]
    for shift in sorted(set(shifts)):
        hit = functools.reduce(lambda p, q: p | q, [j == jt for jt, sh in enumerate(shifts) if sh == shift])

        @pl.when(hit)
        def _(shift=shift):
            if shift == 0:
                o_ref[...] = w_ref[...]
            else:
                o_ref[...] = jnp.concatenate([w_ref[:, shift:], nxt_ref[:, :shift]], axis=1)


def _repack(w16, cuts, tr=1024, tn=2048):
    d, n_src = w16.shape
    n_out = n_src - sum(w for _, w in cuts)
    assert n_out % tn == 0 and d % tr == 0
    shifts = tuple(sum(w for s, w in cuts if s <= jt * tn + sum(w2 for s2, w2 in cuts if s2 < s)) for jt in range(n_out // tn))
    assert max(shifts) < LANES
    assert n_out + shifts[-1] == n_src
    return pl.pallas_call(
        functools.partial(_repack_body, shifts),
        grid=(d // tr, n_out // tn),
        in_specs=[
            pl.BlockSpec((tr, tn), lambda i, j: (i, j)),
            pl.BlockSpec((tr, LANES), lambda i, j: (i, (j + 1) * (tn // LANES))),
        ],
        out_specs=pl.BlockSpec((tr, tn), lambda i, j: (i, j)),
        out_shape=jax.ShapeDtypeStruct((d, n_out), w16.dtype),
        compiler_params=_cparams(("parallel", "parallel")),
        name="w_in_repack",
    )(w16, w16)


ACT_NONE, ACT_SILU, ACT_SIGMOID = 0, 1, 2
_ACT_FN = {ACT_NONE: lambda v: v, ACT_SILU: _silu, ACT_SIGMOID: _sigmoid}


def _in_proj_body(tile_acts, x_ref, g_ref, wb_ref, ws_ref, z_ref, zs_ref, h_ref):
    j = pl.program_id(1)

    @pl.when(j == 0)
    def _():
        hb = _rms(x_ref[...], g_ref[...]).astype(BF16)
        h_ref[...] = hb
        zs_ref[...] = _dot(hb, ws_ref[...])

    for act, fn in _ACT_FN.items():
        tiles = [jt for jt, a in enumerate(tile_acts) if a == act]
        if not tiles:
            continue
        hit = functools.reduce(lambda p, q: p | q, [j == jt for jt in tiles])

        @pl.when(hit)
        def _(fn=fn):
            z_ref[...] = fn(_dot(h_ref[...], wb_ref[...])).astype(BF16)


def _in_proj(x2d, g_mix, w_big, w_small, segments, tm=1024, tn=2048):
    t, d = x2d.shape
    n = w_big.shape[1]
    assert all(lo % tn == 0 for lo, _ in segments) and n % tn == 0
    starts = [lo // tn for lo, _ in segments] + [n // tn]
    tile_acts = tuple(act for (_, act), a, b in zip(segments, starts, starts[1:]) for _ in range(b - a))
    body = functools.partial(_in_proj_body, tile_acts)
    return pl.pallas_call(
        body,
        grid=(t // tm, n // tn),
        in_specs=[
            pl.BlockSpec((tm, d), lambda i, j: (i, 0)),
            pl.BlockSpec((1, d), lambda i, j: (0, 0)),
            pl.BlockSpec((d, tn), lambda i, j: (0, j)),
            pl.BlockSpec((d, LANES), lambda i, j: (0, 0)),
        ],
        out_specs=[
            pl.BlockSpec((tm, tn), lambda i, j: (i, j)),
            pl.BlockSpec((tm, LANES), lambda i, j: (i, 0)),
        ],
        out_shape=[
            jax.ShapeDtypeStruct((t, n), BF16),
            jax.ShapeDtypeStruct((t, LANES), F32),
        ],
        scratch_shapes=[pltpu.VMEM((tm, d), BF16)],
        compiler_params=_cparams(("parallel", "arbitrary")),
        name="in_proj",
    )(x2d, g_mix, w_big, w_small)


def _gates_body(zs_ref, aneg_ref, dtb_ref, g_ref, gt_ref):
    zs = zs_ref[...]
    lane = lax.broadcasted_iota(jnp.int32, zs.shape, 1)
    row = lax.broadcasted_iota(jnp.int32, zs.shape, 0) % DN_CHUNK
    is_a = (lane >= SM_A) & (lane < SM_B)
    is_b = (lane >= SM_B) & (lane < SM_GL)
    g = jnp.where(is_a, aneg_ref[...] * _softplus(zs + dtb_ref[...]), 0.0)
    gcum = _chunk_cumsum(g, row, DN_CHUNK)
    gtot = gcum + _chunk_rev_cumsum(g, row, DN_CHUNK) - g
    beta = _sigmoid(zs)
    out = jnp.where(is_a, gcum, jnp.where(is_b, beta, 0.0))
    out = out + pltpu.roll(jnp.where(is_a, gtot, 0.0), SM_GL - SM_A, 1)
    g_ref[...] = out
    gt_ref[...] = out.T


def _gates(zs3, a_neg, dt_bias):
    b, s, _ = zs3.shape
    return pl.pallas_call(
        _gates_body,
        grid=(b,),
        in_specs=[
            pl.BlockSpec((None, s, LANES), lambda i: (i, 0, 0)),
            pl.BlockSpec((1, LANES), lambda i: (0, 0)),
            pl.BlockSpec((1, LANES), lambda i: (0, 0)),
        ],
        out_specs=[
            pl.BlockSpec((None, s, LANES), lambda i: (i, 0, 0)),
            pl.BlockSpec((None, LANES, s), lambda i: (i, 0, 0)),
        ],
        out_shape=[
            jax.ShapeDtypeStruct((b, s, LANES), F32),
            jax.ShapeDtypeStruct((b, LANES, s), F32),
        ],
        compiler_params=_cparams(("parallel",)),
        name="dn_gates",
    )(zs3, a_neg, dt_bias)


def _gla_body(hb, scale, q_ref, k_ref, v_ref, lr_ref, w2_ref, b_ref, nrm_ref, sg_ref, ga_ref, o_ref, st_ref):
    s = q_ref.shape[0]
    dk = q_ref.shape[1] // hb
    dv = v_ref.shape[1] // hb
    per = TILE // CHUNK
    heads = range(hb)
    kcol = [pl.ds(h * dk, dk) for h in heads]
    vcol = [pl.ds(h * dv, dv) for h in heads]

    @pl.when(pl.program_id(1) == 0)
    def _():
        st_ref[...] = jnp.zeros_like(st_ref)

    row = lax.broadcasted_iota(jnp.int32, (TILE, dk), 0) % CHUNK
    ri = lax.broadcasted_iota(jnp.int32, (TILE, TILE), 0)
    ci = lax.broadcasted_iota(jnp.int32, (TILE, TILE), 1)
    causal = ((ri // CHUNK) == (ci // CHUNK)) & (ci <= ri)
    cross = ((ri // CHUNK) == (ci // CHUNK) + 1) & ((ri // (2 * CHUNK)) == (ci // (2 * CHUNK)))

    def tile(t, carry):
        base = pl.multiple_of(t * TILE, TILE)
        r = pl.ds(base, TILE)
        lrb = lr_ref[r, :].astype(BF16)
        q = [q_ref[r, kcol[h]].astype(F32) * scale for h in heads]
        k = [k_ref[r, kcol[h]].astype(F32) for h in heads]
        v = [v_ref[r, vcol[h]] for h in heads]
        pre = [_dot(lrb, w2_ref[:, kcol[h]]) + b_ref[:, kcol[h]] for h in heads]
        bc = [_chunk_cumsum(_log_sigmoid(x) * (1.0 / GLA_TAU), row) for x in pre]
        q_dec = [q[h] * jnp.exp(bc[h]) for h in heads]
        q_in = [x.astype(BF16) for x in q_dec]
        k_in = [(k[h] * jnp.exp(-bc[h])).astype(BF16) for h in heads]
        bl = [jnp.concatenate([jnp.broadcast_to(bc[h][(c + 1) * CHUNK - 1:(c + 1) * CHUNK, :], (CHUNK, dk))
                               for c in range(per)], axis=0) for h in heads]
        k_dec = [k[h] * jnp.exp(bl[h] - bc[h]) for h in heads]
        a = [(jnp.where(causal, _dot_nt(q_in[h], k_in[h]), 0.0)
              + jnp.where(cross, _dot_nt(q_in[h], k_dec[h].astype(BF16)), 0.0)).astype(BF16) for h in heads]
        o_intra = [_dot(a[h], v[h]) for h in heads]
        for m in range(per // 2):
            p0, p1, p2 = 2 * m * CHUNK, (2 * m + 1) * CHUNK, (2 * m + 2) * CHUNK
            f1 = [jnp.exp(bc[h][p1 - 1:p1, :]) for h in heads]
            f2 = [jnp.exp(bc[h][p2 - 1:p2, :]) for h in heads]
            qs = [jnp.concatenate([q_in[h][p0:p1, :], (q_dec[h][p1:p2, :] * f1[h]).astype(BF16)], axis=0)
                  for h in heads]
            kd = [jnp.concatenate([(k_dec[h][p0:p1, :] * f2[h]).astype(BF16), k_dec[h][p1:p2, :].astype(BF16)],
                                  axis=0) for h in heads]
            st = [st_ref[h] for h in heads]
            o = [o_intra[h][p0:p2, :] + _dot_nt(qs[h], st[h].astype(BF16)) for h in heads]
            for h in heads:
                st_ref[h] = st[h] * (f1[h] * f2[h]) + _dot_tn(v[h][p0:p2, :], kd[h])
            rc = pl.ds(base + p0, 2 * CHUNK)
            for h in heads:
                gate = sg_ref[rc, vcol[h]].astype(F32) * ga_ref[rc, vcol[h]].astype(F32)
                o_ref[rc, vcol[h]] = (_rms(o[h], nrm_ref[...]) * gate).astype(o_ref.dtype)
        return carry

    lax.fori_loop(0, s // TILE, tile, 0)


def _gla(z3, zs3, w2p, gla_b, gla_norm, d_model, gate_cols, ts=512):
    hb = GLA_HEADS
    b, s, _ = z3.shape
    dk = d_model // (2 * GLA_HEADS)
    dv = d_model // GLA_HEADS
    assert s % ts == 0 and ts % TILE == 0
    qk_cols = GLA_HEADS * dk
    kw, vw = hb * dk, hb * dv
    sg_col, ga_col = gate_cols
    assert sg_col % vw == 0 and ga_col % vw == 0
    body = functools.partial(_gla_body, hb, dk ** -0.5)
    return pl.pallas_call(
        body,
        grid=(b, s // ts),
        in_specs=[
            pl.BlockSpec((None, ts, kw), lambda i, t: (i, t, 0)),
            pl.BlockSpec((None, ts, kw), lambda i, t: (i, t, qk_cols // kw)),
            pl.BlockSpec((None, ts, vw), lambda i, t: (i, t, 2 * qk_cols // vw)),
            pl.BlockSpec((None, ts, LANES), lambda i, t: (i, t, 0)),
            pl.BlockSpec((LANES, kw), lambda i, t: (0, 0)),
            pl.BlockSpec((1, kw), lambda i, t: (0, 0)),
            pl.BlockSpec((1, dv), lambda i, t: (0, 0)),
            pl.BlockSpec((None, ts, vw), lambda i, t: (i, t, sg_col // vw)),
            pl.BlockSpec((None, ts, vw), lambda i, t: (i, t, ga_col // vw)),
        ],
        out_specs=pl.BlockSpec((None, ts, vw), lambda i, t: (i, t, 0)),
        out_shape=jax.ShapeDtypeStruct((b, s, GLA_HEADS * dv), BF16),
        scratch_shapes=[pltpu.VMEM((hb, dv, dk), F32)],
        compiler_params=_cparams(("parallel", "arbitrary")),
        name="gla_mixer",
    )(z3, z3, z3, zs3, w2p, gla_b, gla_norm, z3, z3)


def _conv_silu(x_ref, w_ref, cols, base, n_rows, first, buf):
    pbase = pl.multiple_of(jnp.maximum(base - 8, 0), 8)
    buf[pl.ds(0, 8), :] = jnp.where(first, 0.0, x_ref[pl.ds(pbase, 8), cols].astype(F32))
    buf[pl.ds(8, n_rows), :] = x_ref[pl.ds(base, n_rows), cols].astype(F32)
    w = 0.5 * w_ref[:, cols]
    acc = None
    for d in range(DN_CONV):
        term = buf[pl.ds(8 - d, n_rows), :] * w[DN_CONV - 1 - d:DN_CONV - d, :]
        acc = term if acc is None else acc + term
    return _silu_half(acc)


def _l2n(t):
    return t * lax.rsqrt(jnp.sum(t * t, axis=-1, keepdims=True) + EPS)


def _lane_pick(x, idx):
    lane = lax.broadcasted_iota(jnp.int32, x.shape, 1)
    return jnp.sum(jnp.where(lane == idx, x, 0.0), axis=-1, keepdims=True)


def _bdot(a, b):
    return lax.dot_general(a, b, (((2,), (1,)), ((0,), (0,))), preferred_element_type=F32)


def _bdot_nt(a, b):
    return lax.dot_general(a, b, (((2,), (2,)), ((0,), (0,))), preferred_element_type=F32)


def _dn_body(hb, qscale, q_ref, k_ref, v_ref, cq_ref, ck_ref, cv_ref, g_ref, gt_ref, nrm_ref,
             o_ref, kp_s, np_s, qp_s, op_s, fl_s, st_s, cbuf):
    s = q_ref.shape[0]
    n_half = DN_TILE // HALF
    per_tile = DN_TILE // DN_CHUNK
    head0 = pl.program_id(1) * hb

    ri = lax.broadcasted_iota(jnp.int32, (HALF, HALF), 0)
    ci = lax.broadcasted_iota(jnp.int32, (HALF, HALF), 1)
    incl = ci <= ri
    sub = (ri // SUB) == (ci // SUB)
    strict_in = sub & (ci < ri)
    strict_off = (~sub) & (ci < ri)
    n_sq = SUB.bit_length() - 2
    b3 = lambda x: x.reshape(n_half, HALF, x.shape[-1])

    n_tiles = s // DN_TILE
    n_units = hb * n_tiles

    def unit_coords(u):
        u = jnp.asarray(u, jnp.int32)
        i = lax.div(u, n_tiles)
        t = lax.rem(u, n_tiles)
        return i, t, pl.ds(pl.multiple_of(i * LANES, LANES), LANES)

    def prep(u):
        i, t, cols = unit_coords(u)
        head = head0 + i
        base = pl.multiple_of(t * DN_TILE, DN_TILE)
        rows = pl.ds(base, DN_TILE)
        first = t == 0
        q = _l2n(_conv_silu(q_ref, cq_ref, cols, base, DN_TILE, first, cbuf.at[0])) * qscale
        k = _l2n(_conv_silu(k_ref, ck_ref, cols, base, DN_TILE, first, cbuf.at[1]))
        v = _conv_silu(v_ref, cv_ref, cols, base, DN_TILE, first, cbuf.at[2])
        gs = g_ref[rows, :]
        gc = _lane_pick(gs, SM_A + head)
        bt = _lane_pick(gs, SM_B + head)
        gl = _lane_pick(gs, SM_GL + head)
        eg = jnp.exp(gc)
        kb = k * bt
        vb = v * bt
        kbe = kb * eg
        qd = q * eg
        kd = (k * jnp.exp(gl - gc)).astype(BF16)
        flb = jnp.broadcast_to(jnp.exp(gl), (DN_TILE, LANES))
        grow_t = gt_ref[pl.ds(SM_A + head, 1), rows]
        grow = jnp.stack([grow_t[:, j * HALF:(j + 1) * HALF] for j in range(n_half)], axis=0)
        dec = jnp.exp(jnp.where(incl, b3(gc) - grow, -jnp.inf))
        kbf = b3(k).astype(BF16)
        a_full = _bdot_nt(b3(kb).astype(BF16), kbf) * dec
        a = jnp.where(strict_in, a_full, 0.0)
        attn = (_bdot_nt(b3(q).astype(BF16), kbf) * dec).astype(BF16)
        p = -a
        r = p
        for _ in range(n_sq):
            pb = p.astype(BF16)
            p = _bdot(pb, pb)
            r = r + p + _bdot(r.astype(BF16), p.astype(BF16))
        nb = jnp.where(strict_off, a_full, 0.0)
        rb = r.astype(BF16)
        tn = nb + _bdot(rb, nb.astype(BF16))
        r = r - (tn + _bdot(tn.astype(BF16), rb))
        rhs = jnp.concatenate([b3(vb), b3(kbe)], axis=2)
        uw = (rhs + _bdot(r.astype(BF16), rhs.astype(BF16))).astype(BF16)
        aw = _bdot(attn, uw)
        op_s[rows, cols] = aw[:, :, :LANES].reshape(DN_TILE, LANES).astype(op_s.dtype)
        qp_s[rows, cols] = (qd - aw[:, :, LANES:].reshape(DN_TILE, LANES)).astype(qp_s.dtype)
        uw2 = uw.reshape(DN_TILE, 2 * LANES)
        for c in range(per_tile):
            c0 = c * DN_CHUNK
            kn = _dot_tn(kd[c0:c0 + DN_CHUNK, :], uw2[c0:c0 + DN_CHUNK, :])
            cidx = t * per_tile + c
            np_s[i, cidx] = kn[:, :LANES].astype(np_s.dtype)
            kp_s[i, cidx] = kn[:, LANES:].astype(kp_s.dtype)
            fl_s[i, pl.ds(cidx, 1), :] = flb[c0:c0 + 1, :]

    def prep_step(u, carry):
        prep(u)
        return carry

    lax.fori_loop(0, n_units, prep_step, 0)

    st_s[...] = jnp.zeros_like(st_s)

    def scan(c, carry):
        r = pl.ds(pl.multiple_of(c * DN_CHUNK, DN_CHUNK), DN_CHUNK)
        for i in range(hb):
            cols = pl.ds(i * LANES, LANES)
            st = st_s[i]
            stb = st.astype(BF16)
            o = _dot(qp_s[r, cols], stb) + op_s[r, cols].astype(F32)
            st_s[i] = st * fl_s[i, pl.ds(c, 1), :] - _dot(kp_s[i, c], stb) + np_s[i, c]
            o_ref[r, cols] = _rms(o, nrm_ref[...]).astype(o_ref.dtype)
        return carry

    lax.fori_loop(0, s // DN_CHUNK, scan, 0, unroll=SCAN_UNROLL)


def _deltanet(z3, dn_conv, g3, gt3, dn_norm, d_model, qkv_off, hb=4):
    b, s, _ = z3.shape
    dh = d_model // DN_HEADS
    assert dh == LANES and s % DN_TILE == 0
    bw = hb * dh
    nblk = d_model // bw
    n_chunks = s // DN_CHUNK
    body = functools.partial(_dn_body, hb, dh ** -0.5)
    z_spec = lambda part: pl.BlockSpec(
        (None, s, bw), lambda i, h: (i, 0, (qkv_off + part * d_model) // bw + h))
    c_spec = lambda part: pl.BlockSpec((DN_CONV, bw), lambda i, h: (0, part * nblk + h))
    return pl.pallas_call(
        body,
        grid=(b, nblk),
        in_specs=[
            z_spec(0), z_spec(1), z_spec(2),
            c_spec(0), c_spec(1), c_spec(2),
            pl.BlockSpec((None, s, LANES), lambda i, h: (i, 0, 0)),
            pl.BlockSpec((None, LANES, s), lambda i, h: (i, 0, 0)),
            pl.BlockSpec((1, dh), lambda i, h: (0, 0)),
        ],
        out_specs=pl.BlockSpec((None, s, bw), lambda i, h: (i, 0, h)),
        out_shape=jax.ShapeDtypeStruct((b, s, d_model), BF16),
        scratch_shapes=[
            pltpu.VMEM((hb, n_chunks, dh, dh), BF16),
            pltpu.VMEM((hb, n_chunks, dh, dh), F32),
            pltpu.VMEM((s, bw), BF16),
            pltpu.VMEM((s, bw), BF16),
            pltpu.VMEM((hb, n_chunks, LANES), F32),
            pltpu.VMEM((hb, dh, dh), F32),
            pltpu.VMEM((3, DN_TILE + 8, LANES), F32),
        ],
        compiler_params=_cparams(("parallel", "parallel")),
        name="deltanet_mixer",
    )(z3, z3, z3, dn_conv, dn_conv, dn_conv, g3, gt3, dn_norm)


def _merge_body(og_ref, od_ref, sz_ref, gb_ref, x_ref, w_ref, g_ref, x1_ref, h2_ref):
    f = lambda r: r[...].astype(F32)
    mixed = f(og_ref) + f(gb_ref) * (f(od_ref) * f(sz_ref))
    x1 = x_ref[...] + _dot(mixed.astype(BF16), w_ref[...])
    x1_ref[...] = x1
    h2_ref[...] = _rms(x1, g_ref[...]).astype(BF16)


def _merge(o_gla, o_dn, z2d, act_cols, x2d, w_out, g_mlp, tm=512):
    t, d = x2d.shape
    assert all(c % d == 0 for c in act_cols)
    row = lambda c: pl.BlockSpec((tm, d), lambda i, c=c: (i, c))
    sz, gb = (c // d for c in act_cols)
    return pl.pallas_call(
        _merge_body,
        grid=(t // tm,),
        in_specs=[
            row(0), row(0), row(sz), row(gb), row(0),
            pl.BlockSpec((d, d), lambda i: (0, 0), pipeline_mode=pl.Buffered(1)),
            pl.BlockSpec((1, d), lambda i: (0, 0)),
        ],
        out_specs=[row(0), row(0)],
        out_shape=[jax.ShapeDtypeStruct((t, d), F32), jax.ShapeDtypeStruct((t, d), BF16)],
        compiler_params=_cparams(("parallel",)),
        name="merge_out_proj",
    )(o_gla, o_dn, z2d, z2d, x2d, w_out, g_mlp)


def _mlp_body(h_ref, wu_ref, wd_ref, x_ref, o_ref):
    @pl.when(pl.program_id(1) == 0)
    def _():
        o_ref[...] = x_ref[...]

    mid = jnp.square(jnp.maximum(_dot(h_ref[...], wu_ref[...]), 0.0)).astype(BF16)
    o_ref[...] += _dot(mid, wd_ref[...])


def _mlp(h2, w_up, w_down, x1, tm=512, tf=1024):
    t, d = x1.shape
    ff = w_up.shape[1]
    return pl.pallas_call(
        _mlp_body,
        grid=(t // tm, ff // tf),
        in_specs=[
            pl.BlockSpec((tm, d), lambda i, j: (i, 0)),
            pl.BlockSpec((d, tf), lambda i, j: (0, j)),
            pl.BlockSpec((tf, d), lambda i, j: (j, 0)),
            pl.BlockSpec((tm, d), lambda i, j: (i, 0)),
        ],
        out_specs=pl.BlockSpec((tm, d), lambda i, j: (i, 0)),
        out_shape=jax.ShapeDtypeStruct((t, d), F32),
        compiler_params=_cparams(("parallel", "arbitrary")),
        name="relu2_mlp",
    )(h2, w_up, w_down, x1)


def _ple_body(x_ref, p_ref, wg_ref, wp_ref, gp_ref, gf_ref, o_ref):
    x2 = x_ref[...]
    h3 = _rms(x2, gp_ref[...]).astype(BF16)
    gate = _sigmoid(_dot(h3, wg_ref[...]))
    proj = _dot(p_ref[...].astype(BF16), wp_ref[...])
    o_ref[...] = _rms(x2 + gate * proj, gf_ref[...])


def _ple(x2, p2d, w_gate, w_proj, g_ple, g_final, tm=512):
    t, d = x2.shape
    pd = p2d.shape[1]
    return pl.pallas_call(
        _ple_body,
        grid=(t // tm,),
        in_specs=[
            pl.BlockSpec((tm, d), lambda i: (i, 0)),
            pl.BlockSpec((tm, pd), lambda i: (i, 0)),
            pl.BlockSpec((d, d), lambda i: (0, 0)),
            pl.BlockSpec((pd, d), lambda i: (0, 0)),
            pl.BlockSpec((1, d), lambda i: (0, 0)),
            pl.BlockSpec((1, d), lambda i: (0, 0)),
        ],
        out_specs=pl.BlockSpec((tm, d), lambda i: (i, 0)),
        out_shape=jax.ShapeDtypeStruct((t, d), F32),
        compiler_params=_cparams(("parallel",)),
        name="ple_final_norm",
    )(x2, p2d, w_gate, w_proj, g_ple, g_final)


def _layer(x, p_i, g_mix, w_in, gla_w2, gla_b, gla_norm, dn_conv, dn_a_log, dn_dt_bias, dn_norm,
           w_out, g_mlp, w_up, w_down, g_ple, w_ple_gate, w_ple_proj, g_out):
    b, s, d = x.shape
    t = b * s
    gla_qk = d // 2
    dn_qkv = 3 * d
    o_q, o_k = 0, gla_qk
    o_v = 2 * gla_qk
    o_g = o_v + d
    o_lr = o_g + d
    o_dn = o_lr + GLA_LOWRANK
    o_z = o_dn + dn_qkv
    o_a = o_z + d
    o_b = o_a + DN_HEADS
    o_ga = o_b + DN_HEADS
    o_gb = o_ga + d
    assert (o_q, o_k, o_v) == (0, gla_qk, 2 * gla_qk)
    w_in16 = w_in.astype(BF16)
    cs = lambda lo, n: w_in16[:, lo:lo + n]
    w_big = _repack(w_in16, ((o_lr, GLA_LOWRANK), (o_a, 2 * DN_HEADS)))
    w_small = jnp.concatenate(
        [cs(o_lr, GLA_LOWRANK), cs(o_a, DN_HEADS), cs(o_b, DN_HEADS),
         jnp.zeros((d, LANES - SM_GL), BF16)], axis=1)
    n_g, n_dn, n_z = o_g, o_lr, o_lr + dn_qkv
    n_ga = n_z + d
    n_gb = n_ga + d
    segments = ((0, ACT_NONE), (n_g, ACT_SILU), (n_dn, ACT_NONE), (n_z, ACT_SILU), (n_ga, ACT_SIGMOID))
    qkv_off = n_dn

    x2d = x.reshape(t, d)
    z2d, zs2d = _in_proj(x2d, g_mix.reshape(1, d), w_big, w_small, segments)
    z3 = z2d.reshape(b, s, -1)
    zs3 = zs2d.reshape(b, s, LANES)

    pad = lambda v: jnp.zeros((1, LANES), F32).at[0, SM_A:SM_B].set(v.astype(F32))
    g3, gt3 = _gates(zs3, pad(-jnp.exp(dn_a_log.astype(F32))), pad(dn_dt_bias))

    w2p = jnp.zeros((LANES, gla_qk), F32).at[:GLA_LOWRANK].set(gla_w2).astype(BF16)
    o_gla = _gla(z3, zs3, w2p, gla_b.reshape(1, -1).astype(F32), gla_norm.reshape(1, -1).astype(F32), d,
                 (n_g, n_ga))
    o_dn = _deltanet(z3, dn_conv.astype(F32), g3, gt3, dn_norm.reshape(1, -1).astype(F32), d, qkv_off)

    x1, h2 = _merge(o_gla.reshape(t, d), o_dn.reshape(t, d), z2d, (n_z, n_gb), x2d,
                    w_out.astype(BF16), g_mlp.reshape(1, d))
    x2 = _mlp(h2, w_up.astype(BF16), w_down.astype(BF16), x1)
    out = _ple(x2, p_i.reshape(t, -1), w_ple_gate.astype(BF16), w_ple_proj.astype(BF16),
               g_ple.reshape(1, d), g_out.reshape(1, d))
    return out.reshape(b, s, d)


def kernel(x, p, g_mix, w_in, gla_w2, gla_b, gla_norm, dn_conv, dn_a_log, dn_dt_bias, dn_norm,
           w_out, g_mlp, w_up, w_down, g_ple, w_ple_gate, w_ple_proj, g_final):
    depth = w_in.shape[0]
    assert depth == 1, "the final rms_norm is fused into the last layer's kernel"
    return _layer(x, p[0], g_mix[0], w_in[0], gla_w2[0], gla_b[0], gla_norm[0], dn_conv[0],
                  dn_a_log[0], dn_dt_bias[0], dn_norm[0], w_out[0], g_mlp[0], w_up[0], w_down[0],
                  g_ple[0], w_ple_gate[0], w_ple_proj[0], g_final)
```

```python
import functools

import jax
import jax.numpy as jnp
from jax import lax
from jax.experimental import pallas as pl
from jax.experimental.pallas import tpu as pltpu

F32 = jnp.float32
BF16 = jnp.bfloat16

EPS = 1e-6
CHUNK = 64
GLA_HEADS = 4
GLA_LOWRANK = 16
GLA_TAU = 16.0
DN_HEADS = 16
DN_CONV = 4
LANES = 128
TILE = 256
DN_CHUNK = 128
HALF = DN_CHUNK
DN_TILE = 2048
SUB = 64
SCAN_UNROLL = 8
VMEM_LIMIT = 56 * 1024 * 1024

SM_A = GLA_LOWRANK
SM_B = SM_A + DN_HEADS
SM_GL = SM_B + DN_HEADS


def _cparams(sem):
    return pltpu.CompilerParams(dimension_semantics=sem, vmem_limit_bytes=VMEM_LIMIT)


def _dot(a, b):
    return jnp.dot(a, b, preferred_element_type=F32)


def _dot_nt(a, b):
    return lax.dot_general(a, b, (((1,), (1,)), ((), ())), preferred_element_type=F32)


def _dot_tn(a, b):
    return lax.dot_general(a, b, (((0,), (0,)), ((), ())), preferred_element_type=F32)


def _sigmoid(x):
    return 0.5 * jnp.tanh(0.5 * x) + 0.5


def _silu_half(h):
    return h * jnp.tanh(h) + h


def _silu(x):
    return _silu_half(0.5 * x)


def _log_sigmoid(x):
    return jnp.minimum(x, 0.0) - jnp.log(1.0 + jnp.exp(-jnp.abs(x)))


def _softplus(x):
    return jnp.maximum(x, 0.0) + jnp.log(1.0 + jnp.exp(-jnp.abs(x)))


def _rms(x, g):
    return x * lax.rsqrt(jnp.mean(x * x, axis=-1, keepdims=True) + EPS) * g


def _chunk_cumsum(x, row, chunk=CHUNK):
    shift = 1
    while shift < chunk:
        x = x + jnp.where(row >= shift, pltpu.roll(x, shift, 0), 0.0)
        shift *= 2
    return x


def _chunk_rev_cumsum(x, row, chunk=CHUNK):
    n = x.shape[0]
    shift = 1
    while shift < chunk:
        x = x + jnp.where(row < chunk - shift, pltpu.roll(x, n - shift, 0), 0.0)
        shift *= 2
    return x


def _repack_body(shifts, w_ref, nxt_ref, o_ref):
    j = pl.program_id(1)
    tn = o_ref.shape[1]
    for shift in sorted(set(shifts)):
        hit = functools.reduce(lambda p, q: p | q, [j == jt for jt, sh in enumerate(shifts) if sh == shift])

        @pl.when(hit)
        def _(shift=shift):
            if shift == 0:
                o_ref[...] = w_ref[...]
            else:
                o_ref[...] = jnp.concatenate([w_ref[:, shift:], nxt_ref[:, :shift]], axis=1)


def _repack(w16, cuts, tr=1024, tn=2048):
    d, n_src = w16.shape
    n_out = n_src - sum(w for _, w in cuts)
    assert n_out % tn == 0 and d % tr == 0
    shifts = tuple(sum(w for s, w in cuts if s <= jt * tn + sum(w2 for s2, w2 in cuts if s2 < s)) for jt in range(n_out // tn))
    assert max(shifts) < LANES
    assert n_out + shifts[-1] == n_src
    return pl.pallas_call(
        functools.partial(_repack_body, shifts),
        grid=(d // tr, n_out // tn),
        in_specs=[
            pl.BlockSpec((tr, tn), lambda i, j: (i, j)),
            pl.BlockSpec((tr, LANES), lambda i, j: (i, (j + 1) * (tn // LANES))),
        ],
        out_specs=pl.BlockSpec((tr, tn), lambda i, j: (i, j)),
        out_shape=jax.ShapeDtypeStruct((d, n_out), w16.dtype),
        compiler_params=_cparams(("parallel", "parallel")),
        name="w_in_repack",
    )(w16, w16)


ACT_NONE, ACT_SILU, ACT_SIGMOID = 0, 1, 2
_ACT_FN = {ACT_NONE: lambda v: v, ACT_SILU: _silu, ACT_SIGMOID: _sigmoid}


def _in_proj_body(tile_acts, x_ref, g_ref, wb_ref, ws_ref, z_ref, zs_ref, h_ref):
    j = pl.program_id(1)

    @pl.when(j == 0)
    def _():
        hb = _rms(x_ref[...], g_ref[...]).astype(BF16)
        h_ref[...] = hb
        zs_ref[...] = _dot(hb, ws_ref[...])

    for act, fn in _ACT_FN.items():
        tiles = [jt for jt, a in enumerate(tile_acts) if a == act]
        if not tiles:
            continue
        hit = functools.reduce(lambda p, q: p | q, [j == jt for jt in tiles])

        @pl.when(hit)
        def _(fn=fn):
            z_ref[...] = fn(_dot(h_ref[...], wb_ref[...])).astype(BF16)


def _in_proj(x2d, g_mix, w_big, w_small, segments, tm=1024, tn=2048):
    t, d = x2d.shape
    n = w_big.shape[1]
    assert all(lo % tn == 0 for lo, _ in segments) and n % tn == 0
    starts = [lo // tn for lo, _ in segments] + [n // tn]
    tile_acts = tuple(act for (_, act), a, b in zip(segments, starts, starts[1:]) for _ in range(b - a))
    body = functools.partial(_in_proj_body, tile_acts)
    return pl.pallas_call(
        body,
        grid=(t // tm, n // tn),
        in_specs=[
            pl.BlockSpec((tm, d), lambda i, j: (i, 0)),
            pl.BlockSpec((1, d), lambda i, j: (0, 0)),
            pl.BlockSpec((d, tn), lambda i, j: (0, j)),
            pl.BlockSpec((d, LANES), lambda i, j: (0, 0)),
        ],
        out_specs=[
            pl.BlockSpec((tm, tn), lambda i, j: (i, j)),
            pl.BlockSpec((tm, LANES), lambda i, j: (i, 0)),
        ],
        out_shape=[
            jax.ShapeDtypeStruct((t, n), BF16),
            jax.ShapeDtypeStruct((t, LANES), F32),
        ],
        scratch_shapes=[pltpu.VMEM((tm, d), BF16)],
        compiler_params=_cparams(("parallel", "arbitrary")),
        name="in_proj",
    )(x2d, g_mix, w_big, w_small)


def _gates_body(zs_ref, aneg_ref, dtb_ref, g_ref, gt_ref):
    zs = zs_ref[...]
    lane = lax.broadcasted_iota(jnp.int32, zs.shape, 1)
    row = lax.broadcasted_iota(jnp.int32, zs.shape, 0) % DN_CHUNK
    is_a = (lane >= SM_A) & (lane < SM_B)
    is_b = (lane >= SM_B) & (lane < SM_GL)
    g = jnp.where(is_a, aneg_ref[...] * _softplus(zs + dtb_ref[...]), 0.0)
    gcum = _chunk_cumsum(g, row, DN_CHUNK)
    gtot = gcum + _chunk_rev_cumsum(g, row, DN_CHUNK) - g
    beta = _sigmoid(zs)
    out = jnp.where(is_a, gcum, jnp.where(is_b, beta, 0.0))
    out = out + pltpu.roll(jnp.where(is_a, gtot, 0.0), SM_GL - SM_A, 1)
    g_ref[...] = out
    gt_ref[...] = out.T


def _gates(zs3, a_neg, dt_bias):
    b, s, _ = zs3.shape
    return pl.pallas_call(
        _gates_body,
        grid=(b,),
        in_specs=[
            pl.BlockSpec((None, s, LANES), lambda i: (i, 0, 0)),
            pl.BlockSpec((1, LANES), lambda i: (0, 0)),
            pl.BlockSpec((1, LANES), lambda i: (0, 0)),
        ],
        out_specs=[
            pl.BlockSpec((None, s, LANES), lambda i: (i, 0, 0)),
            pl.BlockSpec((None, LANES, s), lambda i: (i, 0, 0)),
        ],
        out_shape=[
            jax.ShapeDtypeStruct((b, s, LANES), F32),
            jax.ShapeDtypeStruct((b, LANES, s), F32),
        ],
        compiler_params=_cparams(("parallel",)),
        name="dn_gates",
    )(zs3, a_neg, dt_bias)


def _gla_body(hb, scale, q_ref, k_ref, v_ref, lr_ref, w2_ref, b_ref, nrm_ref, sg_ref, ga_ref, o_ref, st_ref):
    s = q_ref.shape[0]
    dk = q_ref.shape[1] // hb
    dv = v_ref.shape[1] // hb
    per = TILE // CHUNK
    heads = range(hb)
    kcol = [pl.ds(h * dk, dk) for h in heads]
    vcol = [pl.ds(h * dv, dv) for h in heads]

    @pl.when(pl.program_id(1) == 0)
    def _():
        st_ref[...] = jnp.zeros_like(st_ref)

    row = lax.broadcasted_iota(jnp.int32, (TILE, dk), 0) % CHUNK
    ri = lax.broadcasted_iota(jnp.int32, (TILE, TILE), 0)
    ci = lax.broadcasted_iota(jnp.int32, (TILE, TILE), 1)
    causal = ((ri // CHUNK) == (ci // CHUNK)) & (ci <= ri)
    cross = ((ri // CHUNK) == (ci // CHUNK) + 1) & ((ri // (2 * CHUNK)) == (ci // (2 * CHUNK)))

    def tile(t, carry):
        base = pl.multiple_of(t * TILE, TILE)
        r = pl.ds(base, TILE)
        lrb = lr_ref[r, :].astype(BF16)
        q = [q_ref[r, kcol[h]].astype(F32) * scale for h in heads]
        k = [k_ref[r, kcol[h]].astype(F32) for h in heads]
        v = [v_ref[r, vcol[h]] for h in heads]
        pre = [_dot(lrb, w2_ref[:, kcol[h]]) + b_ref[:, kcol[h]] for h in heads]
        bc = [_chunk_cumsum(_log_sigmoid(x) * (1.0 / GLA_TAU), row) for x in pre]
        q_dec = [q[h] * jnp.exp(bc[h]) for h in heads]
        q_in = [x.astype(BF16) for x in q_dec]
        k_in = [(k[h] * jnp.exp(-bc[h])).astype(BF16) for h in heads]
        bl = [jnp.concatenate([jnp.broadcast_to(bc[h][(c + 1) * CHUNK - 1:(c + 1) * CHUNK, :], (CHUNK, dk))
                               for c in range(per)], axis=0) for h in heads]
        k_dec = [k[h] * jnp.exp(bl[h] - bc[h]) for h in heads]
        a = [(jnp.where(causal, _dot_nt(q_in[h], k_in[h]), 0.0)
              + jnp.where(cross, _dot_nt(q_in[h], k_dec[h].astype(BF16)), 0.0)).astype(BF16) for h in heads]
        o_intra = [_dot(a[h], v[h]) for h in heads]
        for m in range(per // 2):
            p0, p1, p2 = 2 * m * CHUNK, (2 * m + 1) * CHUNK, (2 * m + 2) * CHUNK
            f1 = [jnp.exp(bc[h][p1 - 1:p1, :]) for h in heads]
            f2 = [jnp.exp(bc[h][p2 - 1:p2, :]) for h in heads]
            qs = [jnp.concatenate([q_in[h][p0:p1, :], (q_dec[h][p1:p2, :] * f1[h]).astype(BF16)], axis=0)
                  for h in heads]
            kd = [jnp.concatenate([(k_dec[h][p0:p1, :] * f2[h]).astype(BF16), k_dec[h][p1:p2, :].astype(BF16)],
                                  axis=0) for h in heads]
            st = [st_ref[h] for h in heads]
            o = [o_intra[h][p0:p2, :] + _dot_nt(qs[h], st[h].astype(BF16)) for h in heads]
            for h in heads:
                st_ref[h] = st[h] * (f1[h] * f2[h]) + _dot_tn(v[h][p0:p2, :], kd[h])
            rc = pl.ds(base + p0, 2 * CHUNK)
            for h in heads:
                gate = sg_ref[rc, vcol[h]].astype(F32) * ga_ref[rc, vcol[h]].astype(F32)
                o_ref[rc, vcol[h]] = (_rms(o[h], nrm_ref[...]) * gate).astype(o_ref.dtype)
        return carry

    lax.fori_loop(0, s // TILE, tile, 0)


def _gla(z3, zs3, w2p, gla_b, gla_norm, d_model, gate_cols, ts=512):
    hb = GLA_HEADS
    b, s, _ = z3.shape
    dk = d_model // (2 * GLA_HEADS)
    dv = d_model // GLA_HEADS
    assert s % ts == 0 and ts % TILE == 0
    qk_cols = GLA_HEADS * dk
    kw, vw = hb * dk, hb * dv
    sg_col, ga_col = gate_cols
    assert sg_col % vw == 0 and ga_col % vw == 0
    body = functools.partial(_gla_body, hb, dk ** -0.5)
    return pl.pallas_call(
        body,
        grid=(b, s // ts),
        in_specs=[
            pl.BlockSpec((None, ts, kw), lambda i, t: (i, t, 0)),
            pl.BlockSpec((None, ts, kw), lambda i, t: (i, t, qk_cols // kw)),
            pl.BlockSpec((None, ts, vw), lambda i, t: (i, t, 2 * qk_cols // vw)),
            pl.BlockSpec((None, ts, LANES), lambda i, t: (i, t, 0)),
            pl.BlockSpec((LANES, kw), lambda i, t: (0, 0)),
            pl.BlockSpec((1, kw), lambda i, t: (0, 0)),
            pl.BlockSpec((1, dv), lambda i, t: (0, 0)),
            pl.BlockSpec((None, ts, vw), lambda i, t: (i, t, sg_col // vw)),
            pl.BlockSpec((None, ts, vw), lambda i, t: (i, t, ga_col // vw)),
        ],
        out_specs=pl.BlockSpec((None, ts, vw), lambda i, t: (i, t, 0)),
        out_shape=jax.ShapeDtypeStruct((b, s, GLA_HEADS * dv), BF16),
        scratch_shapes=[pltpu.VMEM((hb, dv, dk), F32)],
        compiler_params=_cparams(("parallel", "arbitrary")),
        name="gla_mixer",
    )(z3, z3, z3, zs3, w2p, gla_b, gla_norm, z3, z3)


def _conv_silu(x_ref, w_ref, cols, base, n_rows, first, buf):
    pbase = pl.multiple_of(jnp.maximum(base - 8, 0), 8)
    buf[pl.ds(0, 8), :] = jnp.where(first, 0.0, x_ref[pl.ds(pbase, 8), cols].astype(F32))
    buf[pl.ds(8, n_rows), :] = x_ref[pl.ds(base, n_rows), cols].astype(F32)
    w = 0.5 * w_ref[:, cols]
    acc = None
    for d in range(DN_CONV):
        term = buf[pl.ds(8 - d, n_rows), :] * w[DN_CONV - 1 - d:DN_CONV - d, :]
        acc = term if acc is None else acc + term
    return _silu_half(acc)


def _l2n(t):
    return t * lax.rsqrt(jnp.sum(t * t, axis=-1, keepdims=True) + EPS)


def _lane_pick(x, idx):
    lane = lax.broadcasted_iota(jnp.int32, x.shape, 1)
    return jnp.sum(jnp.where(lane == idx, x, 0.0), axis=-1, keepdims=True)


def _bdot(a, b):
    return lax.dot_general(a, b, (((2,), (1,)), ((0,), (0,))), preferred_element_type=F32)


def _bdot_nt(a, b):
    return lax.dot_general(a, b, (((2,), (2,)), ((0,), (0,))), preferred_element_type=F32)


def _dn_body(hb, qscale, q_ref, k_ref, v_ref, cq_ref, ck_ref, cv_ref, g_ref, gt_ref, nrm_ref, sz_ref, gb_ref,
             o_ref, kp_s, np_s, qp_s, op_s, fl_s, st_s, cbuf):
    s = q_ref.shape[0]
    n_half = DN_TILE // HALF
    per_tile = DN_TILE // DN_CHUNK
    head0 = pl.program_id(1) * hb

    ri = lax.broadcasted_iota(jnp.int32, (HALF, HALF), 0)
    ci = lax.broadcasted_iota(jnp.int32, (HALF, HALF), 1)
    incl = ci <= ri
    sub = (ri // SUB) == (ci // SUB)
    strict_in = sub & (ci < ri)
    strict_off = (~sub) & (ci < ri)
    n_sq = SUB.bit_length() - 2
    b3 = lambda x: x.reshape(n_half, HALF, x.shape[-1])

    n_tiles = s // DN_TILE
    n_units = hb * n_tiles

    def unit_coords(u):
        u = jnp.asarray(u, jnp.int32)
        i = lax.div(u, n_tiles)
        t = lax.rem(u, n_tiles)
        return i, t, pl.ds(pl.multiple_of(i * LANES, LANES), LANES)

    def prep(u):
        i, t, cols = unit_coords(u)
        head = head0 + i
        base = pl.multiple_of(t * DN_TILE, DN_TILE)
        rows = pl.ds(base, DN_TILE)
        first = t == 0
        q = _l2n(_conv_silu(q_ref, cq_ref, cols, base, DN_TILE, first, cbuf.at[0])) * qscale
        k = _l2n(_conv_silu(k_ref, ck_ref, cols, base, DN_TILE, first, cbuf.at[1]))
        v = _conv_silu(v_ref, cv_ref, cols, base, DN_TILE, first, cbuf.at[2])
        gs = g_ref[rows, :]
        gc = _lane_pick(gs, SM_A + head)
        bt = _lane_pick(gs, SM_B + head)
        gl = _lane_pick(gs, SM_GL + head)
        eg = jnp.exp(gc)
        kb = k * bt
        vb = v * bt
        kbe = kb * eg
        qd = q * eg
        kd = (k * jnp.exp(gl - gc)).astype(BF16)
        flb = jnp.broadcast_to(jnp.exp(gl), (DN_TILE, LANES))
        grow_t = gt_ref[pl.ds(SM_A + head, 1), rows]
        grow = jnp.stack([grow_t[:, j * HALF:(j + 1) * HALF] for j in range(n_half)], axis=0)
        dec = jnp.exp(jnp.where(incl, b3(gc) - grow, -jnp.inf))
        kbf = b3(k).astype(BF16)
        a_full = _bdot_nt(b3(kb).astype(BF16), kbf) * dec
        a = jnp.where(strict_in, a_full, 0.0)
        attn = (_bdot_nt(b3(q).astype(BF16), kbf) * dec).astype(BF16)
        p = -a
        r = p
        for _ in range(n_sq):
            pb = p.astype(BF16)
            p = _bdot(pb, pb)
            r = r + p + _bdot(r.astype(BF16), p.astype(BF16))
        nb = jnp.where(strict_off, a_full, 0.0)
        rb = r.astype(BF16)
        tn = nb + _bdot(rb, nb.astype(BF16))
        r = r - (tn + _bdot(tn.astype(BF16), rb))
        rhs = jnp.concatenate([b3(vb), b3(kbe)], axis=2)
        uw = (rhs + _bdot(r.astype(BF16), rhs.astype(BF16))).astype(BF16)
        aw = _bdot(attn, uw)
        op_s[rows, cols] = aw[:, :, :LANES].reshape(DN_TILE, LANES).astype(op_s.dtype)
        qp_s[rows, cols] = (qd - aw[:, :, LANES:].reshape(DN_TILE, LANES)).astype(qp_s.dtype)
        uw2 = uw.reshape(DN_TILE, 2 * LANES)
        for c in range(per_tile):
            c0 = c * DN_CHUNK
            kn = _dot_tn(kd[c0:c0 + DN_CHUNK, :], uw2[c0:c0 + DN_CHUNK, :])
            cidx = t * per_tile + c
            np_s[i, cidx] = kn[:, :LANES].astype(np_s.dtype)
            kp_s[i, cidx] = kn[:, LANES:].astype(kp_s.dtype)
            fl_s[i, pl.ds(cidx, 1), :] = flb[c0:c0 + 1, :]

    def prep_step(u, carry):
        prep(u)
        return carry

    lax.fori_loop(0, n_units, prep_step, 0)

    st_s[...] = jnp.zeros_like(st_s)

    def scan(c, carry):
        r = pl.ds(pl.multiple_of(c * DN_CHUNK, DN_CHUNK), DN_CHUNK)
        for i in range(hb):
            cols = pl.ds(i * LANES, LANES)
            st = st_s[i]
            stb = st.astype(BF16)
            o = _dot(qp_s[r, cols], stb) + op_s[r, cols].astype(F32)
            st_s[i] = st * fl_s[i, pl.ds(c, 1), :] - _dot(kp_s[i, c], stb) + np_s[i, c]
            gate = sz_ref[r, cols].astype(F32) * gb_ref[r, cols].astype(F32)
            o_ref[r, cols] = (_rms(o, nrm_ref[...]) * gate).astype(o_ref.dtype)
        return carry

    lax.fori_loop(0, s // DN_CHUNK, scan, 0, unroll=SCAN_UNROLL)


def _deltanet(z3, dn_conv, g3, gt3, dn_norm, d_model, qkv_off, gate_cols, hb=4):
    b, s, _ = z3.shape
    dh = d_model // DN_HEADS
    assert dh == LANES and s % DN_TILE == 0
    bw = hb * dh
    nblk = d_model // bw
    n_chunks = s // DN_CHUNK
    body = functools.partial(_dn_body, hb, dh ** -0.5)
    z_spec = lambda part: pl.BlockSpec(
        (None, s, bw), lambda i, h: (i, 0, (qkv_off + part * d_model) // bw + h))
    c_spec = lambda part: pl.BlockSpec((DN_CONV, bw), lambda i, h: (0, part * nblk + h))
    assert all(c % bw == 0 for c in gate_cols)
    gate_spec = lambda c: pl.BlockSpec((None, s, bw), lambda i, h: (i, 0, c // bw + h))
    return pl.pallas_call(
        body,
        grid=(b, nblk),
        in_specs=[
            z_spec(0), z_spec(1), z_spec(2),
            c_spec(0), c_spec(1), c_spec(2),
            pl.BlockSpec((None, s, LANES), lambda i, h: (i, 0, 0)),
            pl.BlockSpec((None, LANES, s), lambda i, h: (i, 0, 0)),
            pl.BlockSpec((1, dh), lambda i, h: (0, 0)),
            gate_spec(gate_cols[0]), gate_spec(gate_cols[1]),
        ],
        out_specs=pl.BlockSpec((None, s, bw), lambda i, h: (i, 0, h)),
        out_shape=jax.ShapeDtypeStruct((b, s, d_model), BF16),
        scratch_shapes=[
            pltpu.VMEM((hb, n_chunks, dh, dh), BF16),
            pltpu.VMEM((hb, n_chunks, dh, dh), F32),
            pltpu.VMEM((s, bw), BF16),
            pltpu.VMEM((s, bw), BF16),
            pltpu.VMEM((hb, n_chunks, LANES), F32),
            pltpu.VMEM((hb, dh, dh), F32),
            pltpu.VMEM((3, DN_TILE + 8, LANES), F32),
        ],
        compiler_params=_cparams(("parallel", "parallel")),
        name="deltanet_mixer",
    )(z3, z3, z3, dn_conv, dn_conv, dn_conv, g3, gt3, dn_norm, z3, z3)


def _merge_body(og_ref, od_ref, x_ref, w_ref, g_ref, x1_ref, h2_ref):
    mixed = og_ref[...].astype(F32) + od_ref[...].astype(F32)
    x1 = x_ref[...] + _dot(mixed.astype(BF16), w_ref[...])
    x1_ref[...] = x1
    h2_ref[...] = _rms(x1, g_ref[...]).astype(BF16)


def _merge(o_gla, o_dn, x2d, w_out, g_mlp, tm=512):
    t, d = x2d.shape
    row = lambda c: pl.BlockSpec((tm, d), lambda i, c=c: (i, c))
    return pl.pallas_call(
        _merge_body,
        grid=(t // tm,),
        in_specs=[
            row(0), row(0), row(0),
            pl.BlockSpec((d, d), lambda i: (0, 0), pipeline_mode=pl.Buffered(1)),
            pl.BlockSpec((1, d), lambda i: (0, 0)),
        ],
        out_specs=[row(0), row(0)],
        out_shape=[jax.ShapeDtypeStruct((t, d), F32), jax.ShapeDtypeStruct((t, d), BF16)],
        compiler_params=_cparams(("parallel",)),
        name="merge_out_proj",
    )(o_gla, o_dn, x2d, w_out, g_mlp)


def _mlp_body(h_ref, wu_ref, wd_ref, x_ref, o_ref):
    @pl.when(pl.program_id(1) == 0)
    def _():
        o_ref[...] = x_ref[...]

    mid = jnp.square(jnp.maximum(_dot(h_ref[...], wu_ref[...]), 0.0)).astype(BF16)
    o_ref[...] += _dot(mid, wd_ref[...])


def _mlp(h2, w_up, w_down, x1, tm=512, tf=1024):
    t, d = x1.shape
    ff = w_up.shape[1]
    return pl.pallas_call(
        _mlp_body,
        grid=(t // tm, ff // tf),
        in_specs=[
            pl.BlockSpec((tm, d), lambda i, j: (i, 0)),
            pl.BlockSpec((d, tf), lambda i, j: (0, j)),
            pl.BlockSpec((tf, d), lambda i, j: (j, 0)),
            pl.BlockSpec((tm, d), lambda i, j: (i, 0)),
        ],
        out_specs=pl.BlockSpec((tm, d), lambda i, j: (i, 0)),
        out_shape=jax.ShapeDtypeStruct((t, d), F32),
        compiler_params=_cparams(("parallel", "arbitrary")),
        name="relu2_mlp",
    )(h2, w_up, w_down, x1)


def _ple_body(x_ref, p_ref, wg_ref, wp_ref, gp_ref, gf_ref, o_ref):
    x2 = x_ref[...]
    h3 = _rms(x2, gp_ref[...]).astype(BF16)
    gate = _sigmoid(_dot(h3, wg_ref[...]))
    proj = _dot(p_ref[...].astype(BF16), wp_ref[...])
    o_ref[...] = _rms(x2 + gate * proj, gf_ref[...])


def _ple(x2, p2d, w_gate, w_proj, g_ple, g_final, tm=512):
    t, d = x2.shape
    pd = p2d.shape[1]
    return pl.pallas_call(
        _ple_body,
        grid=(t // tm,),
        in_specs=[
            pl.BlockSpec((tm, d), lambda i: (i, 0)),
            pl.BlockSpec((tm, pd), lambda i: (i, 0)),
            pl.BlockSpec((d, d), lambda i: (0, 0)),
            pl.BlockSpec((pd, d), lambda i: (0, 0)),
            pl.BlockSpec((1, d), lambda i: (0, 0)),
            pl.BlockSpec((1, d), lambda i: (0, 0)),
        ],
        out_specs=pl.BlockSpec((tm, d), lambda i: (i, 0)),
        out_shape=jax.ShapeDtypeStruct((t, d), F32),
        compiler_params=_cparams(("parallel",)),
        name="ple_final_norm",
    )(x2, p2d, w_gate, w_proj, g_ple, g_final)


def _layer(x, p_i, g_mix, w_in, gla_w2, gla_b, gla_norm, dn_conv, dn_a_log, dn_dt_bias, dn_norm,
           w_out, g_mlp, w_up, w_down, g_ple, w_ple_gate, w_ple_proj, g_out):
    b, s, d = x.shape
    t = b * s
    gla_qk = d // 2
    dn_qkv = 3 * d
    o_q, o_k = 0, gla_qk
    o_v = 2 * gla_qk
    o_g = o_v + d
    o_lr = o_g + d
    o_dn = o_lr + GLA_LOWRANK
    o_z = o_dn + dn_qkv
    o_a = o_z + d
    o_b = o_a + DN_HEADS
    o_ga = o_b + DN_HEADS
    o_gb = o_ga + d
    assert (o_q, o_k, o_v) == (0, gla_qk, 2 * gla_qk)
    w_in16 = w_in.astype(BF16)
    cs = lambda lo, n: w_in16[:, lo:lo + n]
    w_big = _repack(w_in16, ((o_lr, GLA_LOWRANK), (o_a, 2 * DN_HEADS)))
    w_small = jnp.concatenate(
        [cs(o_lr, GLA_LOWRANK), cs(o_a, DN_HEADS), cs(o_b, DN_HEADS),
         jnp.zeros((d, LANES - SM_GL), BF16)], axis=1)
    n_g, n_dn, n_z = o_g, o_lr, o_lr + dn_qkv
    n_ga = n_z + d
    n_gb = n_ga + d
    segments = ((0, ACT_NONE), (n_g, ACT_SILU), (n_dn, ACT_NONE), (n_z, ACT_SILU), (n_ga, ACT_SIGMOID))
    qkv_off = n_dn

    x2d = x.reshape(t, d)
    z2d, zs2d = _in_proj(x2d, g_mix.reshape(1, d), w_big, w_small, segments)
    z3 = z2d.reshape(b, s, -1)
    zs3 = zs2d.reshape(b, s, LANES)

    pad = lambda v: jnp.zeros((1, LANES), F32).at[0, SM_A:SM_B].set(v.astype(F32))
    g3, gt3 = _gates(zs3, pad(-jnp.exp(dn_a_log.astype(F32))), pad(dn_dt_bias))

    w2p = jnp.zeros((LANES, gla_qk), F32).at[:GLA_LOWRANK].set(gla_w2).astype(BF16)
    o_gla = _gla(z3, zs3, w2p, gla_b.reshape(1, -1).astype(F32), gla_norm.reshape(1, -1).astype(F32), d,
                 (n_g, n_ga))
    o_dn = _deltanet(z3, dn_conv.astype(F32), g3, gt3, dn_norm.reshape(1, -1).astype(F32), d, qkv_off,
                     (n_z, n_gb))

    x1, h2 = _merge(o_gla.reshape(t, d), o_dn.reshape(t, d), x2d, w_out.astype(BF16), g_mlp.reshape(1, d))
    x2 = _mlp(h2, w_up.astype(BF16), w_down.astype(BF16), x1)
    out = _ple(x2, p_i.reshape(t, -1), w_ple_gate.astype(BF16), w_ple_proj.astype(BF16),
               g_ple.reshape(1, d), g_out.reshape(1, d))
    return out.reshape(b, s, d)


def kernel(x, p, g_mix, w_in, gla_w2, gla_b, gla_norm, dn_conv, dn_a_log, dn_dt_bias, dn_norm,
           w_out, g_mlp, w_up, w_down, g_ple, w_ple_gate, w_ple_proj, g_final):
    depth = w_in.shape[0]
    assert depth == 1, "the final rms_norm is fused into the last layer's kernel"
    return _layer(x, p[0], g_mix[0], w_in[0], gla_w2[0], gla_b[0], gla_norm[0], dn_conv[0],
                  dn_a_log[0], dn_dt_bias[0], dn_norm[0], w_out[0], g_mlp[0], w_up[0], w_down[0],
                  g_ple[0], w_ple_gate[0], w_ple_proj[0], g_final)
```

```python
import functools

import jax
import jax.numpy as jnp
from jax import lax
from jax.experimental import pallas as pl
from jax.experimental.pallas import tpu as pltpu

F32 = jnp.float32
BF16 = jnp.bfloat16

EPS = 1e-6
CHUNK = 64
GLA_HEADS = 4
GLA_LOWRANK = 16
GLA_TAU = 16.0
DN_HEADS = 16
DN_CONV = 4
LANES = 128
TILE = 256
DN_CHUNK = 128
HALF = DN_CHUNK
DN_TILE = 2048
SUB = 64
SCAN_UNROLL = 8
VMEM_LIMIT = 56 * 1024 * 1024

SM_A = GLA_LOWRANK
SM_B = SM_A + DN_HEADS
SM_GL = SM_B + DN_HEADS


def _cparams(sem):
    return pltpu.CompilerParams(dimension_semantics=sem, vmem_limit_bytes=VMEM_LIMIT)


def _dot(a, b):
    return jnp.dot(a, b, preferred_element_type=F32)


def _dot_nt(a, b):
    return lax.dot_general(a, b, (((1,), (1,)), ((), ())), preferred_element_type=F32)


def _dot_tn(a, b):
    return lax.dot_general(a, b, (((0,), (0,)), ((), ())), preferred_element_type=F32)


def _sigmoid(x):
    return 0.5 * jnp.tanh(0.5 * x) + 0.5


def _silu_half(h):
    return h * jnp.tanh(h) + h


def _silu(x):
    return _silu_half(0.5 * x)


def _log_sigmoid(x):
    return jnp.minimum(x, 0.0) - jnp.log(1.0 + jnp.exp(-jnp.abs(x)))


def _softplus(x):
    return jnp.maximum(x, 0.0) + jnp.log(1.0 + jnp.exp(-jnp.abs(x)))


def _rms(x, g):
    return x * lax.rsqrt(jnp.mean(x * x, axis=-1, keepdims=True) + EPS) * g


def _chunk_cumsum(x, row, chunk=CHUNK):
    shift = 1
    while shift < chunk:
        x = x + jnp.where(row >= shift, pltpu.roll(x, shift, 0), 0.0)
        shift *= 2
    return x


def _chunk_rev_cumsum(x, row, chunk=CHUNK):
    n = x.shape[0]
    shift = 1
    while shift < chunk:
        x = x + jnp.where(row < chunk - shift, pltpu.roll(x, n - shift, 0), 0.0)
        shift *= 2
    return x


def _repack_body(shifts, w_ref, nxt_ref, o_ref):
    j = pl.program_id(1)
    tn = o_ref.shape[1]
    for shift in sorted(set(shifts)):
        hit = functools.reduce(lambda p, q: p | q, [j == jt for jt, sh in enumerate(shifts) if sh == shift])

        @pl.when(hit)
        def _(shift=shift):
            if shift == 0:
                o_ref[...] = w_ref[...]
            else:
                o_ref[...] = jnp.concatenate([w_ref[:, shift:], nxt_ref[:, :shift]], axis=1)


def _repack(w16, cuts, tr=1024, tn=2048):
    d, n_src = w16.shape
    n_out = n_src - sum(w for _, w in cuts)
    assert n_out % tn == 0 and d % tr == 0
    shifts = tuple(sum(w for s, w in cuts if s <= jt * tn + sum(w2 for s2, w2 in cuts if s2 < s)) for jt in range(n_out // tn))
    assert max(shifts) < LANES
    assert n_out + shifts[-1] == n_src
    return pl.pallas_call(
        functools.partial(_repack_body, shifts),
        grid=(d // tr, n_out // tn),
        in_specs=[
            pl.BlockSpec((tr, tn), lambda i, j: (i, j)),
            pl.BlockSpec((tr, LANES), lambda i, j: (i, (j + 1) * (tn // LANES))),
        ],
        out_specs=pl.BlockSpec((tr, tn), lambda i, j: (i, j)),
        out_shape=jax.ShapeDtypeStruct((d, n_out), w16.dtype),
        compiler_params=_cparams(("parallel", "parallel")),
        name="w_in_repack",
    )(w16, w16)


ACT_NONE, ACT_SILU, ACT_SIGMOID = 0, 1, 2
_ACT_FN = {ACT_NONE: lambda v: v, ACT_SILU: _silu, ACT_SIGMOID: _sigmoid}


def _in_proj_body(tile_acts, x_ref, g_ref, wb_ref, ws_ref, z_ref, zs_ref, h_ref):
    j = pl.program_id(1)

    @pl.when(j == 0)
    def _():
        hb = _rms(x_ref[...], g_ref[...]).astype(BF16)
        h_ref[...] = hb
        zs_ref[...] = _dot(hb, ws_ref[...])

    for act, fn in _ACT_FN.items():
        tiles = [jt for jt, a in enumerate(tile_acts) if a == act]
        if not tiles:
            continue
        hit = functools.reduce(lambda p, q: p | q, [j == jt for jt in tiles])

        @pl.when(hit)
        def _(fn=fn):
            z_ref[...] = fn(_dot(h_ref[...], wb_ref[...])).astype(BF16)


def _in_proj(x2d, g_mix, w_big, w_small, segments, tm=1024, tn=2048):
    t, d = x2d.shape
    n = w_big.shape[1]
    assert all(lo % tn == 0 for lo, _ in segments) and n % tn == 0
    starts = [lo // tn for lo, _ in segments] + [n // tn]
    tile_acts = tuple(act for (_, act), a, b in zip(segments, starts, starts[1:]) for _ in range(b - a))
    body = functools.partial(_in_proj_body, tile_acts)
    return pl.pallas_call(
        body,
        grid=(t // tm, n // tn),
        in_specs=[
            pl.BlockSpec((tm, d), lambda i, j: (i, 0)),
            pl.BlockSpec((1, d), lambda i, j: (0, 0)),
            pl.BlockSpec((d, tn), lambda i, j: (0, j)),
            pl.BlockSpec((d, LANES), lambda i, j: (0, 0)),
        ],
        out_specs=[
            pl.BlockSpec((tm, tn), lambda i, j: (i, j)),
            pl.BlockSpec((tm, LANES), lambda i, j: (i, 0)),
        ],
        out_shape=[
            jax.ShapeDtypeStruct((t, n), BF16),
            jax.ShapeDtypeStruct((t, LANES), F32),
        ],
        scratch_shapes=[pltpu.VMEM((tm, d), BF16)],
        compiler_params=_cparams(("parallel", "arbitrary")),
        name="in_proj",
    )(x2d, g_mix, w_big, w_small)


def _gates_body(zs_ref, aneg_ref, dtb_ref, g_ref, gt_ref):
    zs = zs_ref[...]
    lane = lax.broadcasted_iota(jnp.int32, zs.shape, 1)
    row = lax.broadcasted_iota(jnp.int32, zs.shape, 0) % DN_CHUNK
    is_a = (lane >= SM_A) & (lane < SM_B)
    is_b = (lane >= SM_B) & (lane < SM_GL)
    g = jnp.where(is_a, aneg_ref[...] * _softplus(zs + dtb_ref[...]), 0.0)
    gcum = _chunk_cumsum(g, row, DN_CHUNK)
    gtot = gcum + _chunk_rev_cumsum(g, row, DN_CHUNK) - g
    beta = _sigmoid(zs)
    out = jnp.where(is_a, gcum, jnp.where(is_b, beta, 0.0))
    out = out + pltpu.roll(jnp.where(is_a, gtot, 0.0), SM_GL - SM_A, 1)
    g_ref[...] = out
    gt_ref[...] = out.T


def _gates(zs3, a_neg, dt_bias):
    b, s, _ = zs3.shape
    return pl.pallas_call(
        _gates_body,
        grid=(b,),
        in_specs=[
            pl.BlockSpec((None, s, LANES), lambda i: (i, 0, 0)),
            pl.BlockSpec((1, LANES), lambda i: (0, 0)),
            pl.BlockSpec((1, LANES), lambda i: (0, 0)),
        ],
        out_specs=[
            pl.BlockSpec((None, s, LANES), lambda i: (i, 0, 0)),
            pl.BlockSpec((None, LANES, s), lambda i: (i, 0, 0)),
        ],
        out_shape=[
            jax.ShapeDtypeStruct((b, s, LANES), F32),
            jax.ShapeDtypeStruct((b, LANES, s), F32),
        ],
        compiler_params=_cparams(("parallel",)),
        name="dn_gates",
    )(zs3, a_neg, dt_bias)


def _gla_body(hb, scale, q_ref, k_ref, v_ref, lr_ref, w2_ref, b_ref, nrm_ref, sg_ref, ga_ref, o_ref, st_ref):
    s = q_ref.shape[0]
    dk = q_ref.shape[1] // hb
    dv = v_ref.shape[1] // hb
    per = TILE // CHUNK
    heads = range(hb)
    kcol = [pl.ds(h * dk, dk) for h in heads]
    vcol = [pl.ds(h * dv, dv) for h in heads]

    @pl.when(pl.program_id(1) == 0)
    def _():
        st_ref[...] = jnp.zeros_like(st_ref)

    row = lax.broadcasted_iota(jnp.int32, (TILE, dk), 0) % CHUNK
    ri = lax.broadcasted_iota(jnp.int32, (TILE, TILE), 0)
    ci = lax.broadcasted_iota(jnp.int32, (TILE, TILE), 1)
    causal = ((ri // CHUNK) == (ci // CHUNK)) & (ci <= ri)
    cross = ((ri // CHUNK) == (ci // CHUNK) + 1) & ((ri // (2 * CHUNK)) == (ci // (2 * CHUNK)))

    def tile(t, carry):
        base = pl.multiple_of(t * TILE, TILE)
        r = pl.ds(base, TILE)
        lrb = lr_ref[r, :].astype(BF16)
        q = [q_ref[r, kcol[h]].astype(F32) * scale for h in heads]
        k = [k_ref[r, kcol[h]].astype(F32) for h in heads]
        v = [v_ref[r, vcol[h]] for h in heads]
        pre = [_dot(lrb, w2_ref[:, kcol[h]]) + b_ref[:, kcol[h]] for h in heads]
        bc = [_chunk_cumsum(_log_sigmoid(x) * (1.0 / GLA_TAU), row) for x in pre]
        q_dec = [q[h] * jnp.exp(bc[h]) for h in heads]
        q_in = [x.astype(BF16) for x in q_dec]
        k_in = [(k[h] * jnp.exp(-bc[h])).astype(BF16) for h in heads]
        bl = [jnp.concatenate([jnp.broadcast_to(bc[h][(c + 1) * CHUNK - 1:(c + 1) * CHUNK, :], (CHUNK, dk))
                               for c in range(per)], axis=0) for h in heads]
        k_dec = [k[h] * jnp.exp(bl[h] - bc[h]) for h in heads]
        a = [(jnp.where(causal, _dot_nt(q_in[h], k_in[h]), 0.0)
              + jnp.where(cross, _dot_nt(q_in[h], k_dec[h].astype(BF16)), 0.0)).astype(BF16) for h in heads]
        o_intra = [_dot(a[h], v[h]) for h in heads]
        for m in range(per // 2):
            p0, p1, p2 = 2 * m * CHUNK, (2 * m + 1) * CHUNK, (2 * m + 2) * CHUNK
            f1 = [jnp.exp(bc[h][p1 - 1:p1, :]) for h in heads]
            f2 = [jnp.exp(bc[h][p2 - 1:p2, :]) for h in heads]
            qs = [jnp.concatenate([q_in[h][p0:p1, :], (q_dec[h][p1:p2, :] * f1[h]).astype(BF16)], axis=0)
                  for h in heads]
            kd = [jnp.concatenate([(k_dec[h][p0:p1, :] * f2[h]).astype(BF16), k_dec[h][p1:p2, :].astype(BF16)],
                                  axis=0) for h in heads]
            st = [st_ref[h] for h in heads]
            o = [o_intra[h][p0:p2, :] + _dot_nt(qs[h], st[h].astype(BF16)) for h in heads]
            for h in heads:
                st_ref[h] = st[h] * (f1[h] * f2[h]) + _dot_tn(v[h][p0:p2, :], kd[h])
            rc = pl.ds(base + p0, 2 * CHUNK)
            for h in heads:
                gate = sg_ref[rc, vcol[h]].astype(F32) * ga_ref[rc, vcol[h]].astype(F32)
                o_ref[rc, vcol[h]] = (_rms(o[h], nrm_ref[...]) * gate).astype(o_ref.dtype)
        return carry

    lax.fori_loop(0, s // TILE, tile, 0)


def _gla(z3, zs3, w2p, gla_b, gla_norm, d_model, gate_cols, ts=512):
    hb = GLA_HEADS
    b, s, _ = z3.shape
    dk = d_model // (2 * GLA_HEADS)
    dv = d_model // GLA_HEADS
    assert s % ts == 0 and ts % TILE == 0
    qk_cols = GLA_HEADS * dk
    kw, vw = hb * dk, hb * dv
    sg_col, ga_col = gate_cols
    assert sg_col % vw == 0 and ga_col % vw == 0
    body = functools.partial(_gla_body, hb, dk ** -0.5)
    return pl.pallas_call(
        body,
        grid=(b, s // ts),
        in_specs=[
            pl.BlockSpec((None, ts, kw), lambda i, t: (i, t, 0)),
            pl.BlockSpec((None, ts, kw), lambda i, t: (i, t, qk_cols // kw)),
            pl.BlockSpec((None, ts, vw), lambda i, t: (i, t, 2 * qk_cols // vw)),
            pl.BlockSpec((None, ts, LANES), lambda i, t: (i, t, 0)),
            pl.BlockSpec((LANES, kw), lambda i, t: (0, 0)),
            pl.BlockSpec((1, kw), lambda i, t: (0, 0)),
            pl.BlockSpec((1, dv), lambda i, t: (0, 0)),
            pl.BlockSpec((None, ts, vw), lambda i, t: (i, t, sg_col // vw)),
            pl.BlockSpec((None, ts, vw), lambda i, t: (i, t, ga_col // vw)),
        ],
        out_specs=pl.BlockSpec((None, ts, vw), lambda i, t: (i, t, 0)),
        out_shape=jax.ShapeDtypeStruct((b, s, GLA_HEADS * dv), BF16),
        scratch_shapes=[pltpu.VMEM((hb, dv, dk), F32)],
        compiler_params=_cparams(("parallel", "arbitrary")),
        name="gla_mixer",
    )(z3, z3, z3, zs3, w2p, gla_b, gla_norm, z3, z3)


def _conv_silu(x_ref, w_ref, cols, base, n_rows, first, buf):
    pbase = pl.multiple_of(jnp.maximum(base - 8, 0), 8)
    buf[pl.ds(0, 8), :] = jnp.where(first, 0.0, x_ref[pl.ds(pbase, 8), cols].astype(F32))
    buf[pl.ds(8, n_rows), :] = x_ref[pl.ds(base, n_rows), cols].astype(F32)
    w = 0.5 * w_ref[:, cols]
    acc = None
    for d in range(DN_CONV):
        term = buf[pl.ds(8 - d, n_rows), :] * w[DN_CONV - 1 - d:DN_CONV - d, :]
        acc = term if acc is None else acc + term
    return _silu_half(acc)


def _l2n(t):
    return t * lax.rsqrt(jnp.sum(t * t, axis=-1, keepdims=True) + EPS)


def _lane_pick(x, idx):
    lane = lax.broadcasted_iota(jnp.int32, x.shape, 1)
    return jnp.sum(jnp.where(lane == idx, x, 0.0), axis=-1, keepdims=True)


def _bdot(a, b):
    return lax.dot_general(a, b, (((2,), (1,)), ((0,), (0,))), preferred_element_type=F32)


def _bdot_nt(a, b):
    return lax.dot_general(a, b, (((2,), (2,)), ((0,), (0,))), preferred_element_type=F32)


def _dn_body(hb, qscale, q_ref, k_ref, v_ref, cq_ref, ck_ref, cv_ref, g_ref, gt_ref, nrm_ref, sz_ref, gb_ref,
             o_ref, kp_s, np_s, qp_s, op_s, fl_s, st_s, cbuf):
    s = q_ref.shape[0]
    n_half = DN_TILE // HALF
    per_tile = DN_TILE // DN_CHUNK
    head0 = pl.program_id(1) * hb

    ri = lax.broadcasted_iota(jnp.int32, (HALF, HALF), 0)
    ci = lax.broadcasted_iota(jnp.int32, (HALF, HALF), 1)
    incl = ci <= ri
    sub = (ri // SUB) == (ci // SUB)
    strict_in = sub & (ci < ri)
    strict_off = (~sub) & (ci < ri)
    n_sq = SUB.bit_length() - 2
    b3 = lambda x: x.reshape(n_half, HALF, x.shape[-1])

    n_tiles = s // DN_TILE
    n_units = hb * n_tiles

    def unit_coords(u):
        u = jnp.asarray(u, jnp.int32)
        i = lax.div(u, n_tiles)
        t = lax.rem(u, n_tiles)
        return i, t, pl.ds(pl.multiple_of(i * LANES, LANES), LANES)

    def prep(u):
        i, t, cols = unit_coords(u)
        head = head0 + i
        base = pl.multiple_of(t * DN_TILE, DN_TILE)
        rows = pl.ds(base, DN_TILE)
        first = t == 0
        q = _l2n(_conv_silu(q_ref, cq_ref, cols, base, DN_TILE, first, cbuf.at[0])) * qscale
        k = _l2n(_conv_silu(k_ref, ck_ref, cols, base, DN_TILE, first, cbuf.at[1]))
        v = _conv_silu(v_ref, cv_ref, cols, base, DN_TILE, first, cbuf.at[2])
        gs = g_ref[rows, :]
        gc = _lane_pick(gs, SM_A + head)
        bt = _lane_pick(gs, SM_B + head)
        gl = _lane_pick(gs, SM_GL + head)
        eg = jnp.exp(gc)
        kb = k * bt
        vb = v * bt
        kbe = kb * eg
        qd = q * eg
        kd = (k * jnp.exp(gl - gc)).astype(BF16)
        flb = jnp.broadcast_to(jnp.exp(gl), (DN_TILE, LANES))
        grow_t = gt_ref[pl.ds(SM_A + head, 1), rows]
        grow = jnp.stack([grow_t[:, j * HALF:(j + 1) * HALF] for j in range(n_half)], axis=0)
        dec = jnp.exp(jnp.where(incl, b3(gc) - grow, -jnp.inf))
        kbf = b3(k).astype(BF16)
        a_full = _bdot_nt(b3(kb).astype(BF16), kbf) * dec
        a = jnp.where(strict_in, a_full, 0.0)
        attn = (_bdot_nt(b3(q).astype(BF16), kbf) * dec).astype(BF16)
        p = -a
        r = p
        for _ in range(n_sq):
            pb = p.astype(BF16)
            p = _bdot(pb, pb)
            r = r + p + _bdot(r.astype(BF16), p.astype(BF16))
        nb = jnp.where(strict_off, a_full, 0.0)
        rb = r.astype(BF16)
        tn = nb + _bdot(rb, nb.astype(BF16))
        r = r - (tn + _bdot(tn.astype(BF16), rb))
        rhs = jnp.concatenate([b3(vb), b3(kbe)], axis=2)
        uw = (rhs + _bdot(r.astype(BF16), rhs.astype(BF16))).astype(BF16)
        aw = _bdot(attn, uw)
        op_s[rows, cols] = aw[:, :, :LANES].reshape(DN_TILE, LANES).astype(op_s.dtype)
        qp_s[rows, cols] = (qd - aw[:, :, LANES:].reshape(DN_TILE, LANES)).astype(qp_s.dtype)
        uw2 = uw.reshape(DN_TILE, 2 * LANES)
        for c in range(per_tile):
            c0 = c * DN_CHUNK
            kn = _dot_tn(kd[c0:c0 + DN_CHUNK, :], uw2[c0:c0 + DN_CHUNK, :])
            cidx = t * per_tile + c
            np_s[i, cidx] = kn[:, :LANES].astype(np_s.dtype)
            kp_s[i, cidx] = kn[:, LANES:].astype(kp_s.dtype)
            fl_s[i, pl.ds(cidx, 1), :] = flb[c0:c0 + 1, :]

    def prep_step(u, carry):
        prep(u)
        return carry

    lax.fori_loop(0, n_units, prep_step, 0)

    st_s[...] = jnp.zeros_like(st_s)

    def scan(c, carry):
        r = pl.ds(pl.multiple_of(c * DN_CHUNK, DN_CHUNK), DN_CHUNK)
        for i in range(hb):
            cols = pl.ds(i * LANES, LANES)
            st = st_s[i]
            stb = st.astype(BF16)
            o = _dot(qp_s[r, cols], stb) + op_s[r, cols].astype(F32)
            st_s[i] = st * fl_s[i, pl.ds(c, 1), :] - _dot(kp_s[i, c], stb) + np_s[i, c]
            gate = sz_ref[r, cols].astype(F32) * gb_ref[r, cols].astype(F32)
            o_ref[r, cols] = (_rms(o, nrm_ref[...]) * gate).astype(o_ref.dtype)
        return carry

    lax.fori_loop(0, s // DN_CHUNK, scan, 0, unroll=SCAN_UNROLL)


def _deltanet(z3, dn_conv, g3, gt3, dn_norm, d_model, qkv_off, gate_cols, hb=4):
    b, s, _ = z3.shape
    dh = d_model // DN_HEADS
    assert dh == LANES and s % DN_TILE == 0
    bw = hb * dh
    nblk = d_model // bw
    n_chunks = s // DN_CHUNK
    body = functools.partial(_dn_body, hb, dh ** -0.5)
    z_spec = lambda part: pl.BlockSpec(
        (None, s, bw), lambda i, h: (i, 0, (qkv_off + part * d_model) // bw + h))
    c_spec = lambda part: pl.BlockSpec((DN_CONV, bw), lambda i, h: (0, part * nblk + h))
    assert all(c % bw == 0 for c in gate_cols)
    gate_spec = lambda c: pl.BlockSpec((None, s, bw), lambda i, h: (i, 0, c // bw + h))
    return pl.pallas_call(
        body,
        grid=(b, nblk),
        in_specs=[
            z_spec(0), z_spec(1), z_spec(2),
            c_spec(0), c_spec(1), c_spec(2),
            pl.BlockSpec((None, s, LANES), lambda i, h: (i, 0, 0)),
            pl.BlockSpec((None, LANES, s), lambda i, h: (i, 0, 0)),
            pl.BlockSpec((1, dh), lambda i, h: (0, 0)),
            gate_spec(gate_cols[0]), gate_spec(gate_cols[1]),
        ],
        out_specs=pl.BlockSpec((None, s, bw), lambda i, h: (i, 0, h)),
        out_shape=jax.ShapeDtypeStruct((b, s, d_model), BF16),
        scratch_shapes=[
            pltpu.VMEM((hb, n_chunks, dh, dh), BF16),
            pltpu.VMEM((hb, n_chunks, dh, dh), F32),
            pltpu.VMEM((s, bw), BF16),
            pltpu.VMEM((s, bw), BF16),
            pltpu.VMEM((hb, n_chunks, LANES), F32),
            pltpu.VMEM((hb, dh, dh), F32),
            pltpu.VMEM((3, DN_TILE + 8, LANES), F32),
        ],
        compiler_params=_cparams(("parallel", "parallel")),
        name="deltanet_mixer",
    )(z3, z3, z3, dn_conv, dn_conv, dn_conv, g3, gt3, dn_norm, z3, z3)


MERGE_SLOTS = 3


def _merge_body(og_hbm, od_hbm, x_ref, w_ref, g_ref, x1_ref, h2_ref, og_buf, od_buf, sem):
    s = pl.program_id(0)
    n = pl.num_programs(0)
    tm = x_ref.shape[0]
    ahead = MERGE_SLOTS - 1

    def copies(step):
        slot = lax.rem(step, MERGE_SLOTS)
        rows = pl.ds(pl.multiple_of(step * tm, tm), tm)
        return (pltpu.make_async_copy(og_hbm.at[rows, :], og_buf.at[slot], sem.at[0, slot]),
                pltpu.make_async_copy(od_hbm.at[rows, :], od_buf.at[slot], sem.at[1, slot]))

    @pl.when(s == 0)
    def _():
        for k in range(ahead):
            for c in copies(k):
                c.start()

    @pl.when(s + ahead < n)
    def _():
        for c in copies(s + ahead):
            c.start()

    for c in copies(s):
        c.wait()
    slot = lax.rem(s, MERGE_SLOTS)
    mixed = og_buf[slot].astype(F32) + od_buf[slot].astype(F32)
    x1 = x_ref[...] + _dot(mixed.astype(BF16), w_ref[...])
    x1_ref[...] = x1
    h2_ref[...] = _rms(x1, g_ref[...]).astype(BF16)


def _merge(o_gla, o_dn, x2d, w_out, g_mlp, tm=512):
    t, d = x2d.shape
    assert t // tm >= MERGE_SLOTS - 1
    row = lambda c: pl.BlockSpec((tm, d), lambda i, c=c: (i, c))
    return pl.pallas_call(
        _merge_body,
        grid=(t // tm,),
        in_specs=[
            pl.BlockSpec(memory_space=pl.ANY), pl.BlockSpec(memory_space=pl.ANY), row(0),
            pl.BlockSpec((d, d), lambda i: (0, 0), pipeline_mode=pl.Buffered(1)),
            pl.BlockSpec((1, d), lambda i: (0, 0)),
        ],
        out_specs=[row(0), row(0)],
        out_shape=[jax.ShapeDtypeStruct((t, d), F32), jax.ShapeDtypeStruct((t, d), BF16)],
        scratch_shapes=[
            pltpu.VMEM((MERGE_SLOTS, tm, d), BF16),
            pltpu.VMEM((MERGE_SLOTS, tm, d), BF16),
            pltpu.SemaphoreType.DMA((2, MERGE_SLOTS)),
        ],
        compiler_params=_cparams(("arbitrary",)),
        name="merge_out_proj",
    )(o_gla, o_dn, x2d, w_out, g_mlp)


def _mlp_body(h_ref, wu_ref, wd_ref, x_ref, o_ref):
    @pl.when(pl.program_id(1) == 0)
    def _():
        o_ref[...] = x_ref[...]

    mid = jnp.square(jnp.maximum(_dot(h_ref[...], wu_ref[...]), 0.0)).astype(BF16)
    o_ref[...] += _dot(mid, wd_ref[...])


def _mlp(h2, w_up, w_down, x1, tm=512, tf=1024):
    t, d = x1.shape
    ff = w_up.shape[1]
    return pl.pallas_call(
        _mlp_body,
        grid=(t // tm, ff // tf),
        in_specs=[
            pl.BlockSpec((tm, d), lambda i, j: (i, 0)),
            pl.BlockSpec((d, tf), lambda i, j: (0, j)),
            pl.BlockSpec((tf, d), lambda i, j: (j, 0)),
            pl.BlockSpec((tm, d), lambda i, j: (i, 0)),
        ],
        out_specs=pl.BlockSpec((tm, d), lambda i, j: (i, 0)),
        out_shape=jax.ShapeDtypeStruct((t, d), F32),
        compiler_params=_cparams(("parallel", "arbitrary")),
        name="relu2_mlp",
    )(h2, w_up, w_down, x1)


def _ple_body(x_ref, p_ref, wg_ref, wp_ref, gp_ref, gf_ref, o_ref):
    x2 = x_ref[...]
    h3 = _rms(x2, gp_ref[...]).astype(BF16)
    gate = _sigmoid(_dot(h3, wg_ref[...]))
    proj = _dot(p_ref[...].astype(BF16), wp_ref[...])
    o_ref[...] = _rms(x2 + gate * proj, gf_ref[...])


def _ple(x2, p2d, w_gate, w_proj, g_ple, g_final, tm=512):
    t, d = x2.shape
    pd = p2d.shape[1]
    return pl.pallas_call(
        _ple_body,
        grid=(t // tm,),
        in_specs=[
            pl.BlockSpec((tm, d), lambda i: (i, 0)),
            pl.BlockSpec((tm, pd), lambda i: (i, 0)),
            pl.BlockSpec((d, d), lambda i: (0, 0)),
            pl.BlockSpec((pd, d), lambda i: (0, 0)),
            pl.BlockSpec((1, d), lambda i: (0, 0)),
            pl.BlockSpec((1, d), lambda i: (0, 0)),
        ],
        out_specs=pl.BlockSpec((tm, d), lambda i: (i, 0)),
        out_shape=jax.ShapeDtypeStruct((t, d), F32),
        compiler_params=_cparams(("parallel",)),
        name="ple_final_norm",
    )(x2, p2d, w_gate, w_proj, g_ple, g_final)


def _layer(x, p_i, g_mix, w_in, gla_w2, gla_b, gla_norm, dn_conv, dn_a_log, dn_dt_bias, dn_norm,
           w_out, g_mlp, w_up, w_down, g_ple, w_ple_gate, w_ple_proj, g_out):
    b, s, d = x.shape
    t = b * s
    gla_qk = d // 2
    dn_qkv = 3 * d
    o_q, o_k = 0, gla_qk
    o_v = 2 * gla_qk
    o_g = o_v + d
    o_lr = o_g + d
    o_dn = o_lr + GLA_LOWRANK
    o_z = o_dn + dn_qkv
    o_a = o_z + d
    o_b = o_a + DN_HEADS
    o_ga = o_b + DN_HEADS
    o_gb = o_ga + d
    assert (o_q, o_k, o_v) == (0, gla_qk, 2 * gla_qk)
    w_in16 = w_in.astype(BF16)
    cs = lambda lo, n: w_in16[:, lo:lo + n]
    w_big = _repack(w_in16, ((o_lr, GLA_LOWRANK), (o_a, 2 * DN_HEADS)))
    w_small = jnp.concatenate(
        [cs(o_lr, GLA_LOWRANK), cs(o_a, DN_HEADS), cs(o_b, DN_HEADS),
         jnp.zeros((d, LANES - SM_GL), BF16)], axis=1)
    n_g, n_dn, n_z = o_g, o_lr, o_lr + dn_qkv
    n_ga = n_z + d
    n_gb = n_ga + d
    segments = ((0, ACT_NONE), (n_g, ACT_SILU), (n_dn, ACT_NONE), (n_z, ACT_SILU), (n_ga, ACT_SIGMOID))
    qkv_off = n_dn

    x2d = x.reshape(t, d)
    z2d, zs2d = _in_proj(x2d, g_mix.reshape(1, d), w_big, w_small, segments)
    z3 = z2d.reshape(b, s, -1)
    zs3 = zs2d.reshape(b, s, LANES)

    pad = lambda v: jnp.zeros((1, LANES), F32).at[0, SM_A:SM_B].set(v.astype(F32))
    g3, gt3 = _gates(zs3, pad(-jnp.exp(dn_a_log.astype(F32))), pad(dn_dt_bias))

    w2p = jnp.zeros((LANES, gla_qk), F32).at[:GLA_LOWRANK].set(gla_w2).astype(BF16)
    o_gla = _gla(z3, zs3, w2p, gla_b.reshape(1, -1).astype(F32), gla_norm.reshape(1, -1).astype(F32), d,
                 (n_g, n_ga))
    o_dn = _deltanet(z3, dn_conv.astype(F32), g3, gt3, dn_norm.reshape(1, -1).astype(F32), d, qkv_off,
                     (n_z, n_gb))

    x1, h2 = _merge(o_gla.reshape(t, d), o_dn.reshape(t, d), x2d, w_out.astype(BF16), g_mlp.reshape(1, d))
    x2 = _mlp(h2, w_up.astype(BF16), w_down.astype(BF16), x1)
    out = _ple(x2, p_i.reshape(t, -1), w_ple_gate.astype(BF16), w_ple_proj.astype(BF16),
               g_ple.reshape(1, d), g_out.reshape(1, d))
    return out.reshape(b, s, d)


def kernel(x, p, g_mix, w_in, gla_w2, gla_b, gla_norm, dn_conv, dn_a_log, dn_dt_bias, dn_norm,
           w_out, g_mlp, w_up, w_down, g_ple, w_ple_gate, w_ple_proj, g_final):
    depth = w_in.shape[0]
    assert depth == 1, "the final rms_norm is fused into the last layer's kernel"
    return _layer(x, p[0], g_mix[0], w_in[0], gla_w2[0], gla_b[0], gla_norm[0], dn_conv[0],
                  dn_a_log[0], dn_dt_bias[0], dn_norm[0], w_out[0], g_mlp[0], w_up[0], w_down[0],
                  g_ple[0], w_ple_gate[0], w_ple_proj[0], g_final)
```

```python
import functools

import jax
import jax.numpy as jnp
from jax import lax
from jax.experimental import pallas as pl
from jax.experimental.pallas import tpu as pltpu

F32 = jnp.float32
BF16 = jnp.bfloat16

EPS = 1e-6
CHUNK = 64
GLA_HEADS = 4
GLA_LOWRANK = 16
GLA_TAU = 16.0
DN_HEADS = 16
DN_CONV = 4
LANES = 128
TILE = 256
DN_CHUNK = 128
HALF = DN_CHUNK
DN_TILE = 2048
SUB = 64
SCAN_UNROLL = 8
VMEM_LIMIT = 56 * 1024 * 1024

SM_A = GLA_LOWRANK
SM_B = SM_A + DN_HEADS
SM_GL = SM_B + DN_HEADS


def _cparams(sem):
    return pltpu.CompilerParams(dimension_semantics=sem, vmem_limit_bytes=VMEM_LIMIT)


def _dot(a, b):
    return jnp.dot(a, b, preferred_element_type=F32)


def _dot_nt(a, b):
    return lax.dot_general(a, b, (((1,), (1,)), ((), ())), preferred_element_type=F32)


def _dot_tn(a, b):
    return lax.dot_general(a, b, (((0,), (0,)), ((), ())), preferred_element_type=F32)


def _sigmoid(x):
    return 0.5 * jnp.tanh(0.5 * x) + 0.5


def _silu_half(h):
    return h * jnp.tanh(h) + h


def _silu(x):
    return _silu_half(0.5 * x)


def _log_sigmoid(x):
    return jnp.minimum(x, 0.0) - jnp.log(1.0 + jnp.exp(-jnp.abs(x)))


def _softplus(x):
    return jnp.maximum(x, 0.0) + jnp.log(1.0 + jnp.exp(-jnp.abs(x)))


def _rms(x, g):
    return x * lax.rsqrt(jnp.mean(x * x, axis=-1, keepdims=True) + EPS) * g


def _chunk_cumsum(x, row, chunk=CHUNK):
    shift = 1
    while shift < chunk:
        x = x + jnp.where(row >= shift, pltpu.roll(x, shift, 0), 0.0)
        shift *= 2
    return x


def _chunk_rev_cumsum(x, row, chunk=CHUNK):
    n = x.shape[0]
    shift = 1
    while shift < chunk:
        x = x + jnp.where(row < chunk - shift, pltpu.roll(x, n - shift, 0), 0.0)
        shift *= 2
    return x


def _repack_body(shifts, w_ref, nxt_ref, o_ref):
    j = pl.program_id(1)
    tn = o_ref.shape[1]
    for shift in sorted(set(shifts)):
        hit = functools.reduce(lambda p, q: p | q, [j == jt for jt, sh in enumerate(shifts) if sh == shift])

        @pl.when(hit)
        def _(shift=shift):
            if shift == 0:
                o_ref[...] = w_ref[...]
            else:
                o_ref[...] = jnp.concatenate([w_ref[:, shift:], nxt_ref[:, :shift]], axis=1)


def _repack(w16, cuts, tr=1024, tn=2048):
    d, n_src = w16.shape
    n_out = n_src - sum(w for _, w in cuts)
    assert n_out % tn == 0 and d % tr == 0
    shifts = tuple(sum(w for s, w in cuts if s <= jt * tn + sum(w2 for s2, w2 in cuts if s2 < s)) for jt in range(n_out // tn))
    assert max(shifts) < LANES
    assert n_out + shifts[-1] == n_src
    return pl.pallas_call(
        functools.partial(_repack_body, shifts),
        grid=(d // tr, n_out // tn),
        in_specs=[
            pl.BlockSpec((tr, tn), lambda i, j: (i, j)),
            pl.BlockSpec((tr, LANES), lambda i, j: (i, (j + 1) * (tn // LANES))),
        ],
        out_specs=pl.BlockSpec((tr, tn), lambda i, j: (i, j)),
        out_shape=jax.ShapeDtypeStruct((d, n_out), w16.dtype),
        compiler_params=_cparams(("parallel", "parallel")),
        name="w_in_repack",
    )(w16, w16)


ACT_NONE, ACT_SILU, ACT_SIGMOID = 0, 1, 2
_ACT_FN = {ACT_NONE: lambda v: v, ACT_SILU: _silu, ACT_SIGMOID: _sigmoid}


def _in_proj_body(tile_acts, x_ref, g_ref, wb_ref, ws_ref, z_ref, zs_ref, h_ref):
    j = pl.program_id(1)

    @pl.when(j == 0)
    def _():
        hb = _rms(x_ref[...], g_ref[...]).astype(BF16)
        h_ref[...] = hb
        zs_ref[...] = _dot(hb, ws_ref[...])

    for act, fn in _ACT_FN.items():
        tiles = [jt for jt, a in enumerate(tile_acts) if a == act]
        if not tiles:
            continue
        hit = functools.reduce(lambda p, q: p | q, [j == jt for jt in tiles])

        @pl.when(hit)
        def _(fn=fn):
            z_ref[...] = fn(_dot(h_ref[...], wb_ref[...])).astype(BF16)


def _in_proj(x2d, g_mix, w_big, w_small, segments, tm=1024, tn=2048):
    t, d = x2d.shape
    n = w_big.shape[1]
    assert all(lo % tn == 0 for lo, _ in segments) and n % tn == 0
    starts = [lo // tn for lo, _ in segments] + [n // tn]
    tile_acts = tuple(act for (_, act), a, b in zip(segments, starts, starts[1:]) for _ in range(b - a))
    body = functools.partial(_in_proj_body, tile_acts)
    return pl.pallas_call(
        body,
        grid=(t // tm, n // tn),
        in_specs=[
            pl.BlockSpec((tm, d), lambda i, j: (i, 0)),
            pl.BlockSpec((1, d), lambda i, j: (0, 0)),
            pl.BlockSpec((d, tn), lambda i, j: (0, j)),
            pl.BlockSpec((d, LANES), lambda i, j: (0, 0)),
        ],
        out_specs=[
            pl.BlockSpec((tm, tn), lambda i, j: (i, j)),
            pl.BlockSpec((tm, LANES), lambda i, j: (i, 0)),
        ],
        out_shape=[
            jax.ShapeDtypeStruct((t, n), BF16),
            jax.ShapeDtypeStruct((t, LANES), F32),
        ],
        scratch_shapes=[pltpu.VMEM((tm, d), BF16)],
        compiler_params=_cparams(("parallel", "arbitrary")),
        name="in_proj",
    )(x2d, g_mix, w_big, w_small)


def _gates_body(zs_ref, aneg_ref, dtb_ref, g_ref, gt_ref):
    zs = zs_ref[...]
    lane = lax.broadcasted_iota(jnp.int32, zs.shape, 1)
    row = lax.broadcasted_iota(jnp.int32, zs.shape, 0) % DN_CHUNK
    is_a = (lane >= SM_A) & (lane < SM_B)
    is_b = (lane >= SM_B) & (lane < SM_GL)
    g = jnp.where(is_a, aneg_ref[...] * _softplus(zs + dtb_ref[...]), 0.0)
    gcum = _chunk_cumsum(g, row, DN_CHUNK)
    gtot = gcum + _chunk_rev_cumsum(g, row, DN_CHUNK) - g
    beta = _sigmoid(zs)
    out = jnp.where(is_a, gcum, jnp.where(is_b, beta, 0.0))
    out = out + pltpu.roll(jnp.where(is_a, gtot, 0.0), SM_GL - SM_A, 1)
    g_ref[...] = out
    gt_ref[...] = out.T


def _gates(zs3, a_neg, dt_bias):
    b, s, _ = zs3.shape
    return pl.pallas_call(
        _gates_body,
        grid=(b,),
        in_specs=[
            pl.BlockSpec((None, s, LANES), lambda i: (i, 0, 0)),
            pl.BlockSpec((1, LANES), lambda i: (0, 0)),
            pl.BlockSpec((1, LANES), lambda i: (0, 0)),
        ],
        out_specs=[
            pl.BlockSpec((None, s, LANES), lambda i: (i, 0, 0)),
            pl.BlockSpec((None, LANES, s), lambda i: (i, 0, 0)),
        ],
        out_shape=[
            jax.ShapeDtypeStruct((b, s, LANES), F32),
            jax.ShapeDtypeStruct((b, LANES, s), F32),
        ],
        compiler_params=_cparams(("parallel",)),
        name="dn_gates",
    )(zs3, a_neg, dt_bias)


def _gla_body(hb, scale, q_ref, k_ref, v_ref, lr_ref, w2_ref, b_ref, nrm_ref, sg_ref, ga_ref, o_ref, st_ref):
    s = q_ref.shape[0]
    dk = q_ref.shape[1] // hb
    dv = v_ref.shape[1] // hb
    per = TILE // CHUNK
    heads = range(hb)
    kcol = [pl.ds(h * dk, dk) for h in heads]
    vcol = [pl.ds(h * dv, dv) for h in heads]

    @pl.when(pl.program_id(1) == 0)
    def _():
        st_ref[...] = jnp.zeros_like(st_ref)

    row = lax.broadcasted_iota(jnp.int32, (TILE, dk), 0) % CHUNK
    ri = lax.broadcasted_iota(jnp.int32, (TILE, TILE), 0)
    ci = lax.broadcasted_iota(jnp.int32, (TILE, TILE), 1)
    causal = ((ri // CHUNK) == (ci // CHUNK)) & (ci <= ri)
    cross = ((ri // CHUNK) == (ci // CHUNK) + 1) & ((ri // (2 * CHUNK)) == (ci // (2 * CHUNK)))

    def tile(t, carry):
        base = pl.multiple_of(t * TILE, TILE)
        r = pl.ds(base, TILE)
        lrb = lr_ref[r, :].astype(BF16)
        q = [q_ref[r, kcol[h]].astype(F32) * scale for h in heads]
        k = [k_ref[r, kcol[h]].astype(F32) for h in heads]
        v = [v_ref[r, vcol[h]] for h in heads]
        pre = [_dot(lrb, w2_ref[:, kcol[h]]) + b_ref[:, kcol[h]] for h in heads]
        bc = [_chunk_cumsum(_log_sigmoid(x) * (1.0 / GLA_TAU), row) for x in pre]
        q_dec = [q[h] * jnp.exp(bc[h]) for h in heads]
        q_in = [x.astype(BF16) for x in q_dec]
        k_in = [(k[h] * jnp.exp(-bc[h])).astype(BF16) for h in heads]
        bl = [jnp.concatenate([jnp.broadcast_to(bc[h][(c + 1) * CHUNK - 1:(c + 1) * CHUNK, :], (CHUNK, dk))
                               for c in range(per)], axis=0) for h in heads]
        k_dec = [k[h] * jnp.exp(bl[h] - bc[h]) for h in heads]
        a = [(jnp.where(causal, _dot_nt(q_in[h], k_in[h]), 0.0)
              + jnp.where(cross, _dot_nt(q_in[h], k_dec[h].astype(BF16)), 0.0)).astype(BF16) for h in heads]
        o_intra = [_dot(a[h], v[h]) for h in heads]
        for m in range(per // 2):
            p0, p1, p2 = 2 * m * CHUNK, (2 * m + 1) * CHUNK, (2 * m + 2) * CHUNK
            f1 = [jnp.exp(bc[h][p1 - 1:p1, :]) for h in heads]
            f2 = [jnp.exp(bc[h][p2 - 1:p2, :]) for h in heads]
            qs = [jnp.concatenate([q_in[h][p0:p1, :], (q_dec[h][p1:p2, :] * f1[h]).astype(BF16)], axis=0)
                  for h in heads]
            kd = [jnp.concatenate([(k_dec[h][p0:p1, :] * f2[h]).astype(BF16), k_dec[h][p1:p2, :].astype(BF16)],
                                  axis=0) for h in heads]
            st = [st_ref[h] for h in heads]
            o = [o_intra[h][p0:p2, :] + _dot_nt(qs[h], st[h].astype(BF16)) for h in heads]
            for h in heads:
                st_ref[h] = st[h] * (f1[h] * f2[h]) + _dot_tn(v[h][p0:p2, :], kd[h])
            rc = pl.ds(base + p0, 2 * CHUNK)
            for h in heads:
                gate = sg_ref[rc, vcol[h]].astype(F32) * ga_ref[rc, vcol[h]].astype(F32)
                o_ref[rc, vcol[h]] = (_rms(o[h], nrm_ref[...]) * gate).astype(o_ref.dtype)
        return carry

    lax.fori_loop(0, s // TILE, tile, 0)


def _gla(z3, zs3, w2p, gla_b, gla_norm, d_model, gate_cols, ts=512):
    hb = GLA_HEADS
    b, s, _ = z3.shape
    dk = d_model // (2 * GLA_HEADS)
    dv = d_model // GLA_HEADS
    assert s % ts == 0 and ts % TILE == 0
    qk_cols = GLA_HEADS * dk
    kw, vw = hb * dk, hb * dv
    sg_col, ga_col = gate_cols
    assert sg_col % vw == 0 and ga_col % vw == 0
    body = functools.partial(_gla_body, hb, dk ** -0.5)
    return pl.pallas_call(
        body,
        grid=(b, s // ts),
        in_specs=[
            pl.BlockSpec((None, ts, kw), lambda i, t: (i, t, 0)),
            pl.BlockSpec((None, ts, kw), lambda i, t: (i, t, qk_cols // kw)),
            pl.BlockSpec((None, ts, vw), lambda i, t: (i, t, 2 * qk_cols // vw)),
            pl.BlockSpec((None, ts, LANES), lambda i, t: (i, t, 0)),
            pl.BlockSpec((LANES, kw), lambda i, t: (0, 0)),
            pl.BlockSpec((1, kw), lambda i, t: (0, 0)),
            pl.BlockSpec((1, dv), lambda i, t: (0, 0)),
            pl.BlockSpec((None, ts, vw), lambda i, t: (i, t, sg_col // vw)),
            pl.BlockSpec((None, ts, vw), lambda i, t: (i, t, ga_col // vw)),
        ],
        out_specs=pl.BlockSpec((None, ts, vw), lambda i, t: (i, t, 0)),
        out_shape=jax.ShapeDtypeStruct((b, s, GLA_HEADS * dv), BF16),
        scratch_shapes=[pltpu.VMEM((hb, dv, dk), F32)],
        compiler_params=_cparams(("parallel", "arbitrary")),
        name="gla_mixer",
    )(z3, z3, z3, zs3, w2p, gla_b, gla_norm, z3, z3)


def _conv_silu(x_ref, w_ref, cols, base, n_rows, first, buf):
    pbase = pl.multiple_of(jnp.maximum(base - 8, 0), 8)
    buf[pl.ds(0, 8), :] = jnp.where(first, 0.0, x_ref[pl.ds(pbase, 8), cols].astype(F32))
    buf[pl.ds(8, n_rows), :] = x_ref[pl.ds(base, n_rows), cols].astype(F32)
    w = 0.5 * w_ref[:, cols]
    acc = None
    for d in range(DN_CONV):
        term = buf[pl.ds(8 - d, n_rows), :] * w[DN_CONV - 1 - d:DN_CONV - d, :]
        acc = term if acc is None else acc + term
    return _silu_half(acc)


def _l2n(t):
    return t * lax.rsqrt(jnp.sum(t * t, axis=-1, keepdims=True) + EPS)


def _lane_pick(x, idx):
    lane = lax.broadcasted_iota(jnp.int32, x.shape, 1)
    return jnp.sum(jnp.where(lane == idx, x, 0.0), axis=-1, keepdims=True)


def _bdot(a, b):
    return lax.dot_general(a, b, (((2,), (1,)), ((0,), (0,))), preferred_element_type=F32)


def _bdot_nt(a, b):
    return lax.dot_general(a, b, (((2,), (2,)), ((0,), (0,))), preferred_element_type=F32)


def _dn_body(hb, qscale, q_ref, k_ref, v_ref, cq_ref, ck_ref, cv_ref, g_ref, gt_ref, nrm_ref, sz_ref, gb_ref,
             o_ref, kp_s, np_s, qp_s, op_s, fl_s, st_s, cbuf):
    s = q_ref.shape[0]
    n_half = DN_TILE // HALF
    per_tile = DN_TILE // DN_CHUNK
    head0 = pl.program_id(1) * hb

    ri = lax.broadcasted_iota(jnp.int32, (HALF, HALF), 0)
    ci = lax.broadcasted_iota(jnp.int32, (HALF, HALF), 1)
    incl = ci <= ri
    sub = (ri // SUB) == (ci // SUB)
    strict_in = sub & (ci < ri)
    strict_off = (~sub) & (ci < ri)
    n_sq = SUB.bit_length() - 2
    b3 = lambda x: x.reshape(n_half, HALF, x.shape[-1])

    n_tiles = s // DN_TILE
    n_units = hb * n_tiles

    def unit_coords(u):
        u = jnp.asarray(u, jnp.int32)
        i = lax.div(u, n_tiles)
        t = lax.rem(u, n_tiles)
        return i, t, pl.ds(pl.multiple_of(i * LANES, LANES), LANES)

    def prep(u):
        i, t, cols = unit_coords(u)
        head = head0 + i
        base = pl.multiple_of(t * DN_TILE, DN_TILE)
        rows = pl.ds(base, DN_TILE)
        first = t == 0
        q = _l2n(_conv_silu(q_ref, cq_ref, cols, base, DN_TILE, first, cbuf.at[0])) * qscale
        k = _l2n(_conv_silu(k_ref, ck_ref, cols, base, DN_TILE, first, cbuf.at[1]))
        v = _conv_silu(v_ref, cv_ref, cols, base, DN_TILE, first, cbuf.at[2])
        gs = g_ref[rows, :]
        gc = _lane_pick(gs, SM_A + head)
        bt = _lane_pick(gs, SM_B + head)
        gl = _lane_pick(gs, SM_GL + head)
        eg = jnp.exp(gc)
        kb = k * bt
        vb = v * bt
        kbe = kb * eg
        qd = q * eg
        kd = (k * jnp.exp(gl - gc)).astype(BF16)
        flb = jnp.broadcast_to(jnp.exp(gl), (DN_TILE, LANES))
        grow_t = gt_ref[pl.ds(SM_A + head, 1), rows]
        grow = jnp.stack([grow_t[:, j * HALF:(j + 1) * HALF] for j in range(n_half)], axis=0)
        dec = jnp.exp(jnp.where(incl, b3(gc) - grow, -jnp.inf))
        kbf = b3(k).astype(BF16)
        a_full = _bdot_nt(b3(kb).astype(BF16), kbf) * dec
        a = jnp.where(strict_in, a_full, 0.0)
        attn = (_bdot_nt(b3(q).astype(BF16), kbf) * dec).astype(BF16)
        p = -a
        r = p
        for _ in range(n_sq):
            pb = p.astype(BF16)
            p = _bdot(pb, pb)
            r = r + p + _bdot(r.astype(BF16), p.astype(BF16))
        nb = jnp.where(strict_off, a_full, 0.0)
        rb = r.astype(BF16)
        tn = nb + _bdot(rb, nb.astype(BF16))
        r = r - (tn + _bdot(tn.astype(BF16), rb))
        rhs = jnp.concatenate([b3(vb), b3(kbe)], axis=2)
        uw = (rhs + _bdot(r.astype(BF16), rhs.astype(BF16))).astype(BF16)
        aw = _bdot(attn, uw)
        op_s[rows, cols] = aw[:, :, :LANES].reshape(DN_TILE, LANES).astype(op_s.dtype)
        qp_s[rows, cols] = (qd - aw[:, :, LANES:].reshape(DN_TILE, LANES)).astype(qp_s.dtype)
        uw2 = uw.reshape(DN_TILE, 2 * LANES)
        for c in range(per_tile):
            c0 = c * DN_CHUNK
            kn = _dot_tn(kd[c0:c0 + DN_CHUNK, :], uw2[c0:c0 + DN_CHUNK, :])
            cidx = t * per_tile + c
            np_s[i, cidx] = kn[:, :LANES].astype(np_s.dtype)
            kp_s[i, cidx] = kn[:, LANES:].astype(kp_s.dtype)
            fl_s[i, pl.ds(cidx, 1), :] = flb[c0:c0 + 1, :]

    def prep_step(u, carry):
        prep(u)
        return carry

    lax.fori_loop(0, n_units, prep_step, 0)

    st_s[...] = jnp.zeros_like(st_s)

    def scan(c, carry):
        r = pl.ds(pl.multiple_of(c * DN_CHUNK, DN_CHUNK), DN_CHUNK)
        for i in range(hb):
            cols = pl.ds(i * LANES, LANES)
            st = st_s[i]
            stb = st.astype(BF16)
            o = _dot(qp_s[r, cols], stb) + op_s[r, cols].astype(F32)
            st_s[i] = st * fl_s[i, pl.ds(c, 1), :] - _dot(kp_s[i, c], stb) + np_s[i, c]
            gate = sz_ref[r, cols].astype(F32) * gb_ref[r, cols].astype(F32)
            o_ref[r, cols] = (_rms(o, nrm_ref[...]) * gate).astype(o_ref.dtype)
        return carry

    lax.fori_loop(0, s // DN_CHUNK, scan, 0, unroll=SCAN_UNROLL)


def _deltanet(z3, dn_conv, g3, gt3, dn_norm, d_model, qkv_off, gate_cols, hb=4):
    b, s, _ = z3.shape
    dh = d_model // DN_HEADS
    assert dh == LANES and s % DN_TILE == 0
    bw = hb * dh
    nblk = d_model // bw
    n_chunks = s // DN_CHUNK
    body = functools.partial(_dn_body, hb, dh ** -0.5)
    z_spec = lambda part: pl.BlockSpec(
        (None, s, bw), lambda i, h: (i, 0, (qkv_off + part * d_model) // bw + h))
    c_spec = lambda part: pl.BlockSpec((DN_CONV, bw), lambda i, h: (0, part * nblk + h))
    assert all(c % bw == 0 for c in gate_cols)
    gate_spec = lambda c: pl.BlockSpec((None, s, bw), lambda i, h: (i, 0, c // bw + h))
    return pl.pallas_call(
        body,
        grid=(b, nblk),
        in_specs=[
            z_spec(0), z_spec(1), z_spec(2),
            c_spec(0), c_spec(1), c_spec(2),
            pl.BlockSpec((None, s, LANES), lambda i, h: (i, 0, 0)),
            pl.BlockSpec((None, LANES, s), lambda i, h: (i, 0, 0)),
            pl.BlockSpec((1, dh), lambda i, h: (0, 0)),
            gate_spec(gate_cols[0]), gate_spec(gate_cols[1]),
        ],
        out_specs=pl.BlockSpec((None, s, bw), lambda i, h: (i, 0, h)),
        out_shape=jax.ShapeDtypeStruct((b, s, d_model), BF16),
        scratch_shapes=[
            pltpu.VMEM((hb, n_chunks, dh, dh), BF16),
            pltpu.VMEM((hb, n_chunks, dh, dh), F32),
            pltpu.VMEM((s, bw), BF16),
            pltpu.VMEM((s, bw), BF16),
            pltpu.VMEM((hb, n_chunks, LANES), F32),
            pltpu.VMEM((hb, dh, dh), F32),
            pltpu.VMEM((3, DN_TILE + 8, LANES), F32),
        ],
        compiler_params=_cparams(("parallel", "parallel")),
        name="deltanet_mixer",
    )(z3, z3, z3, dn_conv, dn_conv, dn_conv, g3, gt3, dn_norm, z3, z3)


MERGE_SLOTS = 3


def _merge_body(og_hbm, od_hbm, x_hbm, w_ref, g_ref, x1_ref, h2_ref, og_buf, od_buf, x_buf, sem):
    s = pl.program_id(0)
    n = pl.num_programs(0)
    tm = x1_ref.shape[0]
    ahead = MERGE_SLOTS - 1

    def copies(step):
        slot = lax.rem(step, MERGE_SLOTS)
        rows = pl.ds(pl.multiple_of(step * tm, tm), tm)
        return (pltpu.make_async_copy(og_hbm.at[rows, :], og_buf.at[slot], sem.at[0, slot]),
                pltpu.make_async_copy(od_hbm.at[rows, :], od_buf.at[slot], sem.at[1, slot]),
                pltpu.make_async_copy(x_hbm.at[rows, :], x_buf.at[slot], sem.at[2, slot]))

    @pl.when(s == 0)
    def _():
        for k in range(ahead):
            for c in copies(k):
                c.start()

    @pl.when(s + ahead < n)
    def _():
        for c in copies(s + ahead):
            c.start()

    for c in copies(s):
        c.wait()
    slot = lax.rem(s, MERGE_SLOTS)
    mixed = og_buf[slot].astype(F32) + od_buf[slot].astype(F32)
    x1 = x_buf[slot] + _dot(mixed.astype(BF16), w_ref[...])
    x1_ref[...] = x1
    h2_ref[...] = _rms(x1, g_ref[...]).astype(BF16)


def _merge(o_gla, o_dn, x2d, w_out, g_mlp, tm=512):
    t, d = x2d.shape
    assert t // tm >= MERGE_SLOTS - 1
    row = lambda c: pl.BlockSpec((tm, d), lambda i, c=c: (i, c))
    return pl.pallas_call(
        _merge_body,
        grid=(t // tm,),
        in_specs=[
            pl.BlockSpec(memory_space=pl.ANY), pl.BlockSpec(memory_space=pl.ANY), pl.BlockSpec(memory_space=pl.ANY),
            pl.BlockSpec((d, d), lambda i: (0, 0), pipeline_mode=pl.Buffered(1)),
            pl.BlockSpec((1, d), lambda i: (0, 0)),
        ],
        out_specs=[row(0), row(0)],
        out_shape=[jax.ShapeDtypeStruct((t, d), F32), jax.ShapeDtypeStruct((t, d), BF16)],
        scratch_shapes=[
            pltpu.VMEM((MERGE_SLOTS, tm, d), BF16),
            pltpu.VMEM((MERGE_SLOTS, tm, d), BF16),
            pltpu.VMEM((MERGE_SLOTS, tm, d), F32),
            pltpu.SemaphoreType.DMA((3, MERGE_SLOTS)),
        ],
        compiler_params=_cparams(("arbitrary",)),
        name="merge_out_proj",
    )(o_gla, o_dn, x2d, w_out, g_mlp)


def _mlp_body(h_ref, wu_ref, wd_ref, x_ref, o_ref):
    @pl.when(pl.program_id(1) == 0)
    def _():
        o_ref[...] = x_ref[...]

    mid = jnp.square(jnp.maximum(_dot(h_ref[...], wu_ref[...]), 0.0)).astype(BF16)
    o_ref[...] += _dot(mid, wd_ref[...])


def _mlp(h2, w_up, w_down, x1, tm=512, tf=1024):
    t, d = x1.shape
    ff = w_up.shape[1]
    return pl.pallas_call(
        _mlp_body,
        grid=(t // tm, ff // tf),
        in_specs=[
            pl.BlockSpec((tm, d), lambda i, j: (i, 0)),
            pl.BlockSpec((d, tf), lambda i, j: (0, j)),
            pl.BlockSpec((tf, d), lambda i, j: (j, 0)),
            pl.BlockSpec((tm, d), lambda i, j: (i, 0)),
        ],
        out_specs=pl.BlockSpec((tm, d), lambda i, j: (i, 0)),
        out_shape=jax.ShapeDtypeStruct((t, d), F32),
        compiler_params=_cparams(("parallel", "arbitrary")),
        name="relu2_mlp",
    )(h2, w_up, w_down, x1)


def _ple_body(x_ref, p_ref, wg_ref, wp_ref, gp_ref, gf_ref, o_ref):
    x2 = x_ref[...]
    h3 = _rms(x2, gp_ref[...]).astype(BF16)
    gate = _sigmoid(_dot(h3, wg_ref[...]))
    proj = _dot(p_ref[...].astype(BF16), wp_ref[...])
    o_ref[...] = _rms(x2 + gate * proj, gf_ref[...])


def _ple(x2, p2d, w_gate, w_proj, g_ple, g_final, tm=512):
    t, d = x2.shape
    pd = p2d.shape[1]
    return pl.pallas_call(
        _ple_body,
        grid=(t // tm,),
        in_specs=[
            pl.BlockSpec((tm, d), lambda i: (i, 0)),
            pl.BlockSpec((tm, pd), lambda i: (i, 0)),
            pl.BlockSpec((d, d), lambda i: (0, 0)),
            pl.BlockSpec((pd, d), lambda i: (0, 0)),
            pl.BlockSpec((1, d), lambda i: (0, 0)),
            pl.BlockSpec((1, d), lambda i: (0, 0)),
        ],
        out_specs=pl.BlockSpec((tm, d), lambda i: (i, 0)),
        out_shape=jax.ShapeDtypeStruct((t, d), F32),
        compiler_params=_cparams(("parallel",)),
        name="ple_final_norm",
    )(x2, p2d, w_gate, w_proj, g_ple, g_final)


def _layer(x, p_i, g_mix, w_in, gla_w2, gla_b, gla_norm, dn_conv, dn_a_log, dn_dt_bias, dn_norm,
           w_out, g_mlp, w_up, w_down, g_ple, w_ple_gate, w_ple_proj, g_out):
    b, s, d = x.shape
    t = b * s
    gla_qk = d // 2
    dn_qkv = 3 * d
    o_q, o_k = 0, gla_qk
    o_v = 2 * gla_qk
    o_g = o_v + d
    o_lr = o_g + d
    o_dn = o_lr + GLA_LOWRANK
    o_z = o_dn + dn_qkv
    o_a = o_z + d
    o_b = o_a + DN_HEADS
    o_ga = o_b + DN_HEADS
    o_gb = o_ga + d
    assert (o_q, o_k, o_v) == (0, gla_qk, 2 * gla_qk)
    w_in16 = w_in.astype(BF16)
    cs = lambda lo, n: w_in16[:, lo:lo + n]
    w_big = _repack(w_in16, ((o_lr, GLA_LOWRANK), (o_a, 2 * DN_HEADS)))
    w_small = jnp.concatenate(
        [cs(o_lr, GLA_LOWRANK), cs(o_a, DN_HEADS), cs(o_b, DN_HEADS),
         jnp.zeros((d, LANES - SM_GL), BF16)], axis=1)
    n_g, n_dn, n_z = o_g, o_lr, o_lr + dn_qkv
    n_ga = n_z + d
    n_gb = n_ga + d
    segments = ((0, ACT_NONE), (n_g, ACT_SILU), (n_dn, ACT_NONE), (n_z, ACT_SILU), (n_ga, ACT_SIGMOID))
    qkv_off = n_dn

    x2d = x.reshape(t, d)
    z2d, zs2d = _in_proj(x2d, g_mix.reshape(1, d), w_big, w_small, segments)
    z3 = z2d.reshape(b, s, -1)
    zs3 = zs2d.reshape(b, s, LANES)

    pad = lambda v: jnp.zeros((1, LANES), F32).at[0, SM_A:SM_B].set(v.astype(F32))
    g3, gt3 = _gates(zs3, pad(-jnp.exp(dn_a_log.astype(F32))), pad(dn_dt_bias))

    w2p = jnp.zeros((LANES, gla_qk), F32).at[:GLA_LOWRANK].set(gla_w2).astype(BF16)
    o_gla = _gla(z3, zs3, w2p, gla_b.reshape(1, -1).astype(F32), gla_norm.reshape(1, -1).astype(F32), d,
                 (n_g, n_ga))
    o_dn = _deltanet(z3, dn_conv.astype(F32), g3, gt3, dn_norm.reshape(1, -1).astype(F32), d, qkv_off,
                     (n_z, n_gb))

    x1, h2 = _merge(o_gla.reshape(t, d), o_dn.reshape(t, d), x2d, w_out.astype(BF16), g_mlp.reshape(1, d))
    x2 = _mlp(h2, w_up.astype(BF16), w_down.astype(BF16), x1)
    out = _ple(x2, p_i.reshape(t, -1), w_ple_gate.astype(BF16), w_ple_proj.astype(BF16),
               g_ple.reshape(1, d), g_out.reshape(1, d))
    return out.reshape(b, s, d)


def kernel(x, p, g_mix, w_in, gla_w2, gla_b, gla_norm, dn_conv, dn_a_log, dn_dt_bias, dn_norm,
           w_out, g_mlp, w_up, w_down, g_ple, w_ple_gate, w_ple_proj, g_final):
    depth = w_in.shape[0]
    assert depth == 1, "the final rms_norm is fused into the last layer's kernel"
    return _layer(x, p[0], g_mix[0], w_in[0], gla_w2[0], gla_b[0], gla_norm[0], dn_conv[0],
                  dn_a_log[0], dn_dt_bias[0], dn_norm[0], w_out[0], g_mlp[0], w_up[0], w_down[0],
                  g_ple[0], w_ple_gate[0], w_ple_proj[0], g_final)
```

```python
import functools

import jax
import jax.numpy as jnp
from jax import lax
from jax.experimental import pallas as pl
from jax.experimental.pallas import tpu as pltpu

F32 = jnp.float32
BF16 = jnp.bfloat16

EPS = 1e-6
CHUNK = 64
GLA_HEADS = 4
GLA_LOWRANK = 16
GLA_TAU = 16.0
DN_HEADS = 16
DN_CONV = 4
LANES = 128
TILE = 256
DN_CHUNK = 128
HALF = DN_CHUNK
DN_TILE = 2048
SUB = 64
SCAN_UNROLL = 8
VMEM_LIMIT = 56 * 1024 * 1024

SM_A = GLA_LOWRANK
SM_B = SM_A + DN_HEADS
SM_GL = SM_B + DN_HEADS


def _cparams(sem):
    return pltpu.CompilerParams(dimension_semantics=sem, vmem_limit_bytes=VMEM_LIMIT)


def _dot(a, b):
    return jnp.dot(a, b, preferred_element_type=F32)


def _dot_nt(a, b):
    return lax.dot_general(a, b, (((1,), (1,)), ((), ())), preferred_element_type=F32)


def _dot_tn(a, b):
    return lax.dot_general(a, b, (((0,), (0,)), ((), ())), preferred_element_type=F32)


def _sigmoid(x):
    return 0.5 * jnp.tanh(0.5 * x) + 0.5


def _silu_half(h):
    return h * jnp.tanh(h) + h


def _silu(x):
    return _silu_half(0.5 * x)


def _log_sigmoid(x):
    return jnp.minimum(x, 0.0) - jnp.log(1.0 + jnp.exp(-jnp.abs(x)))


def _softplus(x):
    return jnp.maximum(x, 0.0) + jnp.log(1.0 + jnp.exp(-jnp.abs(x)))


def _rms(x, g):
    return x * lax.rsqrt(jnp.mean(x * x, axis=-1, keepdims=True) + EPS) * g


def _chunk_cumsum(x, row, chunk=CHUNK):
    shift = 1
    while shift < chunk:
        x = x + jnp.where(row >= shift, pltpu.roll(x, shift, 0), 0.0)
        shift *= 2
    return x


def _chunk_rev_cumsum(x, row, chunk=CHUNK):
    n = x.shape[0]
    shift = 1
    while shift < chunk:
        x = x + jnp.where(row < chunk - shift, pltpu.roll(x, n - shift, 0), 0.0)
        shift *= 2
    return x


def _repack_body(shifts, w_ref, nxt_ref, o_ref):
    j = pl.program_id(1)
    tn = o_ref.shape[1]
    for shift in sorted(set(shifts)):
        hit = functools.reduce(lambda p, q: p | q, [j == jt for jt, sh in enumerate(shifts) if sh == shift])

        @pl.when(hit)
        def _(shift=shift):
            if shift == 0:
                o_ref[...] = w_ref[...]
            else:
                o_ref[...] = jnp.concatenate([w_ref[:, shift:], nxt_ref[:, :shift]], axis=1)


def _repack(w16, cuts, tr=1024, tn=2048):
    d, n_src = w16.shape
    n_out = n_src - sum(w for _, w in cuts)
    assert n_out % tn == 0 and d % tr == 0
    shifts = tuple(sum(w for s, w in cuts if s <= jt * tn + sum(w2 for s2, w2 in cuts if s2 < s)) for jt in range(n_out // tn))
    assert max(shifts) < LANES
    assert n_out + shifts[-1] == n_src
    return pl.pallas_call(
        functools.partial(_repack_body, shifts),
        grid=(d // tr, n_out // tn),
        in_specs=[
            pl.BlockSpec((tr, tn), lambda i, j: (i, j)),
            pl.BlockSpec((tr, LANES), lambda i, j: (i, (j + 1) * (tn // LANES))),
        ],
        out_specs=pl.BlockSpec((tr, tn), lambda i, j: (i, j)),
        out_shape=jax.ShapeDtypeStruct((d, n_out), w16.dtype),
        compiler_params=_cparams(("parallel", "parallel")),
        name="w_in_repack",
    )(w16, w16)


ACT_NONE, ACT_SILU, ACT_SIGMOID = 0, 1, 2
_ACT_FN = {ACT_NONE: lambda v: v, ACT_SILU: _silu, ACT_SIGMOID: _sigmoid}


def _in_proj_body(tile_acts, x_ref, g_ref, wb_ref, ws_ref, z_ref, zs_ref, h_ref):
    j = pl.program_id(1)

    @pl.when(j == 0)
    def _():
        hb = _rms(x_ref[...], g_ref[...]).astype(BF16)
        h_ref[...] = hb
        zs_ref[...] = _dot(hb, ws_ref[...])

    for act, fn in _ACT_FN.items():
        tiles = [jt for jt, a in enumerate(tile_acts) if a == act]
        if not tiles:
            continue
        hit = functools.reduce(lambda p, q: p | q, [j == jt for jt in tiles])

        @pl.when(hit)
        def _(fn=fn):
            z_ref[...] = fn(_dot(h_ref[...], wb_ref[...])).astype(BF16)


def _in_proj(x2d, g_mix, w_big, w_small, segments, tm=1024, tn=2048):
    t, d = x2d.shape
    n = w_big.shape[1]
    assert all(lo % tn == 0 for lo, _ in segments) and n % tn == 0
    starts = [lo // tn for lo, _ in segments] + [n // tn]
    tile_acts = tuple(act for (_, act), a, b in zip(segments, starts, starts[1:]) for _ in range(b - a))
    body = functools.partial(_in_proj_body, tile_acts)
    return pl.pallas_call(
        body,
        grid=(t // tm, n // tn),
        in_specs=[
            pl.BlockSpec((tm, d), lambda i, j: (i, 0)),
            pl.BlockSpec((1, d), lambda i, j: (0, 0)),
            pl.BlockSpec((d, tn), lambda i, j: (0, j)),
            pl.BlockSpec((d, LANES), lambda i, j: (0, 0)),
        ],
        out_specs=[
            pl.BlockSpec((tm, tn), lambda i, j: (i, j)),
            pl.BlockSpec((tm, LANES), lambda i, j: (i, 0)),
        ],
        out_shape=[
            jax.ShapeDtypeStruct((t, n), BF16),
            jax.ShapeDtypeStruct((t, LANES), F32),
        ],
        scratch_shapes=[pltpu.VMEM((tm, d), BF16)],
        compiler_params=_cparams(("parallel", "arbitrary")),
        name="in_proj",
    )(x2d, g_mix, w_big, w_small)


def _gates_body(zs_ref, aneg_ref, dtb_ref, g_ref, gt_ref):
    zs = zs_ref[...]
    lane = lax.broadcasted_iota(jnp.int32, zs.shape, 1)
    row = lax.broadcasted_iota(jnp.int32, zs.shape, 0) % DN_CHUNK
    is_a = (lane >= SM_A) & (lane < SM_B)
    is_b = (lane >= SM_B) & (lane < SM_GL)
    g = jnp.where(is_a, aneg_ref[...] * _softplus(zs + dtb_ref[...]), 0.0)
    gcum = _chunk_cumsum(g, row, DN_CHUNK)
    gtot = gcum + _chunk_rev_cumsum(g, row, DN_CHUNK) - g
    beta = _sigmoid(zs)
    out = jnp.where(is_a, gcum, jnp.where(is_b, beta, 0.0))
    out = out + pltpu.roll(jnp.where(is_a, gtot, 0.0), SM_GL - SM_A, 1)
    g_ref[...] = out
    gt_ref[...] = out.T


def _gates(zs3, a_neg, dt_bias):
    b, s, _ = zs3.shape
    return pl.pallas_call(
        _gates_body,
        grid=(b,),
        in_specs=[
            pl.BlockSpec((None, s, LANES), lambda i: (i, 0, 0)),
            pl.BlockSpec((1, LANES), lambda i: (0, 0)),
            pl.BlockSpec((1, LANES), lambda i: (0, 0)),
        ],
        out_specs=[
            pl.BlockSpec((None, s, LANES), lambda i: (i, 0, 0)),
            pl.BlockSpec((None, LANES, s), lambda i: (i, 0, 0)),
        ],
        out_shape=[
            jax.ShapeDtypeStruct((b, s, LANES), F32),
            jax.ShapeDtypeStruct((b, LANES, s), F32),
        ],
        compiler_params=_cparams(("parallel",)),
        name="dn_gates",
    )(zs3, a_neg, dt_bias)


def _gla_body(hb, scale, q_ref, k_ref, v_ref, lr_ref, w2_ref, b_ref, nrm_ref, sg_ref, ga_ref, o_ref, st_ref):
    s = q_ref.shape[0]
    dk = q_ref.shape[1] // hb
    dv = v_ref.shape[1] // hb
    per = TILE // CHUNK
    heads = range(hb)
    kcol = [pl.ds(h * dk, dk) for h in heads]
    vcol = [pl.ds(h * dv, dv) for h in heads]

    @pl.when(pl.program_id(1) == 0)
    def _():
        st_ref[...] = jnp.zeros_like(st_ref)

    row = lax.broadcasted_iota(jnp.int32, (TILE, dk), 0) % CHUNK
    ri = lax.broadcasted_iota(jnp.int32, (TILE, TILE), 0)
    ci = lax.broadcasted_iota(jnp.int32, (TILE, TILE), 1)
    causal = ((ri // CHUNK) == (ci // CHUNK)) & (ci <= ri)
    cross = ((ri // CHUNK) == (ci // CHUNK) + 1) & ((ri // (2 * CHUNK)) == (ci // (2 * CHUNK)))

    def tile(t, carry):
        base = pl.multiple_of(t * TILE, TILE)
        r = pl.ds(base, TILE)
        lrb = lr_ref[r, :].astype(BF16)
        q = [q_ref[r, kcol[h]].astype(F32) * scale for h in heads]
        k = [k_ref[r, kcol[h]].astype(F32) for h in heads]
        v = [v_ref[r, vcol[h]] for h in heads]
        pre = [_dot(lrb, w2_ref[:, kcol[h]]) + b_ref[:, kcol[h]] for h in heads]
        bc = [_chunk_cumsum(_log_sigmoid(x) * (1.0 / GLA_TAU), row) for x in pre]
        q_dec = [q[h] * jnp.exp(bc[h]) for h in heads]
        q_in = [x.astype(BF16) for x in q_dec]
        k_in = [(k[h] * jnp.exp(-bc[h])).astype(BF16) for h in heads]
        bl = [jnp.concatenate([jnp.broadcast_to(bc[h][(c + 1) * CHUNK - 1:(c + 1) * CHUNK, :], (CHUNK, dk))
                               for c in range(per)], axis=0) for h in heads]
        k_dec = [k[h] * jnp.exp(bl[h] - bc[h]) for h in heads]
        a = [(jnp.where(causal, _dot_nt(q_in[h], k_in[h]), 0.0)
              + jnp.where(cross, _dot_nt(q_in[h], k_dec[h].astype(BF16)), 0.0)).astype(BF16) for h in heads]
        o_intra = [_dot(a[h], v[h]) for h in heads]
        for m in range(per // 2):
            p0, p1, p2 = 2 * m * CHUNK, (2 * m + 1) * CHUNK, (2 * m + 2) * CHUNK
            f1 = [jnp.exp(bc[h][p1 - 1:p1, :]) for h in heads]
            f2 = [jnp.exp(bc[h][p2 - 1:p2, :]) for h in heads]
            qs = [jnp.concatenate([q_in[h][p0:p1, :], (q_dec[h][p1:p2, :] * f1[h]).astype(BF16)], axis=0)
                  for h in heads]
            kd = [jnp.concatenate([(k_dec[h][p0:p1, :] * f2[h]).astype(BF16), k_dec[h][p1:p2, :].astype(BF16)],
                                  axis=0) for h in heads]
            st = [st_ref[h] for h in heads]
            o = [o_intra[h][p0:p2, :] + _dot_nt(qs[h], st[h].astype(BF16)) for h in heads]
            for h in heads:
                st_ref[h] = st[h] * (f1[h] * f2[h]) + _dot_tn(v[h][p0:p2, :], kd[h])
            rc = pl.ds(base + p0, 2 * CHUNK)
            for h in heads:
                gate = sg_ref[rc, vcol[h]].astype(F32) * ga_ref[rc, vcol[h]].astype(F32)
                o_ref[rc, vcol[h]] = (_rms(o[h], nrm_ref[...]) * gate).astype(o_ref.dtype)
        return carry

    lax.fori_loop(0, s // TILE, tile, 0)


def _gla(z3, zs3, w2p, gla_b, gla_norm, d_model, gate_cols, ts=512):
    hb = GLA_HEADS
    b, s, _ = z3.shape
    dk = d_model // (2 * GLA_HEADS)
    dv = d_model // GLA_HEADS
    assert s % ts == 0 and ts % TILE == 0
    qk_cols = GLA_HEADS * dk
    kw, vw = hb * dk, hb * dv
    sg_col, ga_col = gate_cols
    assert sg_col % vw == 0 and ga_col % vw == 0
    body = functools.partial(_gla_body, hb, dk ** -0.5)
    return pl.pallas_call(
        body,
        grid=(b, s // ts),
        in_specs=[
            pl.BlockSpec((None, ts, kw), lambda i, t: (i, t, 0)),
            pl.BlockSpec((None, ts, kw), lambda i, t: (i, t, qk_cols // kw)),
            pl.BlockSpec((None, ts, vw), lambda i, t: (i, t, 2 * qk_cols // vw)),
            pl.BlockSpec((None, ts, LANES), lambda i, t: (i, t, 0)),
            pl.BlockSpec((LANES, kw), lambda i, t: (0, 0)),
            pl.BlockSpec((1, kw), lambda i, t: (0, 0)),
            pl.BlockSpec((1, dv), lambda i, t: (0, 0)),
            pl.BlockSpec((None, ts, vw), lambda i, t: (i, t, sg_col // vw)),
            pl.BlockSpec((None, ts, vw), lambda i, t: (i, t, ga_col // vw)),
        ],
        out_specs=pl.BlockSpec((None, ts, vw), lambda i, t: (i, t, 0)),
        out_shape=jax.ShapeDtypeStruct((b, s, GLA_HEADS * dv), BF16),
        scratch_shapes=[pltpu.VMEM((hb, dv, dk), F32)],
        compiler_params=_cparams(("parallel", "arbitrary")),
        name="gla_mixer",
    )(z3, z3, z3, zs3, w2p, gla_b, gla_norm, z3, z3)


def _conv_silu(x_ref, w_ref, cols, base, n_rows, first, buf):
    pbase = pl.multiple_of(jnp.maximum(base - 8, 0), 8)
    buf[pl.ds(0, 8), :] = jnp.where(first, 0.0, x_ref[pl.ds(pbase, 8), cols].astype(F32))
    buf[pl.ds(8, n_rows), :] = x_ref[pl.ds(base, n_rows), cols].astype(F32)
    w = 0.5 * w_ref[:, cols]
    acc = None
    for d in range(DN_CONV):
        term = buf[pl.ds(8 - d, n_rows), :] * w[DN_CONV - 1 - d:DN_CONV - d, :]
        acc = term if acc is None else acc + term
    return _silu_half(acc)


def _l2n(t):
    return t * lax.rsqrt(jnp.sum(t * t, axis=-1, keepdims=True) + EPS)


def _lane_pick(x, idx):
    lane = lax.broadcasted_iota(jnp.int32, x.shape, 1)
    return jnp.sum(jnp.where(lane == idx, x, 0.0), axis=-1, keepdims=True)


def _bdot(a, b):
    return lax.dot_general(a, b, (((2,), (1,)), ((0,), (0,))), preferred_element_type=F32)


def _bdot_nt(a, b):
    return lax.dot_general(a, b, (((2,), (2,)), ((0,), (0,))), preferred_element_type=F32)


def _dn_body(hb, qscale, q_ref, k_ref, v_ref, cq_ref, ck_ref, cv_ref, g_ref, gt_ref, nrm_ref, sz_ref, gb_ref,
             o_ref, kp_s, np_s, qp_s, op_s, fl_s, st_s, cbuf):
    s = q_ref.shape[0]
    n_half = DN_TILE // HALF
    per_tile = DN_TILE // DN_CHUNK
    head0 = pl.program_id(1) * hb

    ri = lax.broadcasted_iota(jnp.int32, (HALF, HALF), 0)
    ci = lax.broadcasted_iota(jnp.int32, (HALF, HALF), 1)
    incl = ci <= ri
    sub = (ri // SUB) == (ci // SUB)
    strict_in = sub & (ci < ri)
    strict_off = (~sub) & (ci < ri)
    n_sq = SUB.bit_length() - 2
    b3 = lambda x: x.reshape(n_half, HALF, x.shape[-1])

    n_tiles = s // DN_TILE
    n_units = hb * n_tiles

    def unit_coords(u):
        u = jnp.asarray(u, jnp.int32)
        i = lax.div(u, n_tiles)
        t = lax.rem(u, n_tiles)
        return i, t, pl.ds(pl.multiple_of(i * LANES, LANES), LANES)

    def prep(u):
        i, t, cols = unit_coords(u)
        head = head0 + i
        base = pl.multiple_of(t * DN_TILE, DN_TILE)
        rows = pl.ds(base, DN_TILE)
        first = t == 0
        q = _l2n(_conv_silu(q_ref, cq_ref, cols, base, DN_TILE, first, cbuf.at[0])) * qscale
        k = _l2n(_conv_silu(k_ref, ck_ref, cols, base, DN_TILE, first, cbuf.at[1]))
        v = _conv_silu(v_ref, cv_ref, cols, base, DN_TILE, first, cbuf.at[2])
        gs = g_ref[rows, :]
        gc = _lane_pick(gs, SM_A + head)
        bt = _lane_pick(gs, SM_B + head)
        gl = _lane_pick(gs, SM_GL + head)
        eg = jnp.exp(gc)
        kb = k * bt
        vb = v * bt
        kbe = kb * eg
        qd = q * eg
        kd = (k * jnp.exp(gl - gc)).astype(BF16)
        flb = jnp.broadcast_to(jnp.exp(gl), (DN_TILE, LANES))
        grow_t = gt_ref[pl.ds(SM_A + head, 1), rows]
        grow = jnp.stack([grow_t[:, j * HALF:(j + 1) * HALF] for j in range(n_half)], axis=0)
        dec = jnp.exp(jnp.where(incl, b3(gc) - grow, -jnp.inf))
        kbf = b3(k).astype(BF16)
        a_full = _bdot_nt(b3(kb).astype(BF16), kbf) * dec
        a = jnp.where(strict_in, a_full, 0.0)
        attn = (_bdot_nt(b3(q).astype(BF16), kbf) * dec).astype(BF16)
        p = -a
        r = p
        for _ in range(n_sq):
            pb = p.astype(BF16)
            p = _bdot(pb, pb)
            r = r + p + _bdot(r.astype(BF16), p.astype(BF16))
        nb = jnp.where(strict_off, a_full, 0.0)
        rb = r.astype(BF16)
        tn = nb + _bdot(rb, nb.astype(BF16))
        r = r - (tn + _bdot(tn.astype(BF16), rb))
        rhs = jnp.concatenate([b3(vb), b3(kbe)], axis=2)
        uw = (rhs + _bdot(r.astype(BF16), rhs.astype(BF16))).astype(BF16)
        aw = _bdot(attn, uw)
        op_s[rows, cols] = aw[:, :, :LANES].reshape(DN_TILE, LANES).astype(op_s.dtype)
        qp_s[rows, cols] = (qd - aw[:, :, LANES:].reshape(DN_TILE, LANES)).astype(qp_s.dtype)
        uw2 = uw.reshape(DN_TILE, 2 * LANES)
        for c in range(per_tile):
            c0 = c * DN_CHUNK
            kn = _dot_tn(kd[c0:c0 + DN_CHUNK, :], uw2[c0:c0 + DN_CHUNK, :])
            cidx = t * per_tile + c
            np_s[i, cidx] = kn[:, :LANES].astype(np_s.dtype)
            kp_s[i, cidx] = kn[:, LANES:].astype(kp_s.dtype)
            fl_s[i, pl.ds(cidx, 1), :] = flb[c0:c0 + 1, :]

    def prep_step(u, carry):
        prep(u)
        return carry

    lax.fori_loop(0, n_units, prep_step, 0)

    st_s[...] = jnp.zeros_like(st_s)

    def scan(c, carry):
        r = pl.ds(pl.multiple_of(c * DN_CHUNK, DN_CHUNK), DN_CHUNK)
        for i in range(hb):
            cols = pl.ds(i * LANES, LANES)
            st = st_s[i]
            stb = st.astype(BF16)
            o = _dot(qp_s[r, cols], stb) + op_s[r, cols].astype(F32)
            st_s[i] = st * fl_s[i, pl.ds(c, 1), :] - _dot(kp_s[i, c], stb) + np_s[i, c]
            gate = sz_ref[r, cols].astype(F32) * gb_ref[r, cols].astype(F32)
            o_ref[r, cols] = (_rms(o, nrm_ref[...]) * gate).astype(o_ref.dtype)
        return carry

    lax.fori_loop(0, s // DN_CHUNK, scan, 0, unroll=SCAN_UNROLL)


def _deltanet(z3, dn_conv, g3, gt3, dn_norm, d_model, qkv_off, gate_cols, hb=4):
    b, s, _ = z3.shape
    dh = d_model // DN_HEADS
    assert dh == LANES and s % DN_TILE == 0
    bw = hb * dh
    nblk = d_model // bw
    n_chunks = s // DN_CHUNK
    body = functools.partial(_dn_body, hb, dh ** -0.5)
    z_spec = lambda part: pl.BlockSpec(
        (None, s, bw), lambda i, h: (i, 0, (qkv_off + part * d_model) // bw + h))
    c_spec = lambda part: pl.BlockSpec((DN_CONV, bw), lambda i, h: (0, part * nblk + h))
    assert all(c % bw == 0 for c in gate_cols)
    gate_spec = lambda c: pl.BlockSpec((None, s, bw), lambda i, h: (i, 0, c // bw + h))
    return pl.pallas_call(
        body,
        grid=(b, nblk),
        in_specs=[
            z_spec(0), z_spec(1), z_spec(2),
            c_spec(0), c_spec(1), c_spec(2),
            pl.BlockSpec((None, s, LANES), lambda i, h: (i, 0, 0)),
            pl.BlockSpec((None, LANES, s), lambda i, h: (i, 0, 0)),
            pl.BlockSpec((1, dh), lambda i, h: (0, 0)),
            gate_spec(gate_cols[0]), gate_spec(gate_cols[1]),
        ],
        out_specs=pl.BlockSpec((None, s, bw), lambda i, h: (i, 0, h)),
        out_shape=jax.ShapeDtypeStruct((b, s, d_model), BF16),
        scratch_shapes=[
            pltpu.VMEM((hb, n_chunks, dh, dh), BF16),
            pltpu.VMEM((hb, n_chunks, dh, dh), F32),
            pltpu.VMEM((s, bw), BF16),
            pltpu.VMEM((s, bw), BF16),
            pltpu.VMEM((hb, n_chunks, LANES), F32),
            pltpu.VMEM((hb, dh, dh), F32),
            pltpu.VMEM((3, DN_TILE + 8, LANES), F32),
        ],
        compiler_params=_cparams(("parallel", "parallel")),
        name="deltanet_mixer",
    )(z3, z3, z3, dn_conv, dn_conv, dn_conv, g3, gt3, dn_norm, z3, z3)


MERGE_SLOTS = 3


def _merge_body(og_hbm, od_hbm, x_ref, w_ref, g_ref, x1_ref, h2_ref, og_buf, od_buf, sem):
    s = pl.program_id(0)
    n = pl.num_programs(0)
    tm = x_ref.shape[0]
    ahead = MERGE_SLOTS - 1

    def copies(step):
        slot = lax.rem(step, MERGE_SLOTS)
        rows = pl.ds(pl.multiple_of(step * tm, tm), tm)
        return (pltpu.make_async_copy(og_hbm.at[rows, :], og_buf.at[slot], sem.at[0, slot]),
                pltpu.make_async_copy(od_hbm.at[rows, :], od_buf.at[slot], sem.at[1, slot]))

    @pl.when(s == 0)
    def _():
        for k in range(ahead):
            for stream, c in enumerate(copies(k)):
                c.start(priority=stream)

    @pl.when(s + ahead < n)
    def _():
        for stream, c in enumerate(copies(s + ahead)):
            c.start(priority=stream)

    for c in copies(s):
        c.wait()
    slot = lax.rem(s, MERGE_SLOTS)
    mixed = og_buf[slot].astype(F32) + od_buf[slot].astype(F32)
    x1 = x_ref[...] + _dot(mixed.astype(BF16), w_ref[...])
    x1_ref[...] = x1
    h2_ref[...] = _rms(x1, g_ref[...]).astype(BF16)


def _merge(o_gla, o_dn, x2d, w_out, g_mlp, tm=512):
    t, d = x2d.shape
    assert t // tm >= MERGE_SLOTS - 1
    row = lambda c: pl.BlockSpec((tm, d), lambda i, c=c: (i, c))
    return pl.pallas_call(
        _merge_body,
        grid=(t // tm,),
        in_specs=[
            pl.BlockSpec(memory_space=pl.ANY), pl.BlockSpec(memory_space=pl.ANY), row(0),
            pl.BlockSpec((d, d), lambda i: (0, 0), pipeline_mode=pl.Buffered(1)),
            pl.BlockSpec((1, d), lambda i: (0, 0)),
        ],
        out_specs=[row(0), row(0)],
        out_shape=[jax.ShapeDtypeStruct((t, d), F32), jax.ShapeDtypeStruct((t, d), BF16)],
        scratch_shapes=[
            pltpu.VMEM((MERGE_SLOTS, tm, d), BF16),
            pltpu.VMEM((MERGE_SLOTS, tm, d), BF16),
            pltpu.SemaphoreType.DMA((2, MERGE_SLOTS)),
        ],
        compiler_params=_cparams(("arbitrary",)),
        name="merge_out_proj",
    )(o_gla, o_dn, x2d, w_out, g_mlp)


def _mlp_body(h_ref, wu_ref, wd_ref, x_ref, o_ref):
    @pl.when(pl.program_id(1) == 0)
    def _():
        o_ref[...] = x_ref[...]

    mid = jnp.square(jnp.maximum(_dot(h_ref[...], wu_ref[...]), 0.0)).astype(BF16)
    o_ref[...] += _dot(mid, wd_ref[...])


def _mlp(h2, w_up, w_down, x1, tm=512, tf=1024):
    t, d = x1.shape
    ff = w_up.shape[1]
    return pl.pallas_call(
        _mlp_body,
        grid=(t // tm, ff // tf),
        in_specs=[
            pl.BlockSpec((tm, d), lambda i, j: (i, 0)),
            pl.BlockSpec((d, tf), lambda i, j: (0, j)),
            pl.BlockSpec((tf, d), lambda i, j: (j, 0)),
            pl.BlockSpec((tm, d), lambda i, j: (i, 0)),
        ],
        out_specs=pl.BlockSpec((tm, d), lambda i, j: (i, 0)),
        out_shape=jax.ShapeDtypeStruct((t, d), F32),
        compiler_params=_cparams(("parallel", "arbitrary")),
        name="relu2_mlp",
    )(h2, w_up, w_down, x1)


def _ple_body(x_ref, p_ref, wg_ref, wp_ref, gp_ref, gf_ref, o_ref):
    x2 = x_ref[...]
    h3 = _rms(x2, gp_ref[...]).astype(BF16)
    gate = _sigmoid(_dot(h3, wg_ref[...]))
    proj = _dot(p_ref[...].astype(BF16), wp_ref[...])
    o_ref[...] = _rms(x2 + gate * proj, gf_ref[...])


def _ple(x2, p2d, w_gate, w_proj, g_ple, g_final, tm=512):
    t, d = x2.shape
    pd = p2d.shape[1]
    return pl.pallas_call(
        _ple_body,
        grid=(t // tm,),
        in_specs=[
            pl.BlockSpec((tm, d), lambda i: (i, 0)),
            pl.BlockSpec((tm, pd), lambda i: (i, 0)),
            pl.BlockSpec((d, d), lambda i: (0, 0)),
            pl.BlockSpec((pd, d), lambda i: (0, 0)),
            pl.BlockSpec((1, d), lambda i: (0, 0)),
            pl.BlockSpec((1, d), lambda i: (0, 0)),
        ],
        out_specs=pl.BlockSpec((tm, d), lambda i: (i, 0)),
        out_shape=jax.ShapeDtypeStruct((t, d), F32),
        compiler_params=_cparams(("parallel",)),
        name="ple_final_norm",
    )(x2, p2d, w_gate, w_proj, g_ple, g_final)


def _layer(x, p_i, g_mix, w_in, gla_w2, gla_b, gla_norm, dn_conv, dn_a_log, dn_dt_bias, dn_norm,
           w_out, g_mlp, w_up, w_down, g_ple, w_ple_gate, w_ple_proj, g_out):
    b, s, d = x.shape
    t = b * s
    gla_qk = d // 2
    dn_qkv = 3 * d
    o_q, o_k = 0, gla_qk
    o_v = 2 * gla_qk
    o_g = o_v + d
    o_lr = o_g + d
    o_dn = o_lr + GLA_LOWRANK
    o_z = o_dn + dn_qkv
    o_a = o_z + d
    o_b = o_a + DN_HEADS
    o_ga = o_b + DN_HEADS
    o_gb = o_ga + d
    assert (o_q, o_k, o_v) == (0, gla_qk, 2 * gla_qk)
    w_in16 = w_in.astype(BF16)
    cs = lambda lo, n: w_in16[:, lo:lo + n]
    w_big = _repack(w_in16, ((o_lr, GLA_LOWRANK), (o_a, 2 * DN_HEADS)))
    w_small = jnp.concatenate(
        [cs(o_lr, GLA_LOWRANK), cs(o_a, DN_HEADS), cs(o_b, DN_HEADS),
         jnp.zeros((d, LANES - SM_GL), BF16)], axis=1)
    n_g, n_dn, n_z = o_g, o_lr, o_lr + dn_qkv
    n_ga = n_z + d
    n_gb = n_ga + d
    segments = ((0, ACT_NONE), (n_g, ACT_SILU), (n_dn, ACT_NONE), (n_z, ACT_SILU), (n_ga, ACT_SIGMOID))
    qkv_off = n_dn

    x2d = x.reshape(t, d)
    z2d, zs2d = _in_proj(x2d, g_mix.reshape(1, d), w_big, w_small, segments)
    z3 = z2d.reshape(b, s, -1)
    zs3 = zs2d.reshape(b, s, LANES)

    pad = lambda v: jnp.zeros((1, LANES), F32).at[0, SM_A:SM_B].set(v.astype(F32))
    g3, gt3 = _gates(zs3, pad(-jnp.exp(dn_a_log.astype(F32))), pad(dn_dt_bias))

    w2p = jnp.zeros((LANES, gla_qk), F32).at[:GLA_LOWRANK].set(gla_w2).astype(BF16)
    o_gla = _gla(z3, zs3, w2p, gla_b.reshape(1, -1).astype(F32), gla_norm.reshape(1, -1).astype(F32), d,
                 (n_g, n_ga))
    o_dn = _deltanet(z3, dn_conv.astype(F32), g3, gt3, dn_norm.reshape(1, -1).astype(F32), d, qkv_off,
                     (n_z, n_gb))

    x1, h2 = _merge(o_gla.reshape(t, d), o_dn.reshape(t, d), x2d, w_out.astype(BF16), g_mlp.reshape(1, d))
    x2 = _mlp(h2, w_up.astype(BF16), w_down.astype(BF16), x1)
    out = _ple(x2, p_i.reshape(t, -1), w_ple_gate.astype(BF16), w_ple_proj.astype(BF16),
               g_ple.reshape(1, d), g_out.reshape(1, d))
    return out.reshape(b, s, d)


def kernel(x, p, g_mix, w_in, gla_w2, gla_b, gla_norm, dn_conv, dn_a_log, dn_dt_bias, dn_norm,
           w_out, g_mlp, w_up, w_down, g_ple, w_ple_gate, w_ple_proj, g_final):
    depth = w_in.shape[0]
    assert depth == 1, "the final rms_norm is fused into the last layer's kernel"
    return _layer(x, p[0], g_mix[0], w_in[0], gla_w2[0], gla_b[0], gla_norm[0], dn_conv[0],
                  dn_a_log[0], dn_dt_bias[0], dn_norm[0], w_out[0], g_mlp[0], w_up[0], w_down[0],
                  g_ple[0], w_ple_gate[0], w_ple_proj[0], g_final)
```
